```python
import jax
import jax.numpy as jnp
from jax import lax
import numpy as np

D_MODEL = 2048
BATCH = 8
SEQ = 8192
DEPTH = 4

BRANCH_WIDTH = D_MODEL // 2
HEAD_DIM = 128
A_HEADS = BRANCH_WIDTH // HEAD_DIM
C_HEADS = BRANCH_WIDTH // HEAD_DIM
SHORT_CONV = 4
CONF_CONV = 31
CHUNK = 64
Q_BLOCK = 128
N_BRANCH = 3
NORM_EPS = 1e-6
SPLIT_SIZES = (3 * BRANCH_WIDTH, BRANCH_WIDTH, A_HEADS, A_HEADS,
               2 * BRANCH_WIDTH, BRANCH_WIDTH,
               3 * BRANCH_WIDTH, BRANCH_WIDTH, C_HEADS,
               N_BRANCH * D_MODEL)
N_IN = sum(SPLIT_SIZES)

kernel_name = 'hybrid_gdn_conformer_fox_parallel'


def rms_norm(x, w):
    x32 = x.astype(jnp.float32)
    y = x32 * lax.rsqrt(jnp.mean(x32 * x32, axis=-1, keepdims=True) + NORM_EPS)
    return y.astype(x.dtype) * w


def layer_norm(x, w, b):
    x32 = x.astype(jnp.float32)
    xc = x32 - jnp.mean(x32, axis=-1, keepdims=True)
    y = xc * lax.rsqrt(jnp.mean(xc * xc, axis=-1, keepdims=True) + NORM_EPS)
    return y.astype(x.dtype) * w + b


def l2_normalize(x):
    x32 = x.astype(jnp.float32)
    return x32 * lax.rsqrt(jnp.sum(x32 * x32, axis=-1, keepdims=True) + NORM_EPS)


def causal_depthwise_conv(x, w):
    k_width, ch = w.shape
    return lax.conv_general_dilated(
        x, w[:, None, :].astype(x.dtype), window_strides=(1,), padding=[(k_width - 1, 0)],
        dimension_numbers=('NWC', 'WIO', 'NWC'), feature_group_count=ch)


def gated_delta_rule(q, k, v, log_a, beta):
    f32 = jnp.float32
    bsz, seq, nh, dk = q.shape
    dv = v.shape[-1]
    nc = seq // CHUNK
    qc = q.astype(f32).reshape(bsz, nc, CHUNK, nh, dk) * (dk ** -0.5)
    kc = k.astype(f32).reshape(bsz, nc, CHUNK, nh, dk)
    vc = v.astype(f32).reshape(bsz, nc, CHUNK, nh, dv)
    bc = beta.astype(f32).reshape(bsz, nc, CHUNK, nh)
    g = jnp.cumsum(log_a.astype(f32).reshape(bsz, nc, CHUNK, nh), axis=2)
    gh = jnp.transpose(g, (0, 1, 3, 2))
    bh = jnp.transpose(bc, (0, 1, 3, 2))
    incl = jnp.tril(jnp.ones((CHUNK, CHUNK), dtype=bool))
    strict = jnp.tril(jnp.ones((CHUNK, CHUNK), dtype=bool), -1)
    diff = gh[..., :, None] - gh[..., None, :]
    decay = jnp.exp(jnp.where(incl, diff, -jnp.inf))
    kk = jnp.einsum('bnihd,bnjhd->bnhij', kc, kc)
    lower = jnp.where(strict, bh[..., :, None] * kk * decay, 0.0)
    eye = jnp.eye(CHUNK, dtype=f32)
    t_inv = lax.linalg.triangular_solve(lower + eye, jnp.broadcast_to(eye, lower.shape),
                                        left_side=True, lower=True)
    u0 = jnp.einsum('bnhij,bnjhe->bnihe', t_inv, vc * bc[..., None])
    w_cum = jnp.einsum('bnhij,bnjhd->bnihd', t_inv, kc * (bc * jnp.exp(g))[..., None])
    qk = jnp.einsum('bnihd,bnjhd->bnhij', qc, kc) * decay
    q_dec = qc * jnp.exp(g)[..., None]
    g_last = g[:, :, -1:, :]
    k_tail = kc * jnp.exp(g_last - g)[..., None]
    chunk_decay = jnp.exp(g_last[:, :, 0, :])

    def step(state, inp):
        u0_n, w_n, qk_n, qd_n, kt_n, dec_n = inp
        u = u0_n - jnp.einsum('bihd,bhde->bihe', w_n, state)
        o = jnp.einsum('bihd,bhde->bihe', qd_n, state) + jnp.einsum('bhij,bjhe->bihe', qk_n, u)
        state = state * dec_n[:, :, None, None] + jnp.einsum('bihd,bihe->bhde', kt_n, u)
        return state, o

    xs = tuple(jnp.moveaxis(t, 1, 0) for t in (u0, w_cum, qk, q_dec, k_tail, chunk_decay))
    state0 = jnp.zeros((bsz, nh, dk, dv), f32)
    _, out = lax.scan(step, state0, xs)
    return jnp.moveaxis(out, 0, 1).reshape(bsz, seq, nh, dv).astype(v.dtype)


def forgetting_attention(q, k, v, log_f):
    bsz, seq, nh, hd = q.shape
    nb = seq // Q_BLOCK
    scale = hd ** -0.5
    c = jnp.cumsum(log_f, axis=1)
    c_key = jnp.transpose(c, (0, 2, 1))
    pos = jnp.arange(seq, dtype=jnp.int32)
    q_blocks = jnp.moveaxis(q.reshape(bsz, nb, Q_BLOCK, nh, hd), 1, 0)
    c_blocks = jnp.moveaxis(c.reshape(bsz, nb, Q_BLOCK, nh), 1, 0)
    q_pos = pos.reshape(nb, Q_BLOCK)

    def block(args):
        q_blk, c_blk, qp = args
        s = jnp.einsum('bqhd,bkhd->bhqk', q_blk, k).astype(jnp.float32) * scale
        s = s + jnp.transpose(c_blk, (0, 2, 1))[..., None] - c_key[:, :, None, :]
        s = jnp.where(qp[:, None] >= pos[None, :], s, -jnp.inf)
        p = jax.nn.softmax(s, axis=-1).astype(v.dtype)
        return jnp.einsum('bhqk,bkhd->bqhd', p, v)

    out = lax.map(block, (q_blocks, c_blocks, q_pos))
    return jnp.moveaxis(out, 0, 1).reshape(bsz, seq, nh, hd)


def hybrid_layer(x, pre_w, post_w, w_in, conv_qkv_w, a_log, dt_bias, o_norm_w,
                 conv_w, conv_b, ln_w, ln_b, f_bias, w_branch, w_out):
    f32 = jnp.float32
    bsz, seq, _ = x.shape
    h = rms_norm(x, pre_w)
    proj = jnp.einsum('bsd,dn->bsn', h, w_in)
    split_at = [int(i) for i in np.cumsum(SPLIT_SIZES)[:-1]]
    (qkv_a, z_a, beta_a, alpha_a, glu_in, z_b,
     qkv_c, z_c, f_logit, gate_logit) = jnp.split(proj, split_at, axis=-1)

    qkv_a = jax.nn.silu(causal_depthwise_conv(qkv_a, conv_qkv_w))
    qkv_a = qkv_a.reshape(bsz, seq, 3, A_HEADS, HEAD_DIM)
    beta = jax.nn.sigmoid(beta_a.astype(f32))
    log_a = -jnp.exp(a_log.astype(f32)) * jax.nn.softplus(alpha_a.astype(f32) + dt_bias.astype(f32))
    o_a = gated_delta_rule(l2_normalize(qkv_a[:, :, 0]), l2_normalize(qkv_a[:, :, 1]),
                           qkv_a[:, :, 2], log_a, beta)
    o_a = rms_norm(o_a, o_norm_w) * jax.nn.silu(z_a.reshape(bsz, seq, A_HEADS, HEAD_DIM))
    y_a = o_a.reshape(bsz, seq, BRANCH_WIDTH)

    val, gate = jnp.split(glu_in, 2, axis=-1)
    u = val * jax.nn.sigmoid(gate)
    u = causal_depthwise_conv(u, conv_w) + conv_b
    u = jax.nn.silu(layer_norm(u, ln_w, ln_b))
    y_b = u * jax.nn.silu(z_b)

    qkv_c = qkv_c.reshape(bsz, seq, 3, C_HEADS, HEAD_DIM)
    log_f = jax.nn.log_sigmoid(f_logit.astype(f32) + f_bias.astype(f32))
    o_c = forgetting_attention(qkv_c[:, :, 0], qkv_c[:, :, 1], qkv_c[:, :, 2], log_f)
    y_c = o_c.reshape(bsz, seq, BRANCH_WIDTH) * jax.nn.silu(z_c)

    branches = jnp.einsum('nbsw,nwd->nbsd', jnp.stack([y_a, y_b, y_c]), w_branch)
    gates = jax.nn.sigmoid(gate_logit.reshape(bsz, seq, N_BRANCH, D_MODEL))
    merged = jnp.einsum('bsnd,nbsd->bsd', gates, branches)
    out = jnp.einsum('bsd,de->bse', merged, w_out)
    return x + rms_norm(out, post_w)


def _fwd_setup_inputs(seed: int = 0) -> dict:
    key = jax.random.key(seed)
    ks = jax.random.split(key, 16)
    f32 = jnp.float32

    def nrm(k, shape, scale):
        return scale * jax.random.normal(k, shape, f32)

    x = jax.random.normal(ks[0], (BATCH, SEQ, D_MODEL), f32)
    pre_norm_w = 1.0 + nrm(ks[1], (DEPTH, D_MODEL), 0.02)
    post_norm_w = 1.0 + nrm(ks[2], (DEPTH, D_MODEL), 0.02)
    w_in = nrm(ks[3], (DEPTH, D_MODEL, N_IN), D_MODEL ** -0.5)
    conv_qkv_w = nrm(ks[4], (DEPTH, SHORT_CONV, 3 * BRANCH_WIDTH), SHORT_CONV ** -0.5)
    a_log = jnp.log(jax.random.uniform(ks[5], (DEPTH, A_HEADS), f32, 1.0, 16.0))
    dt = jnp.exp(jax.random.uniform(ks[6], (DEPTH, A_HEADS), f32,
                                    float(np.log(1e-3)), float(np.log(1e-1))))
    dt_bias = dt + jnp.log(-jnp.expm1(-dt))
    o_norm_w = 1.0 + nrm(ks[7], (DEPTH, HEAD_DIM), 0.02)
    conv_w = nrm(ks[8], (DEPTH, CONF_CONV, BRANCH_WIDTH), CONF_CONV ** -0.5)
    conv_b = nrm(ks[9], (DEPTH, BRANCH_WIDTH), 0.01)
    ln_w = 1.0 + nrm(ks[10], (DEPTH, BRANCH_WIDTH), 0.02)
    ln_b = nrm(ks[11], (DEPTH, BRANCH_WIDTH), 0.01)
    f_bias = 3.0 + nrm(ks[12], (DEPTH, C_HEADS), 0.5)
    w_branch = nrm(ks[13], (DEPTH, N_BRANCH, BRANCH_WIDTH, D_MODEL), BRANCH_WIDTH ** -0.5)
    w_out = nrm(ks[14], (DEPTH, D_MODEL, D_MODEL), D_MODEL ** -0.5)
    return {'x': x, 'pre_norm_w': pre_norm_w, 'post_norm_w': post_norm_w, 'w_in': w_in,
            'conv_qkv_w': conv_qkv_w, 'a_log': a_log, 'dt_bias': dt_bias, 'o_norm_w': o_norm_w,
            'conv_w': conv_w, 'conv_b': conv_b, 'ln_w': ln_w, 'ln_b': ln_b, 'f_bias': f_bias,
            'w_branch': w_branch, 'w_out': w_out}


def _fwd_reference(x, pre_norm_w, post_norm_w, w_in, conv_qkv_w, a_log, dt_bias, o_norm_w,
              conv_w, conv_b, ln_w, ln_b, f_bias, w_branch, w_out):
    for l in range(DEPTH):
        x = hybrid_layer(x, pre_norm_w[l], post_norm_w[l], w_in[l], conv_qkv_w[l], a_log[l],
                         dt_bias[l], o_norm_w[l], conv_w[l], conv_b[l], ln_w[l], ln_b[l],
                         f_bias[l], w_branch[l], w_out[l])
    return x


import jax as _jax
import jax.numpy as _jnp

TWIN_FORMAT = 'train_step'
FWD_PARAMS = ['x', 'pre_norm_w', 'post_norm_w', 'w_in', 'conv_qkv_w', 'a_log', 'dt_bias', 'o_norm_w', 'conv_w', 'conv_b', 'ln_w', 'ln_b', 'f_bias', 'w_branch', 'w_out']
TWIN_WEIGHTS = ['pre_norm_w', 'post_norm_w', 'w_in', 'conv_qkv_w', 'a_log', 'dt_bias', 'o_norm_w', 'conv_w', 'conv_b', 'ln_w', 'ln_b', 'f_bias', 'w_branch', 'w_out']
TWIN_DIFF_INPUT = 'x'
TWIN_INPUTS = ['x', 'pre_norm_w', 'post_norm_w', 'w_in', 'conv_qkv_w', 'a_log', 'dt_bias', 'o_norm_w', 'conv_w', 'conv_b', 'ln_w', 'ln_b', 'f_bias', 'w_branch', 'w_out', 'loss_target', 'm_pre_norm_w', 'm_post_norm_w', 'm_w_in', 'm_conv_qkv_w', 'm_a_log', 'm_dt_bias', 'm_o_norm_w', 'm_conv_w', 'm_conv_b', 'm_ln_w', 'm_ln_b', 'm_f_bias', 'm_w_branch', 'm_w_out', 'v_pre_norm_w', 'v_post_norm_w', 'v_w_in', 'v_conv_qkv_w', 'v_a_log', 'v_dt_bias', 'v_o_norm_w', 'v_conv_w', 'v_conv_b', 'v_ln_w', 'v_ln_b', 'v_f_bias', 'v_w_branch', 'v_w_out']
TWIN_OUTPUTS = ['loss', 'grad_x', 'grad_pre_norm_w', 'grad_post_norm_w', 'grad_w_in', 'grad_conv_qkv_w', 'grad_a_log', 'grad_dt_bias', 'grad_o_norm_w', 'grad_conv_w', 'grad_conv_b', 'grad_ln_w', 'grad_ln_b', 'grad_f_bias', 'grad_w_branch', 'grad_w_out', 'delta_pre_norm_w', 'delta_post_norm_w', 'delta_w_in', 'delta_conv_qkv_w', 'delta_a_log', 'delta_dt_bias', 'delta_o_norm_w', 'delta_conv_w', 'delta_conv_b', 'delta_ln_w', 'delta_ln_b', 'delta_f_bias', 'delta_w_branch', 'delta_w_out', 'new_m_pre_norm_w', 'new_m_post_norm_w', 'new_m_w_in', 'new_m_conv_qkv_w', 'new_m_a_log', 'new_m_dt_bias', 'new_m_o_norm_w', 'new_m_conv_w', 'new_m_conv_b', 'new_m_ln_w', 'new_m_ln_b', 'new_m_f_bias', 'new_m_w_branch', 'new_m_w_out', 'new_v_pre_norm_w', 'new_v_post_norm_w', 'new_v_w_in', 'new_v_conv_qkv_w', 'new_v_a_log', 'new_v_dt_bias', 'new_v_o_norm_w', 'new_v_conv_w', 'new_v_conv_b', 'new_v_ln_w', 'new_v_ln_b', 'new_v_f_bias', 'new_v_w_branch', 'new_v_w_out']
TWIN_LEAF_KINDS = {'loss': 'loss', 'grad_x': 'grad_x', 'grad_pre_norm_w': 'grad_w', 'grad_post_norm_w': 'grad_w', 'grad_w_in': 'grad_w', 'grad_conv_qkv_w': 'grad_w', 'grad_a_log': 'grad_w', 'grad_dt_bias': 'grad_w', 'grad_o_norm_w': 'grad_w', 'grad_conv_w': 'grad_w', 'grad_conv_b': 'grad_w', 'grad_ln_w': 'grad_w', 'grad_ln_b': 'grad_w', 'grad_f_bias': 'grad_w', 'grad_w_branch': 'grad_w', 'grad_w_out': 'grad_w', 'delta_pre_norm_w': 'delta_w', 'delta_post_norm_w': 'delta_w', 'delta_w_in': 'delta_w', 'delta_conv_qkv_w': 'delta_w', 'delta_a_log': 'delta_w', 'delta_dt_bias': 'delta_w', 'delta_o_norm_w': 'delta_w', 'delta_conv_w': 'delta_w', 'delta_conv_b': 'delta_w', 'delta_ln_w': 'delta_w', 'delta_ln_b': 'delta_w', 'delta_f_bias': 'delta_w', 'delta_w_branch': 'delta_w', 'delta_w_out': 'delta_w', 'new_m_pre_norm_w': 'new_m', 'new_m_post_norm_w': 'new_m', 'new_m_w_in': 'new_m', 'new_m_conv_qkv_w': 'new_m', 'new_m_a_log': 'new_m', 'new_m_dt_bias': 'new_m', 'new_m_o_norm_w': 'new_m', 'new_m_conv_w': 'new_m', 'new_m_conv_b': 'new_m', 'new_m_ln_w': 'new_m', 'new_m_ln_b': 'new_m', 'new_m_f_bias': 'new_m', 'new_m_w_branch': 'new_m', 'new_m_w_out': 'new_m', 'new_v_pre_norm_w': 'new_v', 'new_v_post_norm_w': 'new_v', 'new_v_w_in': 'new_v', 'new_v_conv_qkv_w': 'new_v', 'new_v_a_log': 'new_v', 'new_v_dt_bias': 'new_v', 'new_v_o_norm_w': 'new_v', 'new_v_conv_w': 'new_v', 'new_v_conv_b': 'new_v', 'new_v_ln_w': 'new_v', 'new_v_ln_b': 'new_v', 'new_v_f_bias': 'new_v', 'new_v_w_branch': 'new_v', 'new_v_w_out': 'new_v'}


def _forward(args):
    return _fwd_reference(*[args[k] for k in FWD_PARAMS])


def _output_shape():
    def fwd():
        inp = _fwd_setup_inputs(0)
        return _fwd_reference(*[inp[k] for k in FWD_PARAMS])
    out = _jax.eval_shape(fwd)
    return out.shape, out.dtype

N_MICROBATCH = 1
ADAM_LR = 0.001
ADAM_B1 = 0.9
ADAM_B2 = 0.999
ADAM_EPS = 1e-08
ADAM_WD = 0.01
ADAM_STEP = 10
PER_EXAMPLE_BATCH_AXIS = {'x': 0, 'loss_target': 0}
SHARED_INPUTS = []
_WEIGHT_DTYPES = {'pre_norm_w': _jnp.float32, 'post_norm_w': _jnp.float32, 'w_in': _jnp.float32, 'conv_qkv_w': _jnp.float32, 'a_log': _jnp.float32, 'dt_bias': _jnp.float32, 'o_norm_w': _jnp.float32, 'conv_w': _jnp.float32, 'conv_b': _jnp.float32, 'ln_w': _jnp.float32, 'ln_b': _jnp.float32, 'f_bias': _jnp.float32, 'w_branch': _jnp.float32, 'w_out': _jnp.float32}
MOMENT_SCALE = {'pre_norm_w': 1.115005e+00, 'post_norm_w': 3.192610e+01, 'w_in': 3.826657e-01, 'conv_qkv_w': 8.851815e-01, 'a_log': 3.221201e+00, 'dt_bias': 3.108310e+00, 'o_norm_w': 6.182666e+00, 'conv_w': 4.660763e-01, 'conv_b': 3.221234e+00, 'ln_w': 1.289050e+00, 'ln_b': 2.005720e+00, 'f_bias': 1.746441e+00, 'w_branch': 9.312187e-01, 'w_out': 1.631565e+00}


def _to_microbatches(a, axis):
    t = _jnp.moveaxis(a, axis, 0)
    t = t.reshape((N_MICROBATCH, t.shape[0] // N_MICROBATCH) + t.shape[1:])
    return _jnp.moveaxis(t, 1, axis + 1)


def setup_inputs(seed: int = 0) -> dict:
    inp = _fwd_setup_inputs(seed)
    key = _jax.random.fold_in(_jax.random.key(seed), 7919)
    shape, _ = _output_shape()
    out = dict(inp)
    out["loss_target"] = _jax.random.normal(_jax.random.fold_in(key, 0), shape, _jnp.float32)
    for i, name in enumerate(TWIN_WEIGHTS):
        w = inp[name].astype(_jnp.float32)
        if MOMENT_SCALE is None:
            s = _jnp.sqrt(_jnp.mean(_jnp.square(w)) + 1e-30)
        else:
            s = MOMENT_SCALE[name]
        km, kv = _jax.random.split(_jax.random.fold_in(key, i + 1))
        out[name] = w
        out["m_" + name] = s * _jax.random.normal(km, w.shape, _jnp.float32)
        out["v_" + name] = (s * s) * _jax.random.uniform(kv, w.shape, _jnp.float32, 0.5, 1.5)
    if N_MICROBATCH > 1:
        for name, axis in PER_EXAMPLE_BATCH_AXIS.items():
            out[name] = _to_microbatches(out[name], axis)
    return {'x': out['x'], 'pre_norm_w': out['pre_norm_w'], 'post_norm_w': out['post_norm_w'], 'w_in': out['w_in'], 'conv_qkv_w': out['conv_qkv_w'], 'a_log': out['a_log'], 'dt_bias': out['dt_bias'], 'o_norm_w': out['o_norm_w'], 'conv_w': out['conv_w'], 'conv_b': out['conv_b'], 'ln_w': out['ln_w'], 'ln_b': out['ln_b'], 'f_bias': out['f_bias'], 'w_branch': out['w_branch'], 'w_out': out['w_out'], 'loss_target': out['loss_target'], 'm_pre_norm_w': out['m_pre_norm_w'], 'm_post_norm_w': out['m_post_norm_w'], 'm_w_in': out['m_w_in'], 'm_conv_qkv_w': out['m_conv_qkv_w'], 'm_a_log': out['m_a_log'], 'm_dt_bias': out['m_dt_bias'], 'm_o_norm_w': out['m_o_norm_w'], 'm_conv_w': out['m_conv_w'], 'm_conv_b': out['m_conv_b'], 'm_ln_w': out['m_ln_w'], 'm_ln_b': out['m_ln_b'], 'm_f_bias': out['m_f_bias'], 'm_w_branch': out['m_w_branch'], 'm_w_out': out['m_w_out'], 'v_pre_norm_w': out['v_pre_norm_w'], 'v_post_norm_w': out['v_post_norm_w'], 'v_w_in': out['v_w_in'], 'v_conv_qkv_w': out['v_conv_qkv_w'], 'v_a_log': out['v_a_log'], 'v_dt_bias': out['v_dt_bias'], 'v_o_norm_w': out['v_o_norm_w'], 'v_conv_w': out['v_conv_w'], 'v_conv_b': out['v_conv_b'], 'v_ln_w': out['v_ln_w'], 'v_ln_b': out['v_ln_b'], 'v_f_bias': out['v_f_bias'], 'v_w_branch': out['v_w_branch'], 'v_w_out': out['v_w_out']}


def _loss(weights, diff, rest, loss_target):
    with _jax.named_scope("forward"):
        args = {**rest, TWIN_DIFF_INPUT: diff, **{k: w.astype(_WEIGHT_DTYPES[k]) for k, w in weights.items()}}
        y = _forward(args)
    with _jax.named_scope("loss_head"):
        err = _jnp.square(y.astype(_jnp.float32) - loss_target)
        return 0.5 * _jnp.sum(_jnp.mean(err, axis=-1)) if err.ndim else 0.5 * err


def _adamw(w, g, m, v):
    m = ADAM_B1 * m + (1.0 - ADAM_B1) * g
    v = ADAM_B2 * v + (1.0 - ADAM_B2) * _jnp.square(g)
    m_hat = m / (1.0 - ADAM_B1 ** ADAM_STEP)
    v_hat = v / (1.0 - ADAM_B2 ** ADAM_STEP)
    delta = -ADAM_LR * (m_hat / (_jnp.sqrt(v_hat) + ADAM_EPS) + ADAM_WD * w)
    return delta, m, v


def reference(x, pre_norm_w, post_norm_w, w_in, conv_qkv_w, a_log, dt_bias, o_norm_w, conv_w, conv_b, ln_w, ln_b, f_bias, w_branch, w_out, loss_target, m_pre_norm_w, m_post_norm_w, m_w_in, m_conv_qkv_w, m_a_log, m_dt_bias, m_o_norm_w, m_conv_w, m_conv_b, m_ln_w, m_ln_b, m_f_bias, m_w_branch, m_w_out, v_pre_norm_w, v_post_norm_w, v_w_in, v_conv_qkv_w, v_a_log, v_dt_bias, v_o_norm_w, v_conv_w, v_conv_b, v_ln_w, v_ln_b, v_f_bias, v_w_branch, v_w_out):
    given = dict(x=x, pre_norm_w=pre_norm_w, post_norm_w=post_norm_w, w_in=w_in, conv_qkv_w=conv_qkv_w, a_log=a_log, dt_bias=dt_bias, o_norm_w=o_norm_w, conv_w=conv_w, conv_b=conv_b, ln_w=ln_w, ln_b=ln_b, f_bias=f_bias, w_branch=w_branch, w_out=w_out, loss_target=loss_target, m_pre_norm_w=m_pre_norm_w, m_post_norm_w=m_post_norm_w, m_w_in=m_w_in, m_conv_qkv_w=m_conv_qkv_w, m_a_log=m_a_log, m_dt_bias=m_dt_bias, m_o_norm_w=m_o_norm_w, m_conv_w=m_conv_w, m_conv_b=m_conv_b, m_ln_w=m_ln_w, m_ln_b=m_ln_b, m_f_bias=m_f_bias, m_w_branch=m_w_branch, m_w_out=m_w_out, v_pre_norm_w=v_pre_norm_w, v_post_norm_w=v_post_norm_w, v_w_in=v_w_in, v_conv_qkv_w=v_conv_qkv_w, v_a_log=v_a_log, v_dt_bias=v_dt_bias, v_o_norm_w=v_o_norm_w, v_conv_w=v_conv_w, v_conv_b=v_conv_b, v_ln_w=v_ln_w, v_ln_b=v_ln_b, v_f_bias=v_f_bias, v_w_branch=v_w_branch, v_w_out=v_w_out)
    weights = {n: given[n] for n in TWIN_WEIGHTS}
    shared = {n: given[n] for n in SHARED_INPUTS}
    per_example = {n: given[n] for n in ['x']}
    grad_fn = _jax.value_and_grad(_loss, argnums=(0, 1))

    def one_microbatch(ex, loss_target):
        ex = dict(ex)
        diff = ex.pop(TWIN_DIFF_INPUT)
        return grad_fn(weights, diff, {**shared, **ex}, loss_target)

    if N_MICROBATCH == 1:
        loss, (grad_w, grad_x) = one_microbatch(per_example, given["loss_target"])
    else:
        def body(carry, xs):
            loss_sum, grad_sum = carry
            l_k, (gw_k, gx_k) = one_microbatch(xs[0], xs[1])
            with _jax.named_scope("update"):
                return (loss_sum + l_k, _jax.tree.map(_jnp.add, grad_sum, gw_k)), gx_k

        init = (_jnp.zeros((), _jnp.float32), _jax.tree.map(_jnp.zeros_like, weights))
        (loss, grad_w), grad_x = _jax.lax.scan(body, init, (per_example, given["loss_target"]))
    with _jax.named_scope("update"):
        delta_w, new_m, new_v = {}, {}, {}
        for n in TWIN_WEIGHTS:
            delta_w[n], new_m[n], new_v[n] = _adamw(weights[n], grad_w[n], given["m_" + n], given["v_" + n])
    return (loss, grad_x, *[grad_w[n] for n in TWIN_WEIGHTS], *[delta_w[n] for n in TWIN_WEIGHTS],
            *[new_m[n] for n in TWIN_WEIGHTS], *[new_v[n] for n in TWIN_WEIGHTS])
```

```python
import functools

import jax
import jax.numpy as jnp
from jax import lax
from jax.experimental import pallas as pl
from jax.experimental.pallas import tpu as pltpu

F32 = jnp.float32
BF16 = jnp.bfloat16
MXU_DTYPE = jnp.bfloat16

D_MODEL = 2048
DEPTH = 4
BRANCH = 1024
HEAD_DIM = 128
HEADS = 8
CHUNK = 64
SHORT_CONV = 4
CONF_CONV = 31
N_BRANCH = 3
NORM_EPS = 1e-6
N_IN = 17432

OFF_GATE = 0
OFF_QKVA = 6144
OFF_ZA = 9216
OFF_VAL = 10240
OFF_GLUG = 11264
OFF_ZB = 12288
OFF_QKVC = 13312
OFF_ZC = 16384
OFF_SMALL = 17408
N_PAD = 17920
LANE = 128
BETA_LANE, ALPHA_LANE, FORGET_LANE = 0, 8, 16

ADAM_LR = 0.001
ADAM_B1 = 0.9
ADAM_B2 = 0.999
ADAM_EPS = 1e-08
ADAM_WD = 0.01
ADAM_STEP = 10

VMEM_LIMIT = 56 * 1024 * 1024

NN = (((1,), (0,)), ((), ()))
NT = (((1,), (1,)), ((), ()))
TN = (((0,), (0,)), ((), ()))
MESH = pl.DeviceIdType.MESH


def _params(sem=None):
    return pltpu.CompilerParams(dimension_semantics=sem, vmem_limit_bytes=VMEM_LIMIT)


def _dot(a, b, dims=NN):
    return lax.dot_general(a.astype(MXU_DTYPE), b.astype(MXU_DTYPE), dims, preferred_element_type=F32)


def _dot_hi(a, b, dims=NN):
    return lax.dot_general(a, b, dims, precision=lax.Precision.HIGHEST, preferred_element_type=F32)


def _sigmoid(x):
    return 1.0 / (1.0 + jnp.exp(-x))


def _silu(x):
    return x * _sigmoid(x)


def _softplus(x):
    return jnp.maximum(x, 0.0) + jnp.log(1.0 + jnp.exp(-jnp.abs(x)))


def _colsel(m, j):
    lane = lax.broadcasted_iota(jnp.int32, m.shape, 1)
    return jnp.sum(jnp.where(lane == j, m, 0.0), axis=1, keepdims=True)


def _rowsel(m, j):
    sub = lax.broadcasted_iota(jnp.int32, m.shape, 0)
    return jnp.sum(jnp.where(sub == j, m, 0.0), axis=0, keepdims=True)


def _row_specs(rows, tile):
    return [pl.BlockSpec((tile, w), functools.partial(lambda i, cb: (i, cb), cb=cb)) for (_, w, cb) in rows]


def _rowwise(name, fn, rows, params, outs, tile):
    seq = rows[0][0].shape[0]
    tile = min(tile, seq)
    n_in = len(rows) + len(params)

    def body(*refs):
        res = fn(*[r[...] for r in refs[:n_in]])
        for o_ref, r in zip(refs[n_in:], res):
            o_ref[...] = r.astype(o_ref.dtype)

    return pl.pallas_call(
        body, name=name, grid=(seq // tile,),
        in_specs=_row_specs(rows, tile) + [pl.BlockSpec(p.shape, lambda i: (0, 0)) for p in params],
        out_specs=[pl.BlockSpec((tile, w), lambda i: (i, 0)) for (w, _) in outs],
        out_shape=[jax.ShapeDtypeStruct((seq, w), dt) for (w, dt) in outs],
        compiler_params=_params(("parallel",)),
    )(*[r[0] for r in rows], *params)


def _rowwise_bwd(name, fn, rows, params, cts, row_grads, tile):
    seq = rows[0][0].shape[0]
    tile = min(tile, seq)
    nr, npar, nct = len(rows), len(params), len(cts)
    n_in = nr + npar

    def body(*refs):
        vals = [r[...] for r in refs[:n_in]]
        res, vjp = jax.vjp(fn, *vals)
        grads = vjp(tuple(c[...].astype(r.dtype) for c, r in zip(refs[n_in:n_in + nct], res)))
        outs = refs[n_in + nct:]
        k = 0
        for idx, dt in enumerate(row_grads):
            if dt is not None:
                outs[k][...] = grads[idx].astype(dt)
                k += 1
        first = pl.program_id(0) == 0
        for j in range(npar):
            g = grads[nr + j].astype(F32)
            o_ref = outs[k + j]

            @pl.when(first)
            def _(o_ref=o_ref, g=g):
                o_ref[...] = g

            @pl.when(jnp.logical_not(first))
            def _(o_ref=o_ref, g=g):
                o_ref[...] += g

    want = [(rows[i][1], dt) for i, dt in enumerate(row_grads) if dt is not None]
    return pl.pallas_call(
        body, name=name, grid=(seq // tile,),
        in_specs=(_row_specs(rows, tile) + [pl.BlockSpec(p.shape, lambda i: (0, 0)) for p in params]
                  + [pl.BlockSpec((tile, c.shape[1]), lambda i: (i, 0)) for c in cts]),
        out_specs=([pl.BlockSpec((tile, w), lambda i: (i, 0)) for (w, _) in want]
                   + [pl.BlockSpec(p.shape, lambda i: (0, 0)) for p in params]),
        out_shape=([jax.ShapeDtypeStruct((seq, w), dt) for (w, dt) in want]
                   + [jax.ShapeDtypeStruct(p.shape, F32) for p in params]),
        compiler_params=_params(("arbitrary",)),
    )(*[r[0] for r in rows], *params, *cts)


def _rms(x, w):
    x = x.astype(F32)
    return x * lax.rsqrt(jnp.mean(x * x, axis=-1, keepdims=True) + NORM_EPS) * w


def _rms_pre_fn(x, w):
    return (_rms(x, w),)


def _rms_pre_res_fn(x, w):
    return (_rms(x, w), x)


def _rms_post_fn(out, x, w):
    return (x + _rms(out, w),)


def _rms_only_fn(out, w):
    return (_rms(out, w),)


def _ln_gate_fn(u, z, w, b):
    u = u.astype(F32)
    uc = u - jnp.mean(u, axis=-1, keepdims=True)
    y = uc * lax.rsqrt(jnp.mean(uc * uc, axis=-1, keepdims=True) + NORM_EPS) * w + b
    return (_silu(y) * _silu(z.astype(F32)),)


def _merge_fn(g0, g1, g2, b0, b1, b2):
    return (_sigmoid(g0) * b0 + _sigmoid(g1) * b1 + _sigmoid(g2) * b2,)


def _matmul(name, a, b, mode, out_dtype, tm, tn, tk):
    if mode == "nn":
        (m, kc), n = a.shape, b.shape[1]
    elif mode == "nt":
        (m, kc), n = a.shape, b.shape[0]
    else:
        (kc, m), n = a.shape, b.shape[1]
    tm, tn, tk = min(tm, m), min(tn, n), min(tk, kc)
    nk = kc // tk
    assert m % tm == 0 and n % tn == 0 and kc % tk == 0, (name, a.shape, b.shape)
    dims = {"nn": NN, "nt": NT, "tn": TN}[mode]
    a_spec = (pl.BlockSpec((tk, tm), lambda j, i, k: (k, i)) if mode == "tn"
              else pl.BlockSpec((tm, tk), lambda j, i, k: (i, k)))
    b_spec = (pl.BlockSpec((tn, tk), lambda j, i, k: (j, k)) if mode == "nt"
              else pl.BlockSpec((tk, tn), lambda j, i, k: (k, j)))
    use_acc = nk > 1 and out_dtype != F32

    def body(a_ref, b_ref, o_ref, *acc):
        p = _dot(a_ref[...], b_ref[...], dims)
        if nk == 1:
            o_ref[...] = p.astype(out_dtype)
            return
        k = pl.program_id(2)
        dst = acc[0] if use_acc else o_ref

        @pl.when(k == 0)
        def _():
            dst[...] = p

        @pl.when(k > 0)
        def _():
            dst[...] += p

        if use_acc:
            @pl.when(k == nk - 1)
            def _():
                o_ref[...] = dst[...].astype(out_dtype)

    return pl.pallas_call(
        body, name=name, grid=(n // tn, m // tm, nk),
        in_specs=[a_spec, b_spec],
        out_specs=pl.BlockSpec((tm, tn), lambda j, i, k: (i, j)),
        out_shape=jax.ShapeDtypeStruct((m, n), out_dtype),
        scratch_shapes=[pltpu.VMEM((tm, tn), F32)] if use_acc else [],
        compiler_params=_params(("parallel", "parallel", "arbitrary")),
    )(a, b)


HALO = 32
CONV_TC = 256
CONV_T = 1024


def _conv_fwd(name, x, x_off, ch, w, b, k_width, gate_off=None):
    seq = x.shape[0]
    t_blk = min(CONV_T, seq)
    tc = CONV_TC
    hb = t_blk // HALO
    xcb = x_off // tc
    has_gate = gate_off is not None

    def body(*refs):
        if has_gate:
            xm_ref, xh_ref, gm_ref, gh_ref, w_ref, b_ref, y_ref, win = refs
        else:
            xm_ref, xh_ref, w_ref, b_ref, y_ref, win = refs
        t = pl.program_id(1)
        xm, xh = xm_ref[...], xh_ref[...]
        if has_gate:
            xm = xm * _sigmoid(gm_ref[...])
            xh = xh * _sigmoid(gh_ref[...])
        win[0:HALO, :] = jnp.where(t == 0, 0.0, xh)
        win[HALO:HALO + t_blk, :] = xm
        acc = jnp.broadcast_to(b_ref[...], (t_blk, tc))
        for k in range(k_width):
            acc = acc + w_ref[k:k + 1, :] * win[HALO - (k_width - 1) + k:HALO - (k_width - 1) + k + t_blk, :]
        y_ref[...] = acc

    main = lambda off: pl.BlockSpec((t_blk, tc), lambda c, t: (t, off + c))
    halo = lambda off: pl.BlockSpec((HALO, tc), lambda c, t: (jnp.maximum(t * hb - 1, 0), off + c))
    ins, specs = [x, x], [main(xcb), halo(xcb)]
    if has_gate:
        gcb = gate_off // tc
        ins += [x, x]
        specs += [main(gcb), halo(gcb)]
    ins += [w, b]
    specs += [pl.BlockSpec((w.shape[0], tc), lambda c, t: (0, c)), pl.BlockSpec((1, tc), lambda c, t: (0, c))]
    return pl.pallas_call(
        body, name=name, grid=(ch // tc, seq // t_blk), in_specs=specs,
        out_specs=pl.BlockSpec((t_blk, tc), lambda c, t: (t, c)),
        out_shape=jax.ShapeDtypeStruct((seq, ch), F32),
        scratch_shapes=[pltpu.VMEM((HALO + t_blk, tc), F32)],
        compiler_params=_params(("parallel", "arbitrary")),
    )(*ins)


def _conv_bwd(name, dy, x, x_off, ch, w, k_width, gate_off=None):
    seq = x.shape[0]
    t_blk = min(CONV_T, seq)
    tc = CONV_TC
    hb = t_blk // HALO
    nt = seq // t_blk
    xcb = x_off // tc
    has_gate = gate_off is not None
    kp = w.shape[0]

    def body(*refs):
        if has_gate:
            dm_ref, dh_ref, xm_ref, xh_ref, gm_ref, gh_ref, w_ref, dv_ref, dg_ref, dw_ref, db_ref, winx, wind = refs
        else:
            dm_ref, dh_ref, xm_ref, xh_ref, w_ref, dx_ref, dw_ref, db_ref, winx, wind = refs
        t = pl.program_id(1)
        xm, xh = xm_ref[...], xh_ref[...]
        if has_gate:
            sg = _sigmoid(gm_ref[...])
            um = xm * sg
            uh = xh * _sigmoid(gh_ref[...])
        else:
            um, uh = xm, xh
        winx[0:HALO, :] = jnp.where(t == nt - 1, 0.0, uh)
        winx[HALO:HALO + t_blk, :] = um
        dm = dm_ref[...]
        wind[0:t_blk, :] = dm
        wind[t_blk:t_blk + HALO, :] = jnp.where(t == 0, 0.0, dh_ref[...])
        du = jnp.zeros((t_blk, tc), F32)
        for k in range(k_width):
            du = du + w_ref[k:k + 1, :] * wind[k_width - 1 - k:k_width - 1 - k + t_blk, :]
        if has_gate:
            dv_ref[...] = (du * sg).astype(dv_ref.dtype)
            dg_ref[...] = (du * xm * sg * (1.0 - sg)).astype(dg_ref.dtype)
        else:
            dx_ref[...] = du.astype(dx_ref.dtype)

        @pl.when(t == 0)
        def _():
            dw_ref[...] = jnp.zeros_like(dw_ref)
            db_ref[...] = jnp.zeros_like(db_ref)

        for k in range(k_width):
            s0 = HALO - (k_width - 1) + k
            dw_ref[k:k + 1, :] += jnp.sum(dm * winx[s0:s0 + t_blk, :], axis=0, keepdims=True)
        db_ref[...] += jnp.sum(dm, axis=0, keepdims=True)

    rt = lambda t: nt - 1 - t
    main = lambda off: pl.BlockSpec((t_blk, tc), lambda c, t: (rt(t), off + c))
    past = lambda off: pl.BlockSpec((HALO, tc), lambda c, t: (jnp.maximum(rt(t) * hb - 1, 0), off + c))
    future = pl.BlockSpec((HALO, tc), lambda c, t: (jnp.minimum((rt(t) + 1) * hb, seq // HALO - 1), c))
    ins, specs = [dy, dy, x, x], [main(0), future, main(xcb), past(xcb)]
    if has_gate:
        gcb = gate_off // tc
        ins += [x, x]
        specs += [main(gcb), past(gcb)]
    ins += [w]
    specs += [pl.BlockSpec((kp, tc), lambda c, t: (0, c))]
    blk = pl.BlockSpec((t_blk, tc), lambda c, t: (rt(t), c))
    n_dx = 2 if has_gate else 1
    return pl.pallas_call(
        body, name=name, grid=(ch // tc, nt), in_specs=specs,
        out_specs=[blk] * n_dx + [pl.BlockSpec((kp, tc), lambda c, t: (0, c)), pl.BlockSpec((1, tc), lambda c, t: (0, c))],
        out_shape=[jax.ShapeDtypeStruct((seq, ch), BF16)] * n_dx + [jax.ShapeDtypeStruct((kp, ch), F32),
                                                                    jax.ShapeDtypeStruct((1, ch), F32)],
        scratch_shapes=[pltpu.VMEM((HALO + t_blk, tc), F32), pltpu.VMEM((HALO + t_blk, tc), F32)],
        compiler_params=_params(("parallel", "arbitrary")),
    )(*ins)


@jax.custom_vjp
def _inv_unit_lower(low):
    n = low.shape[0]
    eye = (lax.broadcasted_iota(jnp.int32, (n, n), 0) == lax.broadcasted_iota(jnp.int32, (n, n), 1)).astype(F32)
    acc = eye - low
    pw = low
    steps = 1
    while steps * 2 < n:
        pw = _dot_hi(pw, pw)
        acc = acc + _dot_hi(acc, pw)
        steps *= 2
    return acc


def _inv_fwd(low):
    t = _inv_unit_lower(low)
    return t, t


def _inv_bwd(t, dt):
    return (-_dot_hi(_dot_hi(t, dt, TN), t, NT),)


_inv_unit_lower.defvjp(_inv_fwd, _inv_bwd)


def _gdr_chunk(cqkv, z, sm, alog, dtb, onw, state):
    c = cqkv.shape[0]
    ri = lax.broadcasted_iota(jnp.int32, (c, c), 0)
    ci = lax.broadcasted_iota(jnp.int32, (c, c), 1)
    incl, strict = ri >= ci, ri > ci
    beta_all = _sigmoid(sm)
    la_all = -jnp.exp(alog) * _softplus(sm + dtb)
    g_cols = _dot_hi(incl.astype(F32), la_all)
    g_rows = _dot_hi(la_all, (ri <= ci).astype(F32), TN)
    g_end = jnp.sum(la_all, axis=0, keepdims=True)
    act = _silu(cqkv)
    ys, states = [], []
    for h in range(HEADS):
        sl = lambda base: slice(base + h * HEAD_DIM, base + (h + 1) * HEAD_DIM)
        q, k, v = act[:, sl(0)], act[:, sl(BRANCH)], act[:, sl(2 * BRANCH)]
        q = q * lax.rsqrt(jnp.sum(q * q, axis=-1, keepdims=True) + NORM_EPS) * (HEAD_DIM ** -0.5)
        k = k * lax.rsqrt(jnp.sum(k * k, axis=-1, keepdims=True) + NORM_EPS)
        beta = _colsel(beta_all, BETA_LANE + h)
        g = _colsel(g_cols, ALPHA_LANE + h)
        g_row = _rowsel(g_rows, ALPHA_LANE + h)
        g_last = _colsel(g_end, ALPHA_LANE + h)
        decay = jnp.where(incl, jnp.exp(jnp.where(incl, g - g_row, 0.0)), 0.0)
        low = jnp.where(strict, beta * _dot(k, k, NT) * decay, 0.0)
        t_inv = _inv_unit_lower(low)
        eg = jnp.exp(g)
        u0 = _dot(t_inv, v * beta)
        w_cum = _dot(t_inv, k * (beta * eg))
        qk = _dot(q, k, NT) * decay
        s_h = state[h]
        u = u0 - _dot(w_cum, s_h)
        o = _dot(q * eg, s_h) + _dot(qk, u)
        states.append(s_h * jnp.exp(g_last) + _dot(k * jnp.exp(g_last - g), u, TN))
        o = o * lax.rsqrt(jnp.mean(o * o, axis=-1, keepdims=True) + NORM_EPS) * onw
        ys.append(o * _silu(z[:, sl(0)]))
    return jnp.concatenate(ys, axis=1), jnp.stack(states)


def _gdr_specs(nc, order):
    return [
        pl.BlockSpec((CHUNK, 3 * BRANCH), lambda n: (order(n), 0)),
        pl.BlockSpec((CHUNK, BRANCH), lambda n: (order(n), OFF_ZA // BRANCH)),
        pl.BlockSpec((CHUNK, LANE), lambda n: (order(n), OFF_SMALL // LANE)),
        pl.BlockSpec((1, LANE), lambda n: (0, 0)),
        pl.BlockSpec((1, LANE), lambda n: (0, 0)),
        pl.BlockSpec((1, LANE), lambda n: (0, 0)),
    ]


def _gdr_fwd(name, cqkv, proj, alog, dtb, onw):
    seq = cqkv.shape[0]
    nc = seq // CHUNK

    def body(c_ref, z_ref, sm_ref, al_ref, dt_ref, on_ref, y_ref, st_ref, state):
        @pl.when(pl.program_id(0) == 0)
        def _():
            state[...] = jnp.zeros_like(state)

        s_in = state[...]
        st_ref[0] = s_in
        y, s_out = _gdr_chunk(c_ref[...], z_ref[...], sm_ref[...], al_ref[...], dt_ref[...], on_ref[...], s_in)
        y_ref[...] = y.astype(y_ref.dtype)
        state[...] = s_out

    return pl.pallas_call(
        body, name=name, grid=(nc,), in_specs=_gdr_specs(nc, lambda n: n),
        out_specs=[pl.BlockSpec((CHUNK, BRANCH), lambda n: (n, 0)),
                   pl.BlockSpec((1, HEADS, HEAD_DIM, HEAD_DIM), lambda n: (n, 0, 0, 0))],
        out_shape=[jax.ShapeDtypeStruct((seq, BRANCH), BF16),
                   jax.ShapeDtypeStruct((nc, HEADS, HEAD_DIM, HEAD_DIM), F32)],
        scratch_shapes=[pltpu.VMEM((HEADS, HEAD_DIM, HEAD_DIM), F32)],
        compiler_params=_params(("arbitrary",)),
    )(cqkv, proj, proj, alog, dtb, onw)


def _gdr_bwd(name, dy, states, cqkv, proj, alog, dtb, onw):
    seq = cqkv.shape[0]
    nc = seq // CHUNK
    rev = lambda n: nc - 1 - n

    def body(c_ref, z_ref, sm_ref, al_ref, dt_ref, on_ref, dy_ref, st_ref,
             dc_ref, dz_ref, dsm_ref, dal_ref, ddt_ref, don_ref, dstate):
        first = pl.program_id(0) == 0

        @pl.when(first)
        def _():
            dstate[...] = jnp.zeros_like(dstate)
            dal_ref[...] = jnp.zeros_like(dal_ref)
            ddt_ref[...] = jnp.zeros_like(ddt_ref)
            don_ref[...] = jnp.zeros_like(don_ref)

        _, vjp = jax.vjp(_gdr_chunk, c_ref[...], z_ref[...], sm_ref[...], al_ref[...], dt_ref[...], on_ref[...],
                         st_ref[0])
        dc, dz, dsm, dal, ddt, don, ds = vjp((dy_ref[...].astype(F32), dstate[...]))
        dc_ref[...] = dc
        dz_ref[...] = dz.astype(dz_ref.dtype)
        dsm_ref[...] = dsm
        dal_ref[...] += dal
        ddt_ref[...] += ddt
        don_ref[...] += don
        dstate[...] = ds

    small = pl.BlockSpec((1, LANE), lambda n: (0, 0))
    return pl.pallas_call(
        body, name=name, grid=(nc,),
        in_specs=_gdr_specs(nc, rev) + [pl.BlockSpec((CHUNK, BRANCH), lambda n: (rev(n), 0)),
                                        pl.BlockSpec((1, HEADS, HEAD_DIM, HEAD_DIM), lambda n: (rev(n), 0, 0, 0))],
        out_specs=[pl.BlockSpec((CHUNK, 3 * BRANCH), lambda n: (rev(n), 0)),
                   pl.BlockSpec((CHUNK, BRANCH), lambda n: (rev(n), 0)),
                   pl.BlockSpec((CHUNK, LANE), lambda n: (rev(n), 0)), small, small, small],
        out_shape=[jax.ShapeDtypeStruct((seq, 3 * BRANCH), F32), jax.ShapeDtypeStruct((seq, BRANCH), BF16),
                   jax.ShapeDtypeStruct((seq, LANE), F32)] + [jax.ShapeDtypeStruct((1, LANE), F32)] * 3,
        scratch_shapes=[pltpu.VMEM((HEADS, HEAD_DIM, HEAD_DIM), F32)],
        compiler_params=_params(("arbitrary",)),
    )(cqkv, proj, proj, alog, dtb, onw, dy, states)


GATE_T = 512
ATT_T = 512


def _fox_gate_fwd(name, proj, fb):
    seq = proj.shape[0]
    tb = min(GATE_T, seq)

    def body(sm_ref, fb_ref, c_ref, ct_ref):
        tri = (lax.broadcasted_iota(jnp.int32, (tb, tb), 0) >= lax.broadcasted_iota(jnp.int32, (tb, tb), 1)).astype(F32)
        carry = jnp.zeros((1, LANE), F32)
        for i in range(seq // tb):
            lf = -_softplus(-(sm_ref[i * tb:(i + 1) * tb, :] + fb_ref[...]))
            cb = _dot_hi(tri, lf) + carry
            c_ref[i * tb:(i + 1) * tb, :] = cb
            ct_ref[:, i * tb:(i + 1) * tb] = cb.T
            carry = carry + jnp.sum(lf, axis=0, keepdims=True)

    return pl.pallas_call(
        body, name=name, grid=(1,),
        in_specs=[pl.BlockSpec((seq, LANE), lambda i: (0, OFF_SMALL // LANE)), pl.BlockSpec((1, LANE), lambda i: (0, 0))],
        out_specs=[pl.BlockSpec((seq, LANE), lambda i: (0, 0)), pl.BlockSpec((LANE, seq), lambda i: (0, 0))],
        out_shape=[jax.ShapeDtypeStruct((seq, LANE), F32), jax.ShapeDtypeStruct((LANE, seq), F32)],
        compiler_params=_params(("arbitrary",)),
    )(proj, fb)


def _fox_gate_bwd(name, dck, proj, fb):
    seq = proj.shape[0]
    tb = min(GATE_T, seq)
    nb = seq // tb

    def body(d_ref, sm_ref, fb_ref, o_ref, dfb_ref, pad):
        tri = (lax.broadcasted_iota(jnp.int32, (tb, tb), 0) >= lax.broadcasted_iota(jnp.int32, (tb, tb), 1)).astype(F32)
        pad[...] = jnp.zeros_like(pad)
        carry = jnp.zeros((HEADS, 1), F32)
        dfb = jnp.zeros((1, LANE), F32)
        for i in reversed(range(nb)):
            blk = d_ref[:, i * tb:(i + 1) * tb]
            pad[FORGET_LANE:FORGET_LANE + HEADS, :] = _dot_hi(blk, tri) + carry
            carry = carry + jnp.sum(blk, axis=1, keepdims=True)
            x = sm_ref[i * tb:(i + 1) * tb, :] + fb_ref[...]
            dsm = pad[...].T * _sigmoid(-x)
            o_ref[i * tb:(i + 1) * tb, :] = dsm
            dfb = dfb + jnp.sum(dsm, axis=0, keepdims=True)
        dfb_ref[...] = dfb

    return pl.pallas_call(
        body, name=name, grid=(1,),
        in_specs=[pl.BlockSpec((HEADS, seq), lambda i: (0, 0)), pl.BlockSpec((seq, LANE), lambda i: (0, OFF_SMALL // LANE)),
                  pl.BlockSpec((1, LANE), lambda i: (0, 0))],
        out_specs=[pl.BlockSpec((seq, LANE), lambda i: (0, 0)), pl.BlockSpec((1, LANE), lambda i: (0, 0))],
        out_shape=[jax.ShapeDtypeStruct((seq, LANE), F32), jax.ShapeDtypeStruct((1, LANE), F32)],
        scratch_shapes=[pltpu.VMEM((LANE, tb), F32)],
        compiler_params=_params(("arbitrary",)),
    )(dck, proj, fb)


def _att_scores(q, k, cq, ck_row, i, j, tq):
    s = _dot(q, k, NT) * (HEAD_DIM ** -0.5) + (cq - ck_row)
    qpos = i * tq + lax.broadcasted_iota(jnp.int32, s.shape, 0)
    kpos = j * tq + lax.broadcasted_iota(jnp.int32, s.shape, 1)
    return jnp.where(qpos >= kpos, s, -jnp.inf)


def _attn_fwd(name, proj, c, ct3):
    seq = proj.shape[0]
    tq = min(ATT_T, seq)
    nq = seq // tq
    qb, zb = OFF_QKVC // HEAD_DIM, OFF_ZC // HEAD_DIM

    def body(q_ref, k_ref, v_ref, z_ref, c_ref, ct_ref, y_ref, o_ref, lse_ref, m_s, l_s, acc_s):
        h, i, j = pl.program_id(0), pl.program_id(1), pl.program_id(2)

        @pl.when(j == 0)
        def _():
            m_s[...] = jnp.full_like(m_s, -jnp.inf)
            l_s[...] = jnp.zeros_like(l_s)
            acc_s[...] = jnp.zeros_like(acc_s)

        @pl.when(j <= i)
        def _():
            cq = _colsel(c_ref[...], FORGET_LANE + h)
            s = _att_scores(q_ref[...], k_ref[...], cq, ct_ref[0], i, j, tq)
            m_old = m_s[...]
            m_new = jnp.maximum(m_old, jnp.max(s, axis=1, keepdims=True))
            p = jnp.exp(s - m_new)
            alpha = jnp.exp(m_old - m_new)
            l_s[...] = alpha * l_s[...] + jnp.sum(p, axis=1, keepdims=True)
            p_hi = p.astype(MXU_DTYPE).astype(F32)
            acc_s[...] = alpha * acc_s[...] + _dot(p_hi, v_ref[...]) + _dot(p - p_hi, v_ref[...])
            m_s[...] = m_new

        @pl.when(j == i)
        def _():
            o = acc_s[...] / l_s[...]
            o_ref[...] = o
            y_ref[...] = (o * _silu(z_ref[...])).astype(y_ref.dtype)
            lse_ref[0] = jnp.broadcast_to(m_s[...] + jnp.log(l_s[...]), (tq, LANE))

    kv = lambda off: pl.BlockSpec((tq, HEAD_DIM), lambda h, i, j: (jnp.minimum(j, i), off + h))
    qs = lambda off: pl.BlockSpec((tq, HEAD_DIM), lambda h, i, j: (i, off + h))
    return pl.pallas_call(
        body, name=name, grid=(HEADS, nq, nq),
        in_specs=[qs(qb), kv(qb + HEADS), kv(qb + 2 * HEADS), qs(zb),
                  pl.BlockSpec((tq, LANE), lambda h, i, j: (i, 0)),
                  pl.BlockSpec((1, 1, tq), lambda h, i, j: (FORGET_LANE + h, 0, jnp.minimum(j, i)))],
        out_specs=[qs(0), qs(0), pl.BlockSpec((1, tq, LANE), lambda h, i, j: (h, i, 0))],
        out_shape=[jax.ShapeDtypeStruct((seq, BRANCH), BF16), jax.ShapeDtypeStruct((seq, BRANCH), F32),
                   jax.ShapeDtypeStruct((HEADS, seq, LANE), F32)],
        scratch_shapes=[pltpu.VMEM((tq, 1), F32), pltpu.VMEM((tq, 1), F32), pltpu.VMEM((tq, HEAD_DIM), F32)],
        compiler_params=_params(("parallel", "parallel", "arbitrary")),
    )(proj, proj, proj, proj, c, ct3)


def _attn_dq(name, dy, o, lse, proj, c, ct3):
    seq = proj.shape[0]
    tq = min(ATT_T, seq)
    nq = seq // tq
    qb, zb = OFF_QKVC // HEAD_DIM, OFF_ZC // HEAD_DIM

    def body(q_ref, k_ref, v_ref, z_ref, c_ref, ct_ref, dy_ref, o_ref, lse_ref, dq_ref, dz_ref, do_s, dl_s, acc_s):
        h, i, j = pl.program_id(0), pl.program_id(1), pl.program_id(2)

        @pl.when(j == 0)
        def _():
            z = z_ref[...]
            sg = _sigmoid(z)
            dyv = dy_ref[...].astype(F32)
            do = dyv * z * sg
            do_s[...] = do
            dl_s[...] = jnp.sum(do.astype(MXU_DTYPE).astype(F32) * o_ref[...], axis=1, keepdims=True)
            dz_ref[...] = (dyv * o_ref[...] * sg * (1.0 + z * (1.0 - sg))).astype(dz_ref.dtype)
            acc_s[...] = jnp.zeros_like(acc_s)

        @pl.when(j <= i)
        def _():
            cq = _colsel(c_ref[...], FORGET_LANE + h)
            s = _att_scores(q_ref[...], k_ref[...], cq, ct_ref[0], i, j, tq)
            p = jnp.exp(s - jnp.max(lse_ref[0], axis=1, keepdims=True))
            dp = _dot(do_s[...], v_ref[...], NT)
            ds = p * (dp - dl_s[...])
            acc_s[...] += _dot(ds, k_ref[...])

        @pl.when(j == i)
        def _():
            dq_ref[...] = (acc_s[...] * (HEAD_DIM ** -0.5)).astype(dq_ref.dtype)

    kv = lambda off: pl.BlockSpec((tq, HEAD_DIM), lambda h, i, j: (jnp.minimum(j, i), off + h))
    qs = lambda off: pl.BlockSpec((tq, HEAD_DIM), lambda h, i, j: (i, off + h))
    return pl.pallas_call(
        body, name=name, grid=(HEADS, nq, nq),
        in_specs=[qs(qb), kv(qb + HEADS), kv(qb + 2 * HEADS), qs(zb),
                  pl.BlockSpec((tq, LANE), lambda h, i, j: (i, 0)),
                  pl.BlockSpec((1, 1, tq), lambda h, i, j: (FORGET_LANE + h, 0, jnp.minimum(j, i))),
                  qs(0), qs(0), pl.BlockSpec((1, tq, LANE), lambda h, i, j: (h, i, 0))],
        out_specs=[qs(0), qs(0)],
        out_shape=[jax.ShapeDtypeStruct((seq, BRANCH), BF16)] * 2,
        scratch_shapes=[pltpu.VMEM((tq, HEAD_DIM), F32), pltpu.VMEM((tq, 1), F32), pltpu.VMEM((tq, HEAD_DIM), F32)],
        compiler_params=_params(("parallel", "parallel", "arbitrary")),
    )(proj, proj, proj, proj, c, ct3, dy, o, lse)


def _attn_dkv(name, dy, o, lse, proj, c, ct3):
    seq = proj.shape[0]
    tq = min(ATT_T, seq)
    nq = seq // tq
    qb, zb = OFF_QKVC // HEAD_DIM, OFF_ZC // HEAD_DIM

    def body(q_ref, k_ref, v_ref, z_ref, c_ref, ct_ref, dy_ref, o_ref, lse_ref, dk_ref, dv_ref, dc_ref, dk_s, dv_s, dc_s):
        h, j, i = pl.program_id(0), pl.program_id(1), pl.program_id(2)

        @pl.when(i == 0)
        def _():
            dk_s[...] = jnp.zeros_like(dk_s)
            dv_s[...] = jnp.zeros_like(dv_s)
            dc_s[...] = jnp.zeros_like(dc_s)

        @pl.when(i >= j)
        def _():
            z = z_ref[...]
            do = dy_ref[...].astype(F32) * _silu(z)
            delta = jnp.sum(do.astype(MXU_DTYPE).astype(F32) * o_ref[...], axis=1, keepdims=True)
            cq = _colsel(c_ref[...], FORGET_LANE + h)
            s = _att_scores(q_ref[...], k_ref[...], cq, ct_ref[0], i, j, tq)
            p = jnp.exp(s - jnp.max(lse_ref[0], axis=1, keepdims=True))
            dv_s[...] += _dot(p, do, TN)
            ds = p * (_dot(do, v_ref[...], NT) - delta)
            dk_s[...] += _dot(ds, q_ref[...], TN)
            dc_s[...] -= jnp.sum(ds, axis=0, keepdims=True)

        @pl.when(i == nq - 1)
        def _():
            dk_ref[...] = (dk_s[...] * (HEAD_DIM ** -0.5)).astype(dk_ref.dtype)
            dv_ref[...] = dv_s[...].astype(dv_ref.dtype)
            dc_ref[0] = dc_s[...]

    kv = lambda off: pl.BlockSpec((tq, HEAD_DIM), lambda h, j, i: (j, off + h))
    qs = lambda off: pl.BlockSpec((tq, HEAD_DIM), lambda h, j, i: (jnp.maximum(i, j), off + h))
    return pl.pallas_call(
        body, name=name, grid=(HEADS, nq, nq),
        in_specs=[qs(qb), kv(qb + HEADS), kv(qb + 2 * HEADS), qs(zb),
                  pl.BlockSpec((tq, LANE), lambda h, j, i: (jnp.maximum(i, j), 0)),
                  pl.BlockSpec((1, 1, tq), lambda h, j, i: (FORGET_LANE + h, 0, j)),
                  qs(0), qs(0), pl.BlockSpec((1, tq, LANE), lambda h, j, i: (h, jnp.maximum(i, j), 0))],
        out_specs=[kv(0), kv(0), pl.BlockSpec((1, 1, tq), lambda h, j, i: (h, 0, j))],
        out_shape=[jax.ShapeDtypeStruct((seq, BRANCH), BF16)] * 2 + [jax.ShapeDtypeStruct((HEADS, 1, seq), F32)],
        scratch_shapes=[pltpu.VMEM((tq, HEAD_DIM), F32), pltpu.VMEM((tq, HEAD_DIM), F32), pltpu.VMEM((1, tq), F32)],
        compiler_params=_params(("parallel", "parallel", "arbitrary")),
    )(proj, proj, proj, proj, c, ct3, dy, o, lse)


def _loss_head(name, y, target):
    seq, d = y.shape
    tile = min(256, seq)

    def body(y_ref, t_ref, dy_ref, l_ref):
        err = y_ref[...] - t_ref[...]
        dy_ref[...] = err / d

        @pl.when(pl.program_id(0) == 0)
        def _():
            l_ref[...] = jnp.zeros_like(l_ref)

        l_ref[...] += 0.5 * jnp.sum(jnp.mean(err * err, axis=-1, keepdims=True), axis=0, keepdims=True)

    return pl.pallas_call(
        body, name=name, grid=(seq // tile,),
        in_specs=[pl.BlockSpec((tile, d), lambda i: (i, 0))] * 2,
        out_specs=[pl.BlockSpec((tile, d), lambda i: (i, 0)), pl.BlockSpec((8, LANE), lambda i: (0, 0))],
        out_shape=[jax.ShapeDtypeStruct((seq, d), F32), jax.ShapeDtypeStruct((8, LANE), F32)],
        compiler_params=_params(("arbitrary",)),
    )(y, target)


def _adamw(name, w, m, v, g_parts, tile):
    rows, cols = w.shape
    tile = min(tile, rows)
    assert rows % tile == 0, (name, w.shape)
    n_g = len(g_parts)

    def body(*refs):
        w_ref, m_ref, v_ref = refs[:3]
        g_refs = refs[3:3 + n_g]
        g_ref, d_ref, nm_ref, nv_ref = refs[3 + n_g:]
        g = None
        for r in g_refs:
            parts = [r[...]] if len(r.shape) == 2 else [r[i] for i in range(r.shape[0])]
            for p in parts:
                g = p.astype(F32) if g is None else g + p.astype(F32)
        m_new = ADAM_B1 * m_ref[...] + (1.0 - ADAM_B1) * g
        v_new = ADAM_B2 * v_ref[...] + (1.0 - ADAM_B2) * (g * g)
        m_hat = m_new / (1.0 - ADAM_B1 ** ADAM_STEP)
        v_hat = v_new / (1.0 - ADAM_B2 ** ADAM_STEP)
        g_ref[...] = g
        d_ref[...] = -ADAM_LR * (m_hat / (jnp.sqrt(v_hat) + ADAM_EPS) + ADAM_WD * w_ref[...])
        nm_ref[...] = m_new
        nv_ref[...] = v_new

    blk = pl.BlockSpec((tile, cols), lambda i: (i, 0))
    g_specs = [blk if p.ndim == 2 else pl.BlockSpec((p.shape[0], tile, cols), lambda i: (0, i, 0)) for p in g_parts]
    return pl.pallas_call(
        body, name=name, grid=(rows // tile,), in_specs=[blk] * 3 + g_specs, out_specs=[blk] * 4,
        out_shape=[jax.ShapeDtypeStruct((rows, cols), F32)] * 4,
        compiler_params=_params(("parallel",)),
    )(w, m, v, *g_parts)


_ORIG_SEGMENTS = (
    ("qkv_a", 0, 3072), ("z_a", 3072, 1024), ("beta", 4096, 8), ("alpha", 4104, 8), ("glu", 4112, 2048),
    ("z_b", 6160, 1024), ("qkv_c", 7184, 3072), ("z_c", 10256, 1024), ("forget", 11280, 8), ("gate", 11288, 6144))
_PAD_ORDER = ("gate", "qkv_a", "z_a", "glu", "z_b", "qkv_c", "z_c", "beta", "alpha", "forget")


def _pad_cols(w):
    seg = {n: w[..., s:s + k] for n, s, k in _ORIG_SEGMENTS}
    fill = jnp.zeros(w.shape[:-1] + (N_PAD - N_IN,), w.dtype)
    return jnp.concatenate([seg[n] for n in _PAD_ORDER] + [fill], axis=-1)


def _unpad_cols(g):
    off, seg = 0, {}
    widths = {n: k for n, _, k in _ORIG_SEGMENTS}
    for n in _PAD_ORDER:
        seg[n] = g[..., off:off + widths[n]]
        off += widths[n]
    return jnp.concatenate([seg[n] for n, _, _ in _ORIG_SEGMENTS], axis=-1)


def _lane_row(vals, lane0):
    return jnp.pad(vals.astype(F32), (lane0, LANE - HEADS - lane0))[None]


def _layer_fwd(x, p):
    seq = x.shape[0]
    h = _rowwise("rms_pre", _rms_pre_fn, [(x, D_MODEL, 0)], [p["pre_w"]], [(D_MODEL, BF16)], 256)[0]
    proj = _matmul("mm_in", h, p["w_in"], "nn", F32, 512, 1280, 2048)
    ca = _conv_fwd("conv_a", proj, OFF_QKVA, 3 * BRANCH, p["w4"], jnp.zeros((1, 3 * BRANCH), F32), SHORT_CONV)
    y_a, states = _gdr_fwd("gdr_fwd", ca, proj, p["alog"], p["dtb"], p["onw"])
    u2 = _conv_fwd("conv_b", proj, OFF_VAL, BRANCH, p["w31"], p["cb"], CONF_CONV, gate_off=OFF_GLUG)
    y_b = _rowwise("ln_gate", _ln_gate_fn, [(u2, BRANCH, 0), (proj, BRANCH, OFF_ZB // BRANCH)],
                   [p["ln_w"], p["ln_b"]], [(BRANCH, BF16)], 256)[0]
    c, ct = _fox_gate_fwd("fox_gate", proj, p["fb"])
    ct3 = ct.reshape(LANE, 1, seq)
    y_c, o_c, lse = _attn_fwd("attn_fwd", proj, c, ct3)
    ys = (y_a, y_b, y_c)
    br = [_matmul("mm_br", ys[n], p["wbr"][n], "nn", F32, 512, 2048, 1024) for n in range(N_BRANCH)]
    merged = _rowwise("merge", _merge_fn, [(proj, D_MODEL, n) for n in range(N_BRANCH)] + [(b, D_MODEL, 0) for b in br],
                      [], [(D_MODEL, BF16)], 256)[0]
    out = _matmul("mm_out", merged, p["wout"], "nn", F32, 512, 2048, 2048)
    x_new = _rowwise("rms_post", _rms_post_fn, [(out, D_MODEL, 0), (x, D_MODEL, 0)], [p["post_w"]],
                     [(D_MODEL, F32)], 256)[0]
    saved = dict(x=x, h=h, proj=proj, ca=ca, states=states, u2=u2, c=c, ct3=ct3, o_c=o_c, lse=lse, ys=ys, br=br,
                 merged=merged, out=out)
    return x_new, saved


def _layer_bwd(dxn, p, sv):
    x, proj = sv["x"], sv["proj"]
    seq = x.shape[0]
    g = {}
    d_out, g["post_w"] = _rowwise_bwd("rms_post_bwd", _rms_only_fn, [(sv["out"], D_MODEL, 0)], [p["post_w"]], [dxn],
                                      [BF16], 256)
    d_merged = _matmul("mm_out_dx", d_out, p["wout"], "nt", F32, 512, 2048, 2048)
    g["wout"] = _matmul("mm_out_dw", sv["merged"], d_out, "tn", F32, 1024, 1024, 512)
    rows = [(proj, D_MODEL, n) for n in range(N_BRANCH)] + [(b, D_MODEL, 0) for b in sv["br"]]
    d_gl0, d_gl1, d_gl2, d_b0, d_b1, d_b2 = _rowwise_bwd("merge_bwd", _merge_fn, rows, [], [d_merged], [BF16] * 6, 128)
    d_br = (d_b0, d_b1, d_b2)
    dys = [_matmul("mm_br_dx", d_br[n], p["wbr"][n], "nt", BF16, 512, 1024, 2048) for n in range(N_BRANCH)]
    g["wbr"] = jnp.stack([_matmul("mm_br_dw", sv["ys"][n], d_br[n], "tn", F32, 1024, 1024, 512)
                          for n in range(N_BRANCH)])
    dq, dzc = _attn_dq("attn_dq", dys[2], sv["o_c"], sv["lse"], proj, sv["c"], sv["ct3"])
    dk, dv, dck = _attn_dkv("attn_dkv", dys[2], sv["o_c"], sv["lse"], proj, sv["c"], sv["ct3"])
    dsm_c, g["fb"] = _fox_gate_bwd("fox_gate_bwd", dck.reshape(HEADS, seq), proj, p["fb"])
    du2, dzb, g["ln_w"], g["ln_b"] = _rowwise_bwd(
        "ln_gate_bwd", _ln_gate_fn, [(sv["u2"], BRANCH, 0), (proj, BRANCH, OFF_ZB // BRANCH)], [p["ln_w"], p["ln_b"]],
        [dys[1]], [F32, BF16], 256)
    dval, dgate, g["w31"], g["cb"] = _conv_bwd("conv_b_bwd", du2, proj, OFF_VAL, BRANCH, p["w31"], CONF_CONV,
                                               gate_off=OFF_GLUG)
    dca, dza, dsm_a, g["alog"], g["dtb"], g["onw"] = _gdr_bwd("gdr_bwd", dys[0], sv["states"], sv["ca"], proj,
                                                              p["alog"], p["dtb"], p["onw"])
    dqkva, g["w4"], _ = _conv_bwd("conv_a_bwd", dca, proj, OFF_QKVA, 3 * BRANCH, p["w4"], SHORT_CONV)
    d_small = jnp.pad((dsm_a + dsm_c).astype(BF16), ((0, 0), (0, N_PAD - OFF_SMALL - LANE)))
    d_proj = jnp.concatenate([d_gl0, d_gl1, d_gl2, dqkva, dza, dval, dgate, dzb, dq, dk, dv, dzc, d_small], axis=1)
    dh = _matmul("mm_in_dx", d_proj, p["w_in"], "nt", F32, 512, 2048, 1280)
    g["w_in"] = _matmul("mm_in_dw", sv["h"], d_proj, "tn", F32, 2048, 640, 512)
    dx, g["pre_w"] = _rowwise_bwd("rms_pre_bwd", _rms_pre_res_fn, [(x, D_MODEL, 0)], [p["pre_w"]], [dh, dxn], [F32], 256)
    return dx, g


N_CHIPS = 4
N_DEV = 8
HBM_SPEC = pl.BlockSpec(memory_space=pltpu.HBM)


def _mesh_pos():
    return lax.axis_index("x"), lax.axis_index("y"), lax.axis_index("c")


def _other_chips(x, y):
    return [(1 - x, y), (x, 1 - y), (1 - x, 1 - y)]


def _allgather_chips(name, arrs):
    n = len(arrs)

    def body(*refs):
        ins, outs = refs[:n], refs[n:2 * n]
        send_sems, recv_sems, local_sems = refs[2 * n:]
        x, y, c = _mesh_pos()
        me = 2 * x + y

        def remote(a, j, slot):
            px, py = _other_chips(x, y)[j]
            return pltpu.make_async_remote_copy(
                src_ref=ins[a], dst_ref=outs[a].at[slot], send_sem=send_sems.at[3 * a + j],
                recv_sem=recv_sems.at[3 * a + j], device_id=(px, py, c), device_id_type=MESH)

        local = [pltpu.make_async_copy(ins[a], outs[a].at[me], local_sems.at[a]) for a in range(n)]
        for a in range(n):
            local[a].start()
            for j in range(3):
                remote(a, j, me).start()
        for a in range(n):
            for j, (px, py) in enumerate(_other_chips(x, y)):
                remote(a, j, 2 * px + py).wait_recv()
        for a in range(n):
            for j in range(3):
                remote(a, j, me).wait_send()
            local[a].wait()

    return pl.pallas_call(
        body, name=name, in_specs=[HBM_SPEC] * n, out_specs=[HBM_SPEC] * n,
        out_shape=[jax.ShapeDtypeStruct((N_CHIPS,) + a.shape, a.dtype) for a in arrs],
        scratch_shapes=[pltpu.SemaphoreType.DMA((3 * n,)), pltpu.SemaphoreType.DMA((3 * n,)),
                        pltpu.SemaphoreType.DMA((n,))],
    )(*arrs)


def _reduce_scatter_chips(name, arrs):
    n = len(arrs)

    def body(*refs):
        ins, outs = refs[:n], refs[n:2 * n]
        send_sems, recv_sems, local_sems = refs[2 * n:]
        x, y, c = _mesh_pos()
        me = 2 * x + y

        def remote(a, j, slot):
            px, py = _other_chips(x, y)[j]
            return pltpu.make_async_remote_copy(
                src_ref=ins[a].at[2 * px + py], dst_ref=outs[a].at[slot], send_sem=send_sems.at[3 * a + j],
                recv_sem=recv_sems.at[3 * a + j], device_id=(px, py, c), device_id_type=MESH)

        local = [pltpu.make_async_copy(ins[a].at[me], outs[a].at[me], local_sems.at[a]) for a in range(n)]
        for a in range(n):
            local[a].start()
            for j in range(3):
                remote(a, j, me).start()
        for a in range(n):
            for j, (px, py) in enumerate(_other_chips(x, y)):
                remote(a, j, 2 * px + py).wait_recv()
        for a in range(n):
            for j in range(3):
                remote(a, j, me).wait_send()
            local[a].wait()

    return pl.pallas_call(
        body, name=name, in_specs=[HBM_SPEC] * n, out_specs=[HBM_SPEC] * n,
        out_shape=[jax.ShapeDtypeStruct(a.shape, a.dtype) for a in arrs],
        scratch_shapes=[pltpu.SemaphoreType.DMA((3 * n,)), pltpu.SemaphoreType.DMA((3 * n,)),
                        pltpu.SemaphoreType.DMA((n,))],
    )(*arrs)


def _exchange_cores(name, arrs):
    n = len(arrs)

    def body(*refs):
        ins, outs = refs[:n], refs[n:2 * n]
        send_sems, recv_sems = refs[2 * n:]
        x, y, c = _mesh_pos()
        copies = [pltpu.make_async_remote_copy(
            src_ref=ins[a], dst_ref=outs[a], send_sem=send_sems.at[a], recv_sem=recv_sems.at[a],
            device_id=(x, y, 1 - c), device_id_type=MESH) for a in range(n)]
        for cp in copies:
            cp.start()
        for cp in copies:
            cp.wait_recv()
        for cp in copies:
            cp.wait_send()

    return pl.pallas_call(
        body, name=name, in_specs=[HBM_SPEC] * n, out_specs=[HBM_SPEC] * n,
        out_shape=[jax.ShapeDtypeStruct(a.shape, a.dtype) for a in arrs],
        scratch_shapes=[pltpu.SemaphoreType.DMA((n,)), pltpu.SemaphoreType.DMA((n,))],
    )(*arrs)


def _allgather_devices(name, buf):
    def body(in_ref, out_ref, send_sems, recv_sems):
        x, y, c = _mesh_pos()
        me = 4 * x + 2 * y + c
        out_ref[me] = in_ref[...]

        def remote(k, slot):
            peer = (x ^ (k >> 2), y ^ ((k >> 1) & 1), c ^ (k & 1))
            return pltpu.make_async_remote_copy(
                src_ref=in_ref, dst_ref=out_ref.at[slot], send_sem=send_sems.at[k - 1], recv_sem=recv_sems.at[k - 1],
                device_id=peer, device_id_type=MESH)

        for k in range(1, N_DEV):
            remote(k, me).start()
        for k in range(1, N_DEV):
            remote(k, me ^ k).wait_recv()
        for k in range(1, N_DEV):
            remote(k, me).wait_send()

    vmem = pl.BlockSpec(memory_space=pltpu.VMEM)
    return pl.pallas_call(
        body, name=name, in_specs=[vmem], out_specs=vmem,
        out_shape=jax.ShapeDtypeStruct((N_DEV,) + buf.shape, buf.dtype),
        scratch_shapes=[pltpu.SemaphoreType.DMA((N_DEV - 1,)), pltpu.SemaphoreType.DMA((N_DEV - 1,))],
    )(buf)


def _sum_slots(name, a, tile):
    n, rows, cols = a.shape
    tile = min(tile, rows)
    assert rows % tile == 0, (name, a.shape)

    def body(a_ref, o_ref):
        acc = a_ref[0].astype(F32)
        for i in range(1, n):
            acc = acc + a_ref[i].astype(F32)
        o_ref[...] = acc

    return pl.pallas_call(
        body, name=name, grid=(rows // tile,),
        in_specs=[pl.BlockSpec((n, tile, cols), lambda i: (0, i, 0))],
        out_specs=pl.BlockSpec((tile, cols), lambda i: (i, 0)),
        out_shape=jax.ShapeDtypeStruct((rows, cols), F32),
        compiler_params=_params(("parallel",)),
    )(a)


_SMALL = (
    ("pre_norm_w", (DEPTH, D_MODEL)), ("post_norm_w", (DEPTH, D_MODEL)), ("a_log", (DEPTH, HEADS)),
    ("dt_bias", (DEPTH, HEADS)), ("o_norm_w", (DEPTH, HEAD_DIM)), ("conv_b", (DEPTH, BRANCH)), ("ln_w", (DEPTH, BRANCH)),
    ("ln_b", (DEPTH, BRANCH)), ("f_bias", (DEPTH, HEADS)), ("conv_qkv_w", (DEPTH, SHORT_CONV, 3 * BRANCH)),
    ("conv_w", (DEPTH, CONF_CONV, BRANCH)), ("loss", (1,)))
_SHARDED_SMALL = {"conv_qkv_w": 3 * BRANCH // N_CHIPS, "conv_w": BRANCH // N_CHIPS}
_WEIGHTS = ("pre_norm_w", "post_norm_w", "w_in", "conv_qkv_w", "a_log", "dt_bias", "o_norm_w", "conv_w", "conv_b",
            "ln_w", "ln_b", "f_bias", "w_branch", "w_out")


def _pack(parts):
    flat = jnp.concatenate([p.reshape(-1).astype(F32) for p in parts])
    rows = -(-flat.shape[0] // (8 * LANE)) * 8
    return jnp.pad(flat, (0, rows * LANE - flat.shape[0])).reshape(rows, LANE)


def _unpack(buf, shapes):
    flat, out, off = buf.reshape(-1), [], 0
    for shp in shapes:
        size = 1
        for s in shp:
            size *= s
        out.append(flat[off:off + size].reshape(shp))
        off += size
    return out


def kernel(x, pre_norm_w, post_norm_w, w_in, conv_qkv_w, a_log, dt_bias, o_norm_w, conv_w, conv_b, ln_w, ln_b, f_bias, w_branch, w_out, loss_target, m_pre_norm_w, m_post_norm_w, m_w_in, m_conv_qkv_w, m_a_log, m_dt_bias, m_o_norm_w, m_conv_w, m_conv_b, m_ln_w, m_ln_b, m_f_bias, m_w_branch, m_w_out, v_pre_norm_w, v_post_norm_w, v_w_in, v_conv_qkv_w, v_a_log, v_dt_bias, v_o_norm_w, v_conv_w, v_conv_b, v_ln_w, v_ln_b, v_f_bias, v_w_branch, v_w_out):
    weights = dict(pre_norm_w=pre_norm_w, post_norm_w=post_norm_w, w_in=w_in, conv_qkv_w=conv_qkv_w, a_log=a_log,
                   dt_bias=dt_bias, o_norm_w=o_norm_w, conv_w=conv_w, conv_b=conv_b, ln_w=ln_w, ln_b=ln_b, f_bias=f_bias,
                   w_branch=w_branch, w_out=w_out)
    mom1 = dict(pre_norm_w=m_pre_norm_w, post_norm_w=m_post_norm_w, w_in=m_w_in, conv_qkv_w=m_conv_qkv_w, a_log=m_a_log,
                dt_bias=m_dt_bias, o_norm_w=m_o_norm_w, conv_w=m_conv_w, conv_b=m_conv_b, ln_w=m_ln_w, ln_b=m_ln_b,
                f_bias=m_f_bias, w_branch=m_w_branch, w_out=m_w_out)
    mom2 = dict(pre_norm_w=v_pre_norm_w, post_norm_w=v_post_norm_w, w_in=v_w_in, conv_qkv_w=v_conv_qkv_w, a_log=v_a_log,
                dt_bias=v_dt_bias, o_norm_w=v_o_norm_w, conv_w=v_conv_w, conv_b=v_conv_b, ln_w=v_ln_w, ln_b=v_ln_b,
                f_bias=v_f_bias, w_branch=v_w_branch, w_out=v_w_out)
    chip = 2 * lax.axis_index("x") + lax.axis_index("y")
    seq = x.shape[1]

    gw_in, gw_br, gw_out, g_c4, g_c31 = _allgather_chips(
        "allgather_weights", [w_in.astype(BF16), w_branch.astype(BF16), w_out.astype(BF16), conv_qkv_w, conv_w])
    cat_last = lambda g: jnp.concatenate([g[s] for s in range(N_CHIPS)], axis=-1)
    w_in_full = _pad_cols(cat_last(gw_in))
    w_br_full = cat_last(gw_br)
    w_out_full = jnp.concatenate([gw_out[s] for s in range(N_CHIPS)], axis=1)
    c4_full = jnp.pad(cat_last(g_c4), ((0, 0), (0, 8 - SHORT_CONV), (0, 0)))
    c31_full = jnp.pad(cat_last(g_c31), ((0, 0), (0, 32 - CONF_CONV), (0, 0)))

    layers = []
    for l in range(DEPTH):
        layers.append(dict(
            pre_w=pre_norm_w[l][None], post_w=post_norm_w[l][None], w_in=w_in_full[l], w4=c4_full[l],
            alog=_lane_row(a_log[l], ALPHA_LANE), dtb=_lane_row(dt_bias[l], ALPHA_LANE), onw=o_norm_w[l][None],
            w31=c31_full[l], cb=conv_b[l][None], ln_w=ln_w[l][None], ln_b=ln_b[l][None],
            fb=_lane_row(f_bias[l], FORGET_LANE), wbr=w_br_full[l], wout=w_out_full[l]))

    act = x[0]
    saved = []
    for l in range(DEPTH):
        act, sv = _layer_fwd(act, layers[l])
        saved.append(sv)
    d_act, loss_blk = _loss_head("loss_head", act, loss_target[0])

    parts_in, parts_br, parts_out, small_g = [], [], [], []
    for l in reversed(range(DEPTH)):
        d_act, g = _layer_bwd(d_act, layers[l], saved[l])
        g_in = _unpad_cols(g["w_in"]).reshape(D_MODEL, N_CHIPS, N_IN // N_CHIPS).transpose(1, 0, 2)
        g_br = g["wbr"].reshape(N_BRANCH * BRANCH, N_CHIPS, D_MODEL // N_CHIPS).transpose(1, 0, 2)
        g_out = g["wout"].reshape(N_CHIPS, D_MODEL // N_CHIPS, D_MODEL)
        recv = _reduce_scatter_chips("reduce_scatter_grads", [g_in, g_br, g_out])
        mine = [_sum_slots("sum_chips", r, t) for r, t in zip(recv, (64, 512, 128))]
        theirs = _exchange_cores("exchange_cores", mine)
        parts_in.append((mine[0], theirs[0]))
        parts_br.append((mine[1], theirs[1]))
        parts_out.append((mine[2], theirs[2]))
        small_g.append(g)
    small_g = small_g[::-1]
    parts_in, parts_br, parts_out = parts_in[::-1], parts_br[::-1], parts_out[::-1]

    stack = lambda key, f=lambda a: a: jnp.stack([f(g[key]) for g in small_g])
    small = dict(
        pre_norm_w=stack("pre_w", lambda a: a[0]), post_norm_w=stack("post_w", lambda a: a[0]),
        a_log=stack("alog", lambda a: a[0, ALPHA_LANE:ALPHA_LANE + HEADS]),
        dt_bias=stack("dtb", lambda a: a[0, ALPHA_LANE:ALPHA_LANE + HEADS]), o_norm_w=stack("onw", lambda a: a[0]),
        conv_b=stack("cb", lambda a: a[0]), ln_w=stack("ln_w", lambda a: a[0]), ln_b=stack("ln_b", lambda a: a[0]),
        f_bias=stack("fb", lambda a: a[0, FORGET_LANE:FORGET_LANE + HEADS]),
        conv_qkv_w=stack("w4", lambda a: a[:SHORT_CONV]), conv_w=stack("w31", lambda a: a[:CONF_CONV]),
        loss=loss_blk[0, 0:1])
    gathered = _allgather_devices("allgather_small", _pack([small[n] for n, _ in _SMALL]))
    total = _unpack(_sum_slots("sum_devices", gathered, gathered.shape[1]), [s for _, s in _SMALL])
    total = {n: t for (n, _), t in zip(_SMALL, total)}
    loss = total.pop("loss")[0]
    for n, width in _SHARDED_SMALL.items():
        total[n] = lax.dynamic_slice_in_dim(total[n], chip * width, width, axis=2)

    names = list(total)
    packed = [_pack([d[n] for n in names]) for d in (weights, mom1, mom2)]
    res = _adamw("adamw_small", packed[0], packed[1], packed[2], [_pack([total[n] for n in names])], packed[0].shape[0])
    shapes = [weights[n].shape for n in names]
    grads, delta, new_m, new_v = [dict(zip(names, _unpack(r, shapes))) for r in res]
    big = (("w_in", parts_in, (DEPTH * D_MODEL, N_IN // N_CHIPS), 64),
           ("w_branch", parts_br, (DEPTH * N_BRANCH * BRANCH, D_MODEL // N_CHIPS), 512),
           ("w_out", parts_out, (DEPTH * D_MODEL // N_CHIPS, D_MODEL), 128))
    for n, parts, shape2, tile in big:
        mine = jnp.concatenate([p[0] for p in parts], axis=0)
        theirs = jnp.concatenate([p[1] for p in parts], axis=0)
        res = _adamw("adamw_" + n, weights[n].reshape(shape2), mom1[n].reshape(shape2), mom2[n].reshape(shape2),
                     [mine, theirs], tile)
        grads[n], delta[n], new_m[n], new_v[n] = [r.reshape(weights[n].shape) for r in res]

    outs = [loss, d_act[None]]
    for d in (grads, delta, new_m, new_v):
        outs += [d[n] for n in _WEIGHTS]
    return tuple(outs)
```

```python
import functools

import jax
import jax.numpy as jnp
from jax import lax
from jax.experimental import pallas as pl
from jax.experimental.pallas import tpu as pltpu

F32 = jnp.float32
BF16 = jnp.bfloat16
MXU_DTYPE = jnp.bfloat16

D_MODEL = 2048
DEPTH = 4
BRANCH = 1024
HEAD_DIM = 128
HEADS = 8
CHUNK = 64
SHORT_CONV = 4
CONF_CONV = 31
N_BRANCH = 3
NORM_EPS = 1e-6
N_IN = 17432

OFF_GATE = 0
OFF_QKVA = 6144
OFF_ZA = 9216
OFF_VAL = 10240
OFF_GLUG = 11264
OFF_ZB = 12288
OFF_QKVC = 13312
OFF_ZC = 16384
OFF_SMALL = 17408
N_PAD = 17920
LANE = 128
BETA_LANE, ALPHA_LANE, FORGET_LANE = 0, 8, 16

ADAM_LR = 0.001
ADAM_B1 = 0.9
ADAM_B2 = 0.999
ADAM_EPS = 1e-08
ADAM_WD = 0.01
ADAM_STEP = 10

VMEM_LIMIT = 56 * 1024 * 1024

NN = (((1,), (0,)), ((), ()))
NT = (((1,), (1,)), ((), ()))
TN = (((0,), (0,)), ((), ()))
MESH = pl.DeviceIdType.MESH


def _params(sem=None):
    return pltpu.CompilerParams(dimension_semantics=sem, vmem_limit_bytes=VMEM_LIMIT)


def _dot(a, b, dims=NN):
    return lax.dot_general(a.astype(MXU_DTYPE), b.astype(MXU_DTYPE), dims, preferred_element_type=F32)


def _dot_hi(a, b, dims=NN):
    return lax.dot_general(a, b, dims, precision=lax.Precision.HIGHEST, preferred_element_type=F32)


def _sigmoid(x):
    return 1.0 / (1.0 + jnp.exp(-x))


def _silu(x):
    return x * _sigmoid(x)


def _softplus(x):
    return jnp.maximum(x, 0.0) + jnp.log(1.0 + jnp.exp(-jnp.abs(x)))


def _colsel(m, j):
    lane = lax.broadcasted_iota(jnp.int32, m.shape, 1)
    return jnp.sum(jnp.where(lane == j, m, 0.0), axis=1, keepdims=True)


def _rowsel(m, j):
    sub = lax.broadcasted_iota(jnp.int32, m.shape, 0)
    return jnp.sum(jnp.where(sub == j, m, 0.0), axis=0, keepdims=True)


def _row_specs(rows, tile):
    return [pl.BlockSpec((tile, w), functools.partial(lambda i, cb: (i, cb), cb=cb)) for (_, w, cb) in rows]


def _rowwise(name, fn, rows, params, outs, tile):
    seq = rows[0][0].shape[0]
    tile = min(tile, seq)
    n_in = len(rows) + len(params)

    def body(*refs):
        res = fn(*[r[...] for r in refs[:n_in]])
        for o_ref, r in zip(refs[n_in:], res):
            o_ref[...] = r.astype(o_ref.dtype)

    return pl.pallas_call(
        body, name=name, grid=(seq // tile,),
        in_specs=_row_specs(rows, tile) + [pl.BlockSpec(p.shape, lambda i: (0, 0)) for p in params],
        out_specs=[pl.BlockSpec((tile, w), lambda i: (i, 0)) for (w, _) in outs],
        out_shape=[jax.ShapeDtypeStruct((seq, w), dt) for (w, dt) in outs],
        compiler_params=_params(("parallel",)),
    )(*[r[0] for r in rows], *params)


def _rowwise_bwd(name, fn, rows, params, cts, row_grads, tile):
    seq = rows[0][0].shape[0]
    tile = min(tile, seq)
    nr, npar, nct = len(rows), len(params), len(cts)
    n_in = nr + npar

    def body(*refs):
        vals = [r[...] for r in refs[:n_in]]
        res, vjp = jax.vjp(fn, *vals)
        grads = vjp(tuple(c[...].astype(r.dtype) for c, r in zip(refs[n_in:n_in + nct], res)))
        outs = refs[n_in + nct:]
        k = 0
        for idx, dt in enumerate(row_grads):
            if dt is not None:
                outs[k][...] = grads[idx].astype(dt)
                k += 1
        first = pl.program_id(0) == 0
        for j in range(npar):
            g = grads[nr + j].astype(F32)
            o_ref = outs[k + j]

            @pl.when(first)
            def _(o_ref=o_ref, g=g):
                o_ref[...] = g

            @pl.when(jnp.logical_not(first))
            def _(o_ref=o_ref, g=g):
                o_ref[...] += g

    want = [(rows[i][1], dt) for i, dt in enumerate(row_grads) if dt is not None]
    return pl.pallas_call(
        body, name=name, grid=(seq // tile,),
        in_specs=(_row_specs(rows, tile) + [pl.BlockSpec(p.shape, lambda i: (0, 0)) for p in params]
                  + [pl.BlockSpec((tile, c.shape[1]), lambda i: (i, 0)) for c in cts]),
        out_specs=([pl.BlockSpec((tile, w), lambda i: (i, 0)) for (w, _) in want]
                   + [pl.BlockSpec(p.shape, lambda i: (0, 0)) for p in params]),
        out_shape=([jax.ShapeDtypeStruct((seq, w), dt) for (w, dt) in want]
                   + [jax.ShapeDtypeStruct(p.shape, F32) for p in params]),
        compiler_params=_params(("arbitrary",)),
    )(*[r[0] for r in rows], *params, *cts)


def _rms(x, w):
    x = x.astype(F32)
    return x * lax.rsqrt(jnp.mean(x * x, axis=-1, keepdims=True) + NORM_EPS) * w


def _rms_pre_fn(x, w):
    return (_rms(x, w),)


def _rms_pre_res_fn(x, w):
    return (_rms(x, w), x)


def _rms_post_fn(out, x, w):
    return (x + _rms(out, w),)


def _rms_only_fn(out, w):
    return (_rms(out, w),)


def _ln_gate_fn(u, z, w, b):
    u = u.astype(F32)
    uc = u - jnp.mean(u, axis=-1, keepdims=True)
    y = uc * lax.rsqrt(jnp.mean(uc * uc, axis=-1, keepdims=True) + NORM_EPS) * w + b
    return (_silu(y) * _silu(z.astype(F32)),)


def _merge_fn(g0, g1, g2, b0, b1, b2):
    return (_sigmoid(g0) * b0 + _sigmoid(g1) * b1 + _sigmoid(g2) * b2,)


def _matmul(name, a, b, mode, out_dtype, tm, tn, tk):
    if mode == "nn":
        (m, kc), n = a.shape, b.shape[1]
    elif mode == "nt":
        (m, kc), n = a.shape, b.shape[0]
    else:
        (kc, m), n = a.shape, b.shape[1]
    tm, tn, tk = min(tm, m), min(tn, n), min(tk, kc)
    nk = kc // tk
    assert m % tm == 0 and n % tn == 0 and kc % tk == 0, (name, a.shape, b.shape)
    dims = {"nn": NN, "nt": NT, "tn": TN}[mode]
    a_spec = (pl.BlockSpec((tk, tm), lambda j, i, k: (k, i)) if mode == "tn"
              else pl.BlockSpec((tm, tk), lambda j, i, k: (i, k)))
    b_spec = (pl.BlockSpec((tn, tk), lambda j, i, k: (j, k)) if mode == "nt"
              else pl.BlockSpec((tk, tn), lambda j, i, k: (k, j)))
    use_acc = nk > 1 and out_dtype != F32

    def body(a_ref, b_ref, o_ref, *acc):
        p = _dot(a_ref[...], b_ref[...], dims)
        if nk == 1:
            o_ref[...] = p.astype(out_dtype)
            return
        k = pl.program_id(2)
        dst = acc[0] if use_acc else o_ref

        @pl.when(k == 0)
        def _():
            dst[...] = p

        @pl.when(k > 0)
        def _():
            dst[...] += p

        if use_acc:
            @pl.when(k == nk - 1)
            def _():
                o_ref[...] = dst[...].astype(out_dtype)

    return pl.pallas_call(
        body, name=name, grid=(n // tn, m // tm, nk),
        in_specs=[a_spec, b_spec],
        out_specs=pl.BlockSpec((tm, tn), lambda j, i, k: (i, j)),
        out_shape=jax.ShapeDtypeStruct((m, n), out_dtype),
        scratch_shapes=[pltpu.VMEM((tm, tn), F32)] if use_acc else [],
        compiler_params=_params(("parallel", "parallel", "arbitrary")),
    )(a, b)


HALO = 32
CONV_TC = 256
CONV_T = 1024


def _conv_fwd(name, x, x_off, ch, w, b, k_width, gate_off=None):
    seq = x.shape[0]
    t_blk = min(CONV_T, seq)
    tc = CONV_TC
    hb = t_blk // HALO
    xcb = x_off // tc
    has_gate = gate_off is not None

    def body(*refs):
        if has_gate:
            xm_ref, xh_ref, gm_ref, gh_ref, w_ref, b_ref, y_ref, win = refs
        else:
            xm_ref, xh_ref, w_ref, b_ref, y_ref, win = refs
        t = pl.program_id(1)
        xm, xh = xm_ref[...], xh_ref[...]
        if has_gate:
            xm = xm * _sigmoid(gm_ref[...])
            xh = xh * _sigmoid(gh_ref[...])
        win[0:HALO, :] = jnp.where(t == 0, 0.0, xh)
        win[HALO:HALO + t_blk, :] = xm
        acc = jnp.broadcast_to(b_ref[...], (t_blk, tc))
        for k in range(k_width):
            acc = acc + w_ref[k:k + 1, :] * win[HALO - (k_width - 1) + k:HALO - (k_width - 1) + k + t_blk, :]
        y_ref[...] = acc

    main = lambda off: pl.BlockSpec((t_blk, tc), lambda c, t: (t, off + c))
    halo = lambda off: pl.BlockSpec((HALO, tc), lambda c, t: (jnp.maximum(t * hb - 1, 0), off + c))
    ins, specs = [x, x], [main(xcb), halo(xcb)]
    if has_gate:
        gcb = gate_off // tc
        ins += [x, x]
        specs += [main(gcb), halo(gcb)]
    ins += [w, b]
    specs += [pl.BlockSpec((w.shape[0], tc), lambda c, t: (0, c)), pl.BlockSpec((1, tc), lambda c, t: (0, c))]
    return pl.pallas_call(
        body, name=name, grid=(ch // tc, seq // t_blk), in_specs=specs,
        out_specs=pl.BlockSpec((t_blk, tc), lambda c, t: (t, c)),
        out_shape=jax.ShapeDtypeStruct((seq, ch), F32),
        scratch_shapes=[pltpu.VMEM((HALO + t_blk, tc), F32)],
        compiler_params=_params(("parallel", "arbitrary")),
    )(*ins)


def _conv_bwd(name, dy, x, x_off, ch, w, k_width, gate_off=None):
    seq = x.shape[0]
    t_blk = min(CONV_T, seq)
    tc = CONV_TC
    hb = t_blk // HALO
    nt = seq // t_blk
    xcb = x_off // tc
    has_gate = gate_off is not None
    kp = w.shape[0]

    def body(*refs):
        if has_gate:
            dm_ref, dh_ref, xm_ref, xh_ref, gm_ref, gh_ref, w_ref, dv_ref, dg_ref, dw_ref, db_ref, winx, wind = refs
        else:
            dm_ref, dh_ref, xm_ref, xh_ref, w_ref, dx_ref, dw_ref, db_ref, winx, wind = refs
        t = pl.program_id(1)
        xm, xh = xm_ref[...], xh_ref[...]
        if has_gate:
            sg = _sigmoid(gm_ref[...])
            um = xm * sg
            uh = xh * _sigmoid(gh_ref[...])
        else:
            um, uh = xm, xh
        winx[0:HALO, :] = jnp.where(t == nt - 1, 0.0, uh)
        winx[HALO:HALO + t_blk, :] = um
        dm = dm_ref[...]
        wind[0:t_blk, :] = dm
        wind[t_blk:t_blk + HALO, :] = jnp.where(t == 0, 0.0, dh_ref[...])
        du = jnp.zeros((t_blk, tc), F32)
        for k in range(k_width):
            du = du + w_ref[k:k + 1, :] * wind[k_width - 1 - k:k_width - 1 - k + t_blk, :]
        if has_gate:
            dv_ref[...] = (du * sg).astype(dv_ref.dtype)
            dg_ref[...] = (du * xm * sg * (1.0 - sg)).astype(dg_ref.dtype)
        else:
            dx_ref[...] = du.astype(dx_ref.dtype)

        @pl.when(t == 0)
        def _():
            dw_ref[...] = jnp.zeros_like(dw_ref)
            db_ref[...] = jnp.zeros_like(db_ref)

        for k in range(k_width):
            s0 = HALO - (k_width - 1) + k
            dw_ref[k:k + 1, :] += jnp.sum(dm * winx[s0:s0 + t_blk, :], axis=0, keepdims=True)
        db_ref[...] += jnp.sum(dm, axis=0, keepdims=True)

    rt = lambda t: nt - 1 - t
    main = lambda off: pl.BlockSpec((t_blk, tc), lambda c, t: (rt(t), off + c))
    past = lambda off: pl.BlockSpec((HALO, tc), lambda c, t: (jnp.maximum(rt(t) * hb - 1, 0), off + c))
    future = pl.BlockSpec((HALO, tc), lambda c, t: (jnp.minimum((rt(t) + 1) * hb, seq // HALO - 1), c))
    ins, specs = [dy, dy, x, x], [main(0), future, main(xcb), past(xcb)]
    if has_gate:
        gcb = gate_off // tc
        ins += [x, x]
        specs += [main(gcb), past(gcb)]
    ins += [w]
    specs += [pl.BlockSpec((kp, tc), lambda c, t: (0, c))]
    blk = pl.BlockSpec((t_blk, tc), lambda c, t: (rt(t), c))
    n_dx = 2 if has_gate else 1
    return pl.pallas_call(
        body, name=name, grid=(ch // tc, nt), in_specs=specs,
        out_specs=[blk] * n_dx + [pl.BlockSpec((kp, tc), lambda c, t: (0, c)), pl.BlockSpec((1, tc), lambda c, t: (0, c))],
        out_shape=[jax.ShapeDtypeStruct((seq, ch), BF16)] * n_dx + [jax.ShapeDtypeStruct((kp, ch), F32),
                                                                    jax.ShapeDtypeStruct((1, ch), F32)],
        scratch_shapes=[pltpu.VMEM((HALO + t_blk, tc), F32), pltpu.VMEM((HALO + t_blk, tc), F32)],
        compiler_params=_params(("parallel", "arbitrary")),
    )(*ins)


@jax.custom_vjp
def _inv_unit_lower(lows):
    n = lows[0].shape[0]
    eye = (lax.broadcasted_iota(jnp.int32, (n, n), 0) == lax.broadcasted_iota(jnp.int32, (n, n), 1)).astype(F32)
    accs = [eye - low for low in lows]
    pws = list(lows)
    steps = 1
    while steps * 2 < n:
        pws = [_dot_hi(pw, pw) for pw in pws]
        accs = [acc + _dot_hi(acc, pw) for acc, pw in zip(accs, pws)]
        steps *= 2
    return tuple(accs)


def _inv_fwd(lows):
    ts = _inv_unit_lower(lows)
    return ts, ts


def _inv_bwd(ts, dts):
    left = [_dot_hi(t, dt, TN) for t, dt in zip(ts, dts)]
    return (tuple(-_dot_hi(l, t, NT) for l, t in zip(left, ts)),)


_inv_unit_lower.defvjp(_inv_fwd, _inv_bwd)


def _gdr_chunk(cqkv, z, sm, alog, dtb, onw, state):
    c = cqkv.shape[0]
    hs = range(HEADS)
    ri = lax.broadcasted_iota(jnp.int32, (c, c), 0)
    ci = lax.broadcasted_iota(jnp.int32, (c, c), 1)
    incl, strict = ri >= ci, ri > ci
    beta_all = _sigmoid(sm)
    la_all = -jnp.exp(alog) * _softplus(sm + dtb)
    g_cols = _dot_hi(incl.astype(F32), la_all)
    g_rows = _dot_hi(la_all, (ri <= ci).astype(F32), TN)
    g_end = jnp.sum(la_all, axis=0, keepdims=True)
    act = _silu(cqkv)
    sl = lambda base, h: slice(base + h * HEAD_DIM, base + (h + 1) * HEAD_DIM)
    q = [act[:, sl(0, h)] for h in hs]
    k = [act[:, sl(BRANCH, h)] for h in hs]
    v = [act[:, sl(2 * BRANCH, h)] for h in hs]
    q = [x * lax.rsqrt(jnp.sum(x * x, axis=-1, keepdims=True) + NORM_EPS) * (HEAD_DIM ** -0.5) for x in q]
    k = [x * lax.rsqrt(jnp.sum(x * x, axis=-1, keepdims=True) + NORM_EPS) for x in k]
    beta = [_colsel(beta_all, BETA_LANE + h) for h in hs]
    g = [_colsel(g_cols, ALPHA_LANE + h) for h in hs]
    g_row = [_rowsel(g_rows, ALPHA_LANE + h) for h in hs]
    g_last = [_colsel(g_end, ALPHA_LANE + h) for h in hs]
    decay = [jnp.where(incl, jnp.exp(jnp.where(incl, g[h] - g_row[h], 0.0)), 0.0) for h in hs]
    kk = [_dot(k[h], k[h], NT) for h in hs]
    qk = [_dot(q[h], k[h], NT) * decay[h] for h in hs]
    t_inv = _inv_unit_lower(tuple(jnp.where(strict, beta[h] * kk[h] * decay[h], 0.0) for h in hs))
    eg = [jnp.exp(g[h]) for h in hs]
    u0 = [_dot(t_inv[h], v[h] * beta[h]) for h in hs]
    w_cum = [_dot(t_inv[h], k[h] * (beta[h] * eg[h])) for h in hs]
    s_in = [state[h] for h in hs]
    u = [u0[h] - _dot(w_cum[h], s_in[h]) for h in hs]
    o = [_dot(q[h] * eg[h], s_in[h]) + _dot(qk[h], u[h]) for h in hs]
    s_out = [s_in[h] * jnp.exp(g_last[h]) + _dot(k[h] * jnp.exp(g_last[h] - g[h]), u[h], TN) for h in hs]
    o = [x * lax.rsqrt(jnp.mean(x * x, axis=-1, keepdims=True) + NORM_EPS) * onw for x in o]
    y = [o[h] * _silu(z[:, sl(0, h)]) for h in hs]
    return jnp.concatenate(y, axis=1), jnp.concatenate([s[None] for s in s_out], axis=0)


def _gdr_specs(nc, order):
    return [
        pl.BlockSpec((CHUNK, 3 * BRANCH), lambda n: (order(n), 0)),
        pl.BlockSpec((CHUNK, BRANCH), lambda n: (order(n), OFF_ZA // BRANCH)),
        pl.BlockSpec((CHUNK, LANE), lambda n: (order(n), OFF_SMALL // LANE)),
        pl.BlockSpec((1, LANE), lambda n: (0, 0)),
        pl.BlockSpec((1, LANE), lambda n: (0, 0)),
        pl.BlockSpec((1, LANE), lambda n: (0, 0)),
    ]


def _gdr_fwd(name, cqkv, proj, alog, dtb, onw):
    seq = cqkv.shape[0]
    nc = seq // CHUNK

    def body(c_ref, z_ref, sm_ref, al_ref, dt_ref, on_ref, y_ref, st_ref, state):
        @pl.when(pl.program_id(0) == 0)
        def _():
            state[...] = jnp.zeros_like(state)

        s_in = state[...]
        st_ref[0] = s_in
        y, s_out = _gdr_chunk(c_ref[...], z_ref[...], sm_ref[...], al_ref[...], dt_ref[...], on_ref[...], s_in)
        y_ref[...] = y.astype(y_ref.dtype)
        state[...] = s_out

    return pl.pallas_call(
        body, name=name, grid=(nc,), in_specs=_gdr_specs(nc, lambda n: n),
        out_specs=[pl.BlockSpec((CHUNK, BRANCH), lambda n: (n, 0)),
                   pl.BlockSpec((1, HEADS, HEAD_DIM, HEAD_DIM), lambda n: (n, 0, 0, 0))],
        out_shape=[jax.ShapeDtypeStruct((seq, BRANCH), BF16),
                   jax.ShapeDtypeStruct((nc, HEADS, HEAD_DIM, HEAD_DIM), F32)],
        scratch_shapes=[pltpu.VMEM((HEADS, HEAD_DIM, HEAD_DIM), F32)],
        compiler_params=_params(("arbitrary",)),
    )(cqkv, proj, proj, alog, dtb, onw)


def _gdr_bwd(name, dy, states, cqkv, proj, alog, dtb, onw):
    seq = cqkv.shape[0]
    nc = seq // CHUNK
    rev = lambda n: nc - 1 - n

    def body(c_ref, z_ref, sm_ref, al_ref, dt_ref, on_ref, dy_ref, st_ref,
             dc_ref, dz_ref, dsm_ref, dal_ref, ddt_ref, don_ref, dstate):
        first = pl.program_id(0) == 0

        @pl.when(first)
        def _():
            dstate[...] = jnp.zeros_like(dstate)
            dal_ref[...] = jnp.zeros_like(dal_ref)
            ddt_ref[...] = jnp.zeros_like(ddt_ref)
            don_ref[...] = jnp.zeros_like(don_ref)

        _, vjp = jax.vjp(_gdr_chunk, c_ref[...], z_ref[...], sm_ref[...], al_ref[...], dt_ref[...], on_ref[...],
                         st_ref[0])
        dc, dz, dsm, dal, ddt, don, ds = vjp((dy_ref[...].astype(F32), dstate[...]))
        dc_ref[...] = dc
        dz_ref[...] = dz.astype(dz_ref.dtype)
        dsm_ref[...] = dsm
        dal_ref[...] += dal
        ddt_ref[...] += ddt
        don_ref[...] += don
        dstate[...] = ds

    small = pl.BlockSpec((1, LANE), lambda n: (0, 0))
    return pl.pallas_call(
        body, name=name, grid=(nc,),
        in_specs=_gdr_specs(nc, rev) + [pl.BlockSpec((CHUNK, BRANCH), lambda n: (rev(n), 0)),
                                        pl.BlockSpec((1, HEADS, HEAD_DIM, HEAD_DIM), lambda n: (rev(n), 0, 0, 0))],
        out_specs=[pl.BlockSpec((CHUNK, 3 * BRANCH), lambda n: (rev(n), 0)),
                   pl.BlockSpec((CHUNK, BRANCH), lambda n: (rev(n), 0)),
                   pl.BlockSpec((CHUNK, LANE), lambda n: (rev(n), 0)), small, small, small],
        out_shape=[jax.ShapeDtypeStruct((seq, 3 * BRANCH), F32), jax.ShapeDtypeStruct((seq, BRANCH), BF16),
                   jax.ShapeDtypeStruct((seq, LANE), F32)] + [jax.ShapeDtypeStruct((1, LANE), F32)] * 3,
        scratch_shapes=[pltpu.VMEM((HEADS, HEAD_DIM, HEAD_DIM), F32)],
        compiler_params=_params(("arbitrary",)),
    )(cqkv, proj, proj, alog, dtb, onw, dy, states)


GATE_T = 512
ATT_T = 512


def _fox_gate_fwd(name, proj, fb):
    seq = proj.shape[0]
    tb = min(GATE_T, seq)

    def body(sm_ref, fb_ref, c_ref, ct_ref):
        tri = (lax.broadcasted_iota(jnp.int32, (tb, tb), 0) >= lax.broadcasted_iota(jnp.int32, (tb, tb), 1)).astype(F32)
        carry = jnp.zeros((1, LANE), F32)
        for i in range(seq // tb):
            lf = -_softplus(-(sm_ref[i * tb:(i + 1) * tb, :] + fb_ref[...]))
            cb = _dot_hi(tri, lf) + carry
            c_ref[i * tb:(i + 1) * tb, :] = cb
            ct_ref[:, i * tb:(i + 1) * tb] = cb.T
            carry = carry + jnp.sum(lf, axis=0, keepdims=True)

    return pl.pallas_call(
        body, name=name, grid=(1,),
        in_specs=[pl.BlockSpec((seq, LANE), lambda i: (0, OFF_SMALL // LANE)), pl.BlockSpec((1, LANE), lambda i: (0, 0))],
        out_specs=[pl.BlockSpec((seq, LANE), lambda i: (0, 0)), pl.BlockSpec((LANE, seq), lambda i: (0, 0))],
        out_shape=[jax.ShapeDtypeStruct((seq, LANE), F32), jax.ShapeDtypeStruct((LANE, seq), F32)],
        compiler_params=_params(("arbitrary",)),
    )(proj, fb)


def _fox_gate_bwd(name, dck, proj, fb):
    seq = proj.shape[0]
    tb = min(GATE_T, seq)
    nb = seq // tb

    def body(d_ref, sm_ref, fb_ref, o_ref, dfb_ref, pad):
        tri = (lax.broadcasted_iota(jnp.int32, (tb, tb), 0) >= lax.broadcasted_iota(jnp.int32, (tb, tb), 1)).astype(F32)
        pad[...] = jnp.zeros_like(pad)
        carry = jnp.zeros((HEADS, 1), F32)
        dfb = jnp.zeros((1, LANE), F32)
        for i in reversed(range(nb)):
            blk = d_ref[:, i * tb:(i + 1) * tb]
            pad[FORGET_LANE:FORGET_LANE + HEADS, :] = _dot_hi(blk, tri) + carry
            carry = carry + jnp.sum(blk, axis=1, keepdims=True)
            x = sm_ref[i * tb:(i + 1) * tb, :] + fb_ref[...]
            dsm = pad[...].T * _sigmoid(-x)
            o_ref[i * tb:(i + 1) * tb, :] = dsm
            dfb = dfb + jnp.sum(dsm, axis=0, keepdims=True)
        dfb_ref[...] = dfb

    return pl.pallas_call(
        body, name=name, grid=(1,),
        in_specs=[pl.BlockSpec((HEADS, seq), lambda i: (0, 0)), pl.BlockSpec((seq, LANE), lambda i: (0, OFF_SMALL // LANE)),
                  pl.BlockSpec((1, LANE), lambda i: (0, 0))],
        out_specs=[pl.BlockSpec((seq, LANE), lambda i: (0, 0)), pl.BlockSpec((1, LANE), lambda i: (0, 0))],
        out_shape=[jax.ShapeDtypeStruct((seq, LANE), F32), jax.ShapeDtypeStruct((1, LANE), F32)],
        scratch_shapes=[pltpu.VMEM((LANE, tb), F32)],
        compiler_params=_params(("arbitrary",)),
    )(dck, proj, fb)


def _att_scores(q, k, cq, ck_row, diagonal):
    s = _dot(q, k, NT) * (HEAD_DIM ** -0.5) + (cq - ck_row)
    if diagonal:
        keep = lax.broadcasted_iota(jnp.int32, s.shape, 0) >= lax.broadcasted_iota(jnp.int32, s.shape, 1)
        s = jnp.where(keep, s, -jnp.inf)
    return s


def _causal_pairs(nq, key_major):
    pairs = ([(i, j) for j in range(nq) for i in range(j, nq)] if key_major
             else [(i, j) for i in range(nq) for j in range(i + 1)])
    return jnp.asarray([p[0] for p in pairs], jnp.int32), jnp.asarray([p[1] for p in pairs], jnp.int32)


def _attn_specs(tq):
    qs = lambda off: pl.BlockSpec((tq, HEAD_DIM), lambda h, t, it, jt: (it[t], off + h))
    kv = lambda off: pl.BlockSpec((tq, HEAD_DIM), lambda h, t, it, jt: (jt[t], off + h))
    c_spec = pl.BlockSpec((tq, LANE), lambda h, t, it, jt: (it[t], 0))
    ct_spec = pl.BlockSpec((1, 1, tq), lambda h, t, it, jt: (FORGET_LANE + h, 0, jt[t]))
    lse_spec = pl.BlockSpec((1, tq, LANE), lambda h, t, it, jt: (h, it[t], 0))
    return qs, kv, c_spec, ct_spec, lse_spec


def _attn_fwd(name, proj, c, ct3):
    seq = proj.shape[0]
    tq = min(ATT_T, seq)
    nq = seq // tq
    qb, zb = OFF_QKVC // HEAD_DIM, OFF_ZC // HEAD_DIM
    i_tab, j_tab = _causal_pairs(nq, False)

    def body(it, jt, q_ref, k_ref, v_ref, z_ref, c_ref, ct_ref, y_ref, o_ref, lse_ref, m_s, l_s, acc_s):
        h, t = pl.program_id(0), pl.program_id(1)
        i, j = it[t], jt[t]

        @pl.when(j == 0)
        def _():
            m_s[...] = jnp.full_like(m_s, -jnp.inf)
            l_s[...] = jnp.zeros_like(l_s)
            acc_s[...] = jnp.zeros_like(acc_s)

        def step(diagonal):
            cq = _colsel(c_ref[...], FORGET_LANE + h)
            s = _att_scores(q_ref[...], k_ref[...], cq, ct_ref[0], diagonal)
            m_old = m_s[...]
            m_new = jnp.maximum(m_old, jnp.max(s, axis=1, keepdims=True))
            p = jnp.exp(s - m_new)
            alpha = jnp.exp(m_old - m_new)
            l_s[...] = alpha * l_s[...] + jnp.sum(p, axis=1, keepdims=True)
            p_hi = p.astype(MXU_DTYPE).astype(F32)
            acc_s[...] = alpha * acc_s[...] + _dot(p_hi, v_ref[...]) + _dot(p - p_hi, v_ref[...])
            m_s[...] = m_new

        @pl.when(j < i)
        def _():
            step(False)

        @pl.when(j == i)
        def _():
            step(True)
            o = acc_s[...] / l_s[...]
            o_ref[...] = o
            y_ref[...] = (o * _silu(z_ref[...])).astype(y_ref.dtype)
            lse_ref[0] = jnp.broadcast_to(m_s[...] + jnp.log(l_s[...]), (tq, LANE))

    qs, kv, c_spec, ct_spec, lse_spec = _attn_specs(tq)
    return pl.pallas_call(
        body, name=name,
        grid_spec=pltpu.PrefetchScalarGridSpec(
            num_scalar_prefetch=2, grid=(HEADS, i_tab.shape[0]),
            in_specs=[qs(qb), kv(qb + HEADS), kv(qb + 2 * HEADS), qs(zb), c_spec, ct_spec],
            out_specs=[qs(0), qs(0), lse_spec],
            scratch_shapes=[pltpu.VMEM((tq, 1), F32), pltpu.VMEM((tq, 1), F32), pltpu.VMEM((tq, HEAD_DIM), F32)]),
        out_shape=[jax.ShapeDtypeStruct((seq, BRANCH), BF16), jax.ShapeDtypeStruct((seq, BRANCH), F32),
                   jax.ShapeDtypeStruct((HEADS, seq, LANE), F32)],
        compiler_params=_params(("parallel", "arbitrary")),
    )(i_tab, j_tab, proj, proj, proj, proj, c, ct3)


def _attn_dq(name, dy, o, lse, proj, c, ct3):
    seq = proj.shape[0]
    tq = min(ATT_T, seq)
    nq = seq // tq
    qb, zb = OFF_QKVC // HEAD_DIM, OFF_ZC // HEAD_DIM
    i_tab, j_tab = _causal_pairs(nq, False)

    def body(it, jt, q_ref, k_ref, v_ref, z_ref, c_ref, ct_ref, dy_ref, o_ref, lse_ref, dq_ref, dz_ref, do_s, dl_s, acc_s):
        h, t = pl.program_id(0), pl.program_id(1)
        i, j = it[t], jt[t]

        @pl.when(j == 0)
        def _():
            z = z_ref[...]
            sg = _sigmoid(z)
            dyv = dy_ref[...].astype(F32)
            do = dyv * z * sg
            do_s[...] = do
            dl_s[...] = jnp.sum(do.astype(MXU_DTYPE).astype(F32) * o_ref[...], axis=1, keepdims=True)
            dz_ref[...] = (dyv * o_ref[...] * sg * (1.0 + z * (1.0 - sg))).astype(dz_ref.dtype)
            acc_s[...] = jnp.zeros_like(acc_s)

        def step(diagonal):
            cq = _colsel(c_ref[...], FORGET_LANE + h)
            s = _att_scores(q_ref[...], k_ref[...], cq, ct_ref[0], diagonal)
            p = jnp.exp(s - jnp.max(lse_ref[0], axis=1, keepdims=True))
            dp = _dot(do_s[...], v_ref[...], NT)
            ds = p * (dp - dl_s[...])
            acc_s[...] += _dot(ds, k_ref[...])

        @pl.when(j < i)
        def _():
            step(False)

        @pl.when(j == i)
        def _():
            step(True)
            dq_ref[...] = (acc_s[...] * (HEAD_DIM ** -0.5)).astype(dq_ref.dtype)

    qs, kv, c_spec, ct_spec, lse_spec = _attn_specs(tq)
    return pl.pallas_call(
        body, name=name,
        grid_spec=pltpu.PrefetchScalarGridSpec(
            num_scalar_prefetch=2, grid=(HEADS, i_tab.shape[0]),
            in_specs=[qs(qb), kv(qb + HEADS), kv(qb + 2 * HEADS), qs(zb), c_spec, ct_spec, qs(0), qs(0), lse_spec],
            out_specs=[qs(0), qs(0)],
            scratch_shapes=[pltpu.VMEM((tq, HEAD_DIM), F32), pltpu.VMEM((tq, 1), F32), pltpu.VMEM((tq, HEAD_DIM), F32)]),
        out_shape=[jax.ShapeDtypeStruct((seq, BRANCH), BF16)] * 2,
        compiler_params=_params(("parallel", "arbitrary")),
    )(i_tab, j_tab, proj, proj, proj, proj, c, ct3, dy, o, lse)


def _attn_dkv(name, dy, o, lse, proj, c, ct3):
    seq = proj.shape[0]
    tq = min(ATT_T, seq)
    nq = seq // tq
    qb, zb = OFF_QKVC // HEAD_DIM, OFF_ZC // HEAD_DIM
    i_tab, j_tab = _causal_pairs(nq, True)

    def body(it, jt, q_ref, k_ref, v_ref, z_ref, c_ref, ct_ref, dy_ref, o_ref, lse_ref, dk_ref, dv_ref, dc_ref, dk_s, dv_s, dc_s):
        h, t = pl.program_id(0), pl.program_id(1)
        i, j = it[t], jt[t]

        @pl.when(i == j)
        def _():
            dk_s[...] = jnp.zeros_like(dk_s)
            dv_s[...] = jnp.zeros_like(dv_s)
            dc_s[...] = jnp.zeros_like(dc_s)

        def step(diagonal):
            z = z_ref[...]
            do = dy_ref[...].astype(F32) * _silu(z)
            delta = jnp.sum(do.astype(MXU_DTYPE).astype(F32) * o_ref[...], axis=1, keepdims=True)
            cq = _colsel(c_ref[...], FORGET_LANE + h)
            s = _att_scores(q_ref[...], k_ref[...], cq, ct_ref[0], diagonal)
            p = jnp.exp(s - jnp.max(lse_ref[0], axis=1, keepdims=True))
            dv_s[...] += _dot(p, do, TN)
            ds = p * (_dot(do, v_ref[...], NT) - delta)
            dk_s[...] += _dot(ds, q_ref[...], TN)
            dc_s[...] -= jnp.sum(ds, axis=0, keepdims=True)

        @pl.when(i == j)
        def _():
            step(True)

        @pl.when(i > j)
        def _():
            step(False)

        @pl.when(i == nq - 1)
        def _():
            dk_ref[...] = (dk_s[...] * (HEAD_DIM ** -0.5)).astype(dk_ref.dtype)
            dv_ref[...] = dv_s[...].astype(dv_ref.dtype)
            dc_ref[0] = dc_s[...]

    qs, kv, c_spec, ct_spec, lse_spec = _attn_specs(tq)
    return pl.pallas_call(
        body, name=name,
        grid_spec=pltpu.PrefetchScalarGridSpec(
            num_scalar_prefetch=2, grid=(HEADS, i_tab.shape[0]),
            in_specs=[qs(qb), kv(qb + HEADS), kv(qb + 2 * HEADS), qs(zb), c_spec, ct_spec, qs(0), qs(0), lse_spec],
            out_specs=[kv(0), kv(0), pl.BlockSpec((1, 1, tq), lambda h, t, it, jt: (h, 0, jt[t]))],
            scratch_shapes=[pltpu.VMEM((tq, HEAD_DIM), F32), pltpu.VMEM((tq, HEAD_DIM), F32), pltpu.VMEM((1, tq), F32)]),
        out_shape=[jax.ShapeDtypeStruct((seq, BRANCH), BF16)] * 2 + [jax.ShapeDtypeStruct((HEADS, 1, seq), F32)],
        compiler_params=_params(("parallel", "arbitrary")),
    )(i_tab, j_tab, proj, proj, proj, proj, c, ct3, dy, o, lse)


def _loss_head(name, y, target):
    seq, d = y.shape
    tile = min(256, seq)

    def body(y_ref, t_ref, dy_ref, l_ref):
        err = y_ref[...] - t_ref[...]
        dy_ref[...] = err / d

        @pl.when(pl.program_id(0) == 0)
        def _():
            l_ref[...] = jnp.zeros_like(l_ref)

        l_ref[...] += 0.5 * jnp.sum(jnp.mean(err * err, axis=-1, keepdims=True), axis=0, keepdims=True)

    return pl.pallas_call(
        body, name=name, grid=(seq // tile,),
        in_specs=[pl.BlockSpec((tile, d), lambda i: (i, 0))] * 2,
        out_specs=[pl.BlockSpec((tile, d), lambda i: (i, 0)), pl.BlockSpec((8, LANE), lambda i: (0, 0))],
        out_shape=[jax.ShapeDtypeStruct((seq, d), F32), jax.ShapeDtypeStruct((8, LANE), F32)],
        compiler_params=_params(("arbitrary",)),
    )(y, target)


def _adamw(name, w, m, v, g_parts, tile):
    rows, cols = w.shape
    tile = min(tile, rows)
    assert rows % tile == 0, (name, w.shape)
    n_g = len(g_parts)

    def body(*refs):
        w_ref, m_ref, v_ref = refs[:3]
        g_refs = refs[3:3 + n_g]
        g_ref, d_ref, nm_ref, nv_ref = refs[3 + n_g:]
        g = None
        for r in g_refs:
            parts = [r[...]] if len(r.shape) == 2 else [r[i] for i in range(r.shape[0])]
            for p in parts:
                g = p.astype(F32) if g is None else g + p.astype(F32)
        m_new = ADAM_B1 * m_ref[...] + (1.0 - ADAM_B1) * g
        v_new = ADAM_B2 * v_ref[...] + (1.0 - ADAM_B2) * (g * g)
        m_hat = m_new / (1.0 - ADAM_B1 ** ADAM_STEP)
        v_hat = v_new / (1.0 - ADAM_B2 ** ADAM_STEP)
        g_ref[...] = g
        d_ref[...] = -ADAM_LR * (m_hat / (jnp.sqrt(v_hat) + ADAM_EPS) + ADAM_WD * w_ref[...])
        nm_ref[...] = m_new
        nv_ref[...] = v_new

    blk = pl.BlockSpec((tile, cols), lambda i: (i, 0))
    g_specs = [blk if p.ndim == 2 else pl.BlockSpec((p.shape[0], tile, cols), lambda i: (0, i, 0)) for p in g_parts]
    return pl.pallas_call(
        body, name=name, grid=(rows // tile,), in_specs=[blk] * 3 + g_specs, out_specs=[blk] * 4,
        out_shape=[jax.ShapeDtypeStruct((rows, cols), F32)] * 4,
        compiler_params=_params(("parallel",)),
    )(w, m, v, *g_parts)


_ORIG_SEGMENTS = (
    ("qkv_a", 0, 3072), ("z_a", 3072, 1024), ("beta", 4096, 8), ("alpha", 4104, 8), ("glu", 4112, 2048),
    ("z_b", 6160, 1024), ("qkv_c", 7184, 3072), ("z_c", 10256, 1024), ("forget", 11280, 8), ("gate", 11288, 6144))
_PAD_ORDER = ("gate", "qkv_a", "z_a", "glu", "z_b", "qkv_c", "z_c", "beta", "alpha", "forget")


def _pad_cols(w):
    seg = {n: w[..., s:s + k] for n, s, k in _ORIG_SEGMENTS}
    fill = jnp.zeros(w.shape[:-1] + (N_PAD - N_IN,), w.dtype)
    return jnp.concatenate([seg[n] for n in _PAD_ORDER] + [fill], axis=-1)


def _unpad_cols(g):
    off, seg = 0, {}
    widths = {n: k for n, _, k in _ORIG_SEGMENTS}
    for n in _PAD_ORDER:
        seg[n] = g[..., off:off + widths[n]]
        off += widths[n]
    return jnp.concatenate([seg[n] for n, _, _ in _ORIG_SEGMENTS], axis=-1)


def _lane_row(vals, lane0):
    return jnp.pad(vals.astype(F32), (lane0, LANE - HEADS - lane0))[None]


def _layer_fwd(x, p):
    seq = x.shape[0]
    h = _rowwise("rms_pre", _rms_pre_fn, [(x, D_MODEL, 0)], [p["pre_w"]], [(D_MODEL, BF16)], 256)[0]
    proj = _matmul("mm_in", h, p["w_in"], "nn", F32, 512, 1280, 2048)
    ca = _conv_fwd("conv_a", proj, OFF_QKVA, 3 * BRANCH, p["w4"], jnp.zeros((1, 3 * BRANCH), F32), SHORT_CONV)
    y_a, states = _gdr_fwd("gdr_fwd", ca, proj, p["alog"], p["dtb"], p["onw"])
    u2 = _conv_fwd("conv_b", proj, OFF_VAL, BRANCH, p["w31"], p["cb"], CONF_CONV, gate_off=OFF_GLUG)
    y_b = _rowwise("ln_gate", _ln_gate_fn, [(u2, BRANCH, 0), (proj, BRANCH, OFF_ZB // BRANCH)],
                   [p["ln_w"], p["ln_b"]], [(BRANCH, BF16)], 256)[0]
    c, ct = _fox_gate_fwd("fox_gate", proj, p["fb"])
    ct3 = ct.reshape(LANE, 1, seq)
    y_c, o_c, lse = _attn_fwd("attn_fwd", proj, c, ct3)
    ys = (y_a, y_b, y_c)
    br = [_matmul("mm_br", ys[n], p["wbr"][n], "nn", F32, 512, 2048, 1024) for n in range(N_BRANCH)]
    merged = _rowwise("merge", _merge_fn, [(proj, D_MODEL, n) for n in range(N_BRANCH)] + [(b, D_MODEL, 0) for b in br],
                      [], [(D_MODEL, BF16)], 256)[0]
    out = _matmul("mm_out", merged, p["wout"], "nn", F32, 512, 2048, 2048)
    x_new = _rowwise("rms_post", _rms_post_fn, [(out, D_MODEL, 0), (x, D_MODEL, 0)], [p["post_w"]],
                     [(D_MODEL, F32)], 256)[0]
    saved = dict(x=x, h=h, proj=proj, ca=ca, states=states, u2=u2, c=c, ct3=ct3, o_c=o_c, lse=lse, ys=ys, br=br,
                 merged=merged, out=out)
    return x_new, saved


def _layer_bwd(dxn, p, sv):
    x, proj = sv["x"], sv["proj"]
    seq = x.shape[0]
    g = {}
    d_out, g["post_w"] = _rowwise_bwd("rms_post_bwd", _rms_only_fn, [(sv["out"], D_MODEL, 0)], [p["post_w"]], [dxn],
                                      [BF16], 256)
    d_merged = _matmul("mm_out_dx", d_out, p["wout"], "nt", F32, 512, 2048, 2048)
    g["wout"] = _matmul("mm_out_dw", sv["merged"], d_out, "tn", F32, 1024, 1024, 512)
    rows = [(proj, D_MODEL, n) for n in range(N_BRANCH)] + [(b, D_MODEL, 0) for b in sv["br"]]
    d_gl0, d_gl1, d_gl2, d_b0, d_b1, d_b2 = _rowwise_bwd("merge_bwd", _merge_fn, rows, [], [d_merged], [BF16] * 6, 128)
    d_br = (d_b0, d_b1, d_b2)
    dys = [_matmul("mm_br_dx", d_br[n], p["wbr"][n], "nt", BF16, 512, 1024, 2048) for n in range(N_BRANCH)]
    g["wbr"] = jnp.stack([_matmul("mm_br_dw", sv["ys"][n], d_br[n], "tn", F32, 1024, 1024, 512)
                          for n in range(N_BRANCH)])
    dq, dzc = _attn_dq("attn_dq", dys[2], sv["o_c"], sv["lse"], proj, sv["c"], sv["ct3"])
    dk, dv, dck = _attn_dkv("attn_dkv", dys[2], sv["o_c"], sv["lse"], proj, sv["c"], sv["ct3"])
    dsm_c, g["fb"] = _fox_gate_bwd("fox_gate_bwd", dck.reshape(HEADS, seq), proj, p["fb"])
    du2, dzb, g["ln_w"], g["ln_b"] = _rowwise_bwd(
        "ln_gate_bwd", _ln_gate_fn, [(sv["u2"], BRANCH, 0), (proj, BRANCH, OFF_ZB // BRANCH)], [p["ln_w"], p["ln_b"]],
        [dys[1]], [F32, BF16], 256)
    dval, dgate, g["w31"], g["cb"] = _conv_bwd("conv_b_bwd", du2, proj, OFF_VAL, BRANCH, p["w31"], CONF_CONV,
                                               gate_off=OFF_GLUG)
    dca, dza, dsm_a, g["alog"], g["dtb"], g["onw"] = _gdr_bwd("gdr_bwd", dys[0], sv["states"], sv["ca"], proj,
                                                              p["alog"], p["dtb"], p["onw"])
    dqkva, g["w4"], _ = _conv_bwd("conv_a_bwd", dca, proj, OFF_QKVA, 3 * BRANCH, p["w4"], SHORT_CONV)
    d_small = jnp.pad((dsm_a + dsm_c).astype(BF16), ((0, 0), (0, N_PAD - OFF_SMALL - LANE)))
    d_proj = jnp.concatenate([d_gl0, d_gl1, d_gl2, dqkva, dza, dval, dgate, dzb, dq, dk, dv, dzc, d_small], axis=1)
    dh = _matmul("mm_in_dx", d_proj, p["w_in"], "nt", F32, 512, 2048, 1280)
    g["w_in"] = _matmul("mm_in_dw", sv["h"], d_proj, "tn", F32, 2048, 640, 512)
    dx, g["pre_w"] = _rowwise_bwd("rms_pre_bwd", _rms_pre_res_fn, [(x, D_MODEL, 0)], [p["pre_w"]], [dh, dxn], [F32], 256)
    return dx, g


N_CHIPS = 4
N_DEV = 8
HBM_SPEC = pl.BlockSpec(memory_space=pltpu.HBM)


def _mesh_pos():
    return lax.axis_index("x"), lax.axis_index("y"), lax.axis_index("c")


def _other_chips(x, y):
    return [(1 - x, y), (x, 1 - y), (1 - x, 1 - y)]


def _allgather_chips(name, arrs):
    n = len(arrs)
    half = DEPTH // 2

    def body(*refs):
        ins, outs = refs[:n], refs[n:2 * n]
        send_sems, recv_sems, pass_send, pass_recv, local_sems = refs[2 * n:]
        x, y, c = _mesh_pos()
        me = 2 * x + y
        mine, theirs = pl.ds(half * c, half), pl.ds(half * (1 - c), half)
        chips = [2 * px + py for px, py in _other_chips(x, y)]

        def ici(a, j, slot):
            px, py = _other_chips(x, y)[j]
            return pltpu.make_async_remote_copy(
                src_ref=ins[a].at[mine], dst_ref=outs[a].at[slot, mine], send_sem=send_sems.at[3 * a + j],
                recv_sem=recv_sems.at[3 * a + j], device_id=(px, py, c), device_id_type=MESH)

        def d2d(a, j, layers):
            blk = outs[a].at[chips[j], layers]
            return pltpu.make_async_remote_copy(
                src_ref=blk, dst_ref=blk, send_sem=pass_send.at[3 * a + j], recv_sem=pass_recv.at[3 * a + j],
                device_id=(x, y, 1 - c), device_id_type=MESH)

        local = [pltpu.make_async_copy(ins[a], outs[a].at[me], local_sems.at[a]) for a in range(n)]
        for a in range(n):
            local[a].start()
            for j in range(3):
                ici(a, j, me).start()
        for a in range(n):
            for j in range(3):
                ici(a, j, chips[j]).wait_recv()
                d2d(a, j, mine).start()
        for a in range(n):
            for j in range(3):
                d2d(a, j, theirs).wait_recv()
        for a in range(n):
            for j in range(3):
                ici(a, j, me).wait_send()
                d2d(a, j, mine).wait_send()
            local[a].wait()

    return pl.pallas_call(
        body, name=name, in_specs=[HBM_SPEC] * n, out_specs=[HBM_SPEC] * n,
        out_shape=[jax.ShapeDtypeStruct((N_CHIPS,) + a.shape, a.dtype) for a in arrs],
        scratch_shapes=[pltpu.SemaphoreType.DMA((3 * n,))] * 4 + [pltpu.SemaphoreType.DMA((n,))],
    )(*arrs)


def _halves_to_sibling(name, arrs):
    n = len(arrs)

    def body(*refs):
        ins, outs = refs[:n], refs[n:2 * n]
        send_sems, recv_sems = refs[2 * n:]
        x, y, c = _mesh_pos()
        copies = []
        for a in range(n):
            rows = ins[a].shape[1] // 2
            for s in range(N_CHIPS):
                copies.append(pltpu.make_async_remote_copy(
                    src_ref=ins[a].at[s, pl.ds((1 - c) * rows, rows)], dst_ref=outs[a].at[s],
                    send_sem=send_sems.at[N_CHIPS * a + s], recv_sem=recv_sems.at[N_CHIPS * a + s],
                    device_id=(x, y, 1 - c), device_id_type=MESH))
        for cp in copies:
            cp.start()
        for cp in copies:
            cp.wait_recv()
        for cp in copies:
            cp.wait_send()

    return pl.pallas_call(
        body, name=name, in_specs=[HBM_SPEC] * n, out_specs=[HBM_SPEC] * n,
        out_shape=[jax.ShapeDtypeStruct((a.shape[0], a.shape[1] // 2, a.shape[2]), a.dtype) for a in arrs],
        scratch_shapes=[pltpu.SemaphoreType.DMA((N_CHIPS * n,)), pltpu.SemaphoreType.DMA((N_CHIPS * n,))],
    )(*arrs)


def _add_own_half(name, full, other, core, tile):
    n, rows, cols = other.shape
    tile = min(tile, rows)
    assert rows % tile == 0, (name, other.shape)
    nb = rows // tile

    def body(c_ref, f_ref, o_ref, out_ref):
        out_ref[...] = f_ref[...] + o_ref[...]

    return pl.pallas_call(
        body, name=name,
        grid_spec=pltpu.PrefetchScalarGridSpec(
            num_scalar_prefetch=1, grid=(n, nb),
            in_specs=[pl.BlockSpec((1, tile, cols), lambda s, i, c_ref: (s, c_ref[0] * nb + i, 0)),
                      pl.BlockSpec((1, tile, cols), lambda s, i, c_ref: (s, i, 0))],
            out_specs=pl.BlockSpec((1, tile, cols), lambda s, i, c_ref: (s, i, 0))),
        out_shape=jax.ShapeDtypeStruct(other.shape, F32),
        compiler_params=_params(("parallel", "parallel")),
    )(core, full, other)


def _reduce_scatter_chips(name, arrs):
    n = len(arrs)

    def body(*refs):
        ins, outs = refs[:n], refs[n:2 * n]
        send_sems, recv_sems, local_sems = refs[2 * n:]
        x, y, c = _mesh_pos()
        me = 2 * x + y

        def remote(a, j, slot):
            px, py = _other_chips(x, y)[j]
            return pltpu.make_async_remote_copy(
                src_ref=ins[a].at[2 * px + py], dst_ref=outs[a].at[slot], send_sem=send_sems.at[3 * a + j],
                recv_sem=recv_sems.at[3 * a + j], device_id=(px, py, c), device_id_type=MESH)

        local = [pltpu.make_async_copy(ins[a].at[me], outs[a].at[me], local_sems.at[a]) for a in range(n)]
        for a in range(n):
            local[a].start()
            for j in range(3):
                remote(a, j, me).start()
        for a in range(n):
            for j, (px, py) in enumerate(_other_chips(x, y)):
                remote(a, j, 2 * px + py).wait_recv()
        for a in range(n):
            for j in range(3):
                remote(a, j, me).wait_send()
            local[a].wait()

    return pl.pallas_call(
        body, name=name, in_specs=[HBM_SPEC] * n, out_specs=[HBM_SPEC] * n,
        out_shape=[jax.ShapeDtypeStruct(a.shape, a.dtype) for a in arrs],
        scratch_shapes=[pltpu.SemaphoreType.DMA((3 * n,)), pltpu.SemaphoreType.DMA((3 * n,)),
                        pltpu.SemaphoreType.DMA((n,))],
    )(*arrs)


def _join_cores(name, arrs):
    n = len(arrs)

    def body(*refs):
        ins, outs = refs[:n], refs[n:2 * n]
        send_sems, recv_sems, local_sems = refs[2 * n:]
        x, y, c = _mesh_pos()
        local = [pltpu.make_async_copy(ins[a], outs[a].at[c], local_sems.at[a]) for a in range(n)]
        remote = [pltpu.make_async_remote_copy(
            src_ref=ins[a], dst_ref=outs[a].at[c], send_sem=send_sems.at[a], recv_sem=recv_sems.at[a],
            device_id=(x, y, 1 - c), device_id_type=MESH) for a in range(n)]
        for a in range(n):
            local[a].start()
            remote[a].start()
        for a in range(n):
            pltpu.make_async_remote_copy(
                src_ref=ins[a], dst_ref=outs[a].at[1 - c], send_sem=send_sems.at[a], recv_sem=recv_sems.at[a],
                device_id=(x, y, 1 - c), device_id_type=MESH).wait_recv()
        for a in range(n):
            remote[a].wait_send()
            local[a].wait()

    return pl.pallas_call(
        body, name=name, in_specs=[HBM_SPEC] * n, out_specs=[HBM_SPEC] * n,
        out_shape=[jax.ShapeDtypeStruct((2,) + a.shape, a.dtype) for a in arrs],
        scratch_shapes=[pltpu.SemaphoreType.DMA((n,)), pltpu.SemaphoreType.DMA((n,)), pltpu.SemaphoreType.DMA((n,))],
    )(*arrs)


def _allgather_devices(name, buf):
    def body(in_ref, out_ref, send_sems, recv_sems):
        x, y, c = _mesh_pos()
        me = 4 * x + 2 * y + c
        out_ref[me] = in_ref[...]

        def remote(k, slot):
            peer = (x ^ (k >> 2), y ^ ((k >> 1) & 1), c ^ (k & 1))
            return pltpu.make_async_remote_copy(
                src_ref=in_ref, dst_ref=out_ref.at[slot], send_sem=send_sems.at[k - 1], recv_sem=recv_sems.at[k - 1],
                device_id=peer, device_id_type=MESH)

        for k in range(1, N_DEV):
            remote(k, me).start()
        for k in range(1, N_DEV):
            remote(k, me ^ k).wait_recv()
        for k in range(1, N_DEV):
            remote(k, me).wait_send()

    vmem = pl.BlockSpec(memory_space=pltpu.VMEM)
    return pl.pallas_call(
        body, name=name, in_specs=[vmem], out_specs=vmem,
        out_shape=jax.ShapeDtypeStruct((N_DEV,) + buf.shape, buf.dtype),
        scratch_shapes=[pltpu.SemaphoreType.DMA((N_DEV - 1,)), pltpu.SemaphoreType.DMA((N_DEV - 1,))],
    )(buf)


def _sum_slots(name, a, tile):
    n, rows, cols = a.shape
    tile = min(tile, rows)
    assert rows % tile == 0, (name, a.shape)

    def body(a_ref, o_ref):
        acc = a_ref[0].astype(F32)
        for i in range(1, n):
            acc = acc + a_ref[i].astype(F32)
        o_ref[...] = acc

    return pl.pallas_call(
        body, name=name, grid=(rows // tile,),
        in_specs=[pl.BlockSpec((n, tile, cols), lambda i: (0, i, 0))],
        out_specs=pl.BlockSpec((tile, cols), lambda i: (i, 0)),
        out_shape=jax.ShapeDtypeStruct((rows, cols), F32),
        compiler_params=_params(("parallel",)),
    )(a)


_SMALL = (
    ("pre_norm_w", (DEPTH, D_MODEL)), ("post_norm_w", (DEPTH, D_MODEL)), ("a_log", (DEPTH, HEADS)),
    ("dt_bias", (DEPTH, HEADS)), ("o_norm_w", (DEPTH, HEAD_DIM)), ("conv_b", (DEPTH, BRANCH)), ("ln_w", (DEPTH, BRANCH)),
    ("ln_b", (DEPTH, BRANCH)), ("f_bias", (DEPTH, HEADS)), ("conv_qkv_w", (DEPTH, SHORT_CONV, 3 * BRANCH)),
    ("conv_w", (DEPTH, CONF_CONV, BRANCH)), ("loss", (1,)))
_SHARDED_SMALL = {"conv_qkv_w": 3 * BRANCH // N_CHIPS, "conv_w": BRANCH // N_CHIPS}
_WEIGHTS = ("pre_norm_w", "post_norm_w", "w_in", "conv_qkv_w", "a_log", "dt_bias", "o_norm_w", "conv_w", "conv_b",
            "ln_w", "ln_b", "f_bias", "w_branch", "w_out")


def _pack(parts):
    flat = jnp.concatenate([p.reshape(-1).astype(F32) for p in parts])
    rows = -(-flat.shape[0] // (8 * LANE)) * 8
    return jnp.pad(flat, (0, rows * LANE - flat.shape[0])).reshape(rows, LANE)


def _unpack(buf, shapes):
    flat, out, off = buf.reshape(-1), [], 0
    for shp in shapes:
        size = 1
        for s in shp:
            size *= s
        out.append(flat[off:off + size].reshape(shp))
        off += size
    return out


def kernel(x, pre_norm_w, post_norm_w, w_in, conv_qkv_w, a_log, dt_bias, o_norm_w, conv_w, conv_b, ln_w, ln_b, f_bias, w_branch, w_out, loss_target, m_pre_norm_w, m_post_norm_w, m_w_in, m_conv_qkv_w, m_a_log, m_dt_bias, m_o_norm_w, m_conv_w, m_conv_b, m_ln_w, m_ln_b, m_f_bias, m_w_branch, m_w_out, v_pre_norm_w, v_post_norm_w, v_w_in, v_conv_qkv_w, v_a_log, v_dt_bias, v_o_norm_w, v_conv_w, v_conv_b, v_ln_w, v_ln_b, v_f_bias, v_w_branch, v_w_out):
    weights = dict(pre_norm_w=pre_norm_w, post_norm_w=post_norm_w, w_in=w_in, conv_qkv_w=conv_qkv_w, a_log=a_log,
                   dt_bias=dt_bias, o_norm_w=o_norm_w, conv_w=conv_w, conv_b=conv_b, ln_w=ln_w, ln_b=ln_b, f_bias=f_bias,
                   w_branch=w_branch, w_out=w_out)
    mom1 = dict(pre_norm_w=m_pre_norm_w, post_norm_w=m_post_norm_w, w_in=m_w_in, conv_qkv_w=m_conv_qkv_w, a_log=m_a_log,
                dt_bias=m_dt_bias, o_norm_w=m_o_norm_w, conv_w=m_conv_w, conv_b=m_conv_b, ln_w=m_ln_w, ln_b=m_ln_b,
                f_bias=m_f_bias, w_branch=m_w_branch, w_out=m_w_out)
    mom2 = dict(pre_norm_w=v_pre_norm_w, post_norm_w=v_post_norm_w, w_in=v_w_in, conv_qkv_w=v_conv_qkv_w, a_log=v_a_log,
                dt_bias=v_dt_bias, o_norm_w=v_o_norm_w, conv_w=v_conv_w, conv_b=v_conv_b, ln_w=v_ln_w, ln_b=v_ln_b,
                f_bias=v_f_bias, w_branch=v_w_branch, w_out=v_w_out)
    chip = 2 * lax.axis_index("x") + lax.axis_index("y")
    core = lax.axis_index("c").astype(jnp.int32).reshape(1)

    gw_in, gw_br, gw_out, g_c4, g_c31 = _allgather_chips(
        "allgather_weights", [w_in.astype(BF16), w_branch.astype(BF16), w_out.astype(BF16), conv_qkv_w, conv_w])
    cat_last = lambda g: jnp.concatenate([g[s] for s in range(N_CHIPS)], axis=-1)
    w_in_full = _pad_cols(cat_last(gw_in))
    w_br_full = cat_last(gw_br)
    w_out_full = jnp.concatenate([gw_out[s] for s in range(N_CHIPS)], axis=1)
    c4_full = jnp.pad(cat_last(g_c4), ((0, 0), (0, 8 - SHORT_CONV), (0, 0)))
    c31_full = jnp.pad(cat_last(g_c31), ((0, 0), (0, 32 - CONF_CONV), (0, 0)))

    layers = []
    for l in range(DEPTH):
        layers.append(dict(
            pre_w=pre_norm_w[l][None], post_w=post_norm_w[l][None], w_in=w_in_full[l], w4=c4_full[l],
            alog=_lane_row(a_log[l], ALPHA_LANE), dtb=_lane_row(dt_bias[l], ALPHA_LANE), onw=o_norm_w[l][None],
            w31=c31_full[l], cb=conv_b[l][None], ln_w=ln_w[l][None], ln_b=ln_b[l][None],
            fb=_lane_row(f_bias[l], FORGET_LANE), wbr=w_br_full[l], wout=w_out_full[l]))

    act = x[0]
    saved = []
    for l in range(DEPTH):
        act, sv = _layer_fwd(act, layers[l])
        saved.append(sv)
    d_act, loss_blk = _loss_head("loss_head", act, loss_target[0])

    parts_in, parts_br, parts_out, small_g = [], [], [], []
    for l in reversed(range(DEPTH)):
        d_act, g = _layer_bwd(d_act, layers[l], saved[l])
        g_in = _unpad_cols(g["w_in"]).reshape(D_MODEL, N_CHIPS, N_IN // N_CHIPS).transpose(1, 0, 2)
        g_br = g["wbr"].reshape(N_BRANCH * BRANCH, N_CHIPS, D_MODEL // N_CHIPS).transpose(1, 0, 2)
        g_out = g["wout"].reshape(N_CHIPS, D_MODEL // N_CHIPS, D_MODEL)
        parts = [g_in, g_br, g_out]
        other = _halves_to_sibling("halves_to_sibling", parts)
        summed = [_add_own_half("add_own_half", f, o, core, t) for f, o, t in zip(parts, other, (64, 512, 128))]
        recv = _reduce_scatter_chips("reduce_scatter_grads", summed)
        mine = [_sum_slots("sum_chips", r, t) for r, t in zip(recv, (64, 512, 128))]
        joined = _join_cores("join_cores", mine)
        for lst, j in zip((parts_in, parts_br, parts_out), joined):
            lst.append(j.reshape(2 * j.shape[1], j.shape[2]))
        small_g.append(g)
    small_g = small_g[::-1]
    parts_in, parts_br, parts_out = parts_in[::-1], parts_br[::-1], parts_out[::-1]

    stack = lambda key, f=lambda a: a: jnp.stack([f(g[key]) for g in small_g])
    small = dict(
        pre_norm_w=stack("pre_w", lambda a: a[0]), post_norm_w=stack("post_w", lambda a: a[0]),
        a_log=stack("alog", lambda a: a[0, ALPHA_LANE:ALPHA_LANE + HEADS]),
        dt_bias=stack("dtb", lambda a: a[0, ALPHA_LANE:ALPHA_LANE + HEADS]), o_norm_w=stack("onw", lambda a: a[0]),
        conv_b=stack("cb", lambda a: a[0]), ln_w=stack("ln_w", lambda a: a[0]), ln_b=stack("ln_b", lambda a: a[0]),
        f_bias=stack("fb", lambda a: a[0, FORGET_LANE:FORGET_LANE + HEADS]),
        conv_qkv_w=stack("w4", lambda a: a[:SHORT_CONV]), conv_w=stack("w31", lambda a: a[:CONF_CONV]),
        loss=loss_blk[0, 0:1])
    gathered = _allgather_devices("allgather_small", _pack([small[n] for n, _ in _SMALL]))
    total = _unpack(_sum_slots("sum_devices", gathered, gathered.shape[1]), [s for _, s in _SMALL])
    total = {n: t for (n, _), t in zip(_SMALL, total)}
    loss = total.pop("loss")[0]
    for n, width in _SHARDED_SMALL.items():
        total[n] = lax.dynamic_slice_in_dim(total[n], chip * width, width, axis=2)

    names = list(total)
    packed = [_pack([d[n] for n in names]) for d in (weights, mom1, mom2)]
    res = _adamw("adamw_small", packed[0], packed[1], packed[2], [_pack([total[n] for n in names])], packed[0].shape[0])
    shapes = [weights[n].shape for n in names]
    grads, delta, new_m, new_v = [dict(zip(names, _unpack(r, shapes))) for r in res]
    big = (("w_in", parts_in, (DEPTH * D_MODEL, N_IN // N_CHIPS), 64),
           ("w_branch", parts_br, (DEPTH * N_BRANCH * BRANCH, D_MODEL // N_CHIPS), 512),
           ("w_out", parts_out, (DEPTH * D_MODEL // N_CHIPS, D_MODEL), 128))
    for n, parts, shape2, tile in big:
        res = _adamw("adamw_" + n, weights[n].reshape(shape2), mom1[n].reshape(shape2), mom2[n].reshape(shape2),
                     [jnp.concatenate(parts, axis=0)], tile)
        grads[n], delta[n], new_m[n], new_v[n] = [r.reshape(weights[n].shape) for r in res]

    outs = [loss, d_act[None]]
    for d in (grads, delta, new_m, new_v):
        outs += [d[n] for n in _WEIGHTS]
    return tuple(outs)
```

```python
import functools

import jax
import jax.numpy as jnp
from jax import lax
from jax.experimental import pallas as pl
from jax.experimental.pallas import tpu as pltpu

F32 = jnp.float32
BF16 = jnp.bfloat16
MXU_DTYPE = jnp.bfloat16

D_MODEL = 2048
DEPTH = 4
BRANCH = 1024
HEAD_DIM = 128
HEADS = 8
CHUNK = 64
SHORT_CONV = 4
CONF_CONV = 31
N_BRANCH = 3
NORM_EPS = 1e-6
N_IN = 17432

OFF_GATE = 0
OFF_QKVA = 6144
OFF_ZA = 9216
OFF_VAL = 10240
OFF_GLUG = 11264
OFF_ZB = 12288
OFF_QKVC = 13312
OFF_ZC = 16384
OFF_SMALL = 17408
N_PAD = 17920
LANE = 128
BETA_LANE, ALPHA_LANE, FORGET_LANE = 0, 8, 16

ADAM_LR = 0.001
ADAM_B1 = 0.9
ADAM_B2 = 0.999
ADAM_EPS = 1e-08
ADAM_WD = 0.01
ADAM_STEP = 10

VMEM_LIMIT = 56 * 1024 * 1024

NN = (((1,), (0,)), ((), ()))
NT = (((1,), (1,)), ((), ()))
TN = (((0,), (0,)), ((), ()))
MESH = pl.DeviceIdType.MESH


def _params(sem=None):
    return pltpu.CompilerParams(dimension_semantics=sem, vmem_limit_bytes=VMEM_LIMIT)


def _dot(a, b, dims=NN):
    return lax.dot_general(a.astype(MXU_DTYPE), b.astype(MXU_DTYPE), dims, preferred_element_type=F32)


def _dot_hi(a, b, dims=NN):
    return lax.dot_general(a, b, dims, precision=lax.Precision.HIGHEST, preferred_element_type=F32)


def _sigmoid(x):
    return 1.0 / (1.0 + jnp.exp(-x))


def _silu(x):
    return x * _sigmoid(x)


def _softplus(x):
    return jnp.maximum(x, 0.0) + jnp.log(1.0 + jnp.exp(-jnp.abs(x)))


def _colsel(m, j):
    lane = lax.broadcasted_iota(jnp.int32, m.shape, 1)
    return jnp.sum(jnp.where(lane == j, m, 0.0), axis=1, keepdims=True)


def _rowsel(m, j):
    sub = lax.broadcasted_iota(jnp.int32, m.shape, 0)
    return jnp.sum(jnp.where(sub == j, m, 0.0), axis=0, keepdims=True)


def _row_specs(rows, tile):
    return [pl.BlockSpec((tile, w), functools.partial(lambda i, cb: (i, cb), cb=cb)) for (_, w, cb) in rows]


def _rowwise(name, fn, rows, params, outs, tile):
    seq = rows[0][0].shape[0]
    tile = min(tile, seq)
    n_in = len(rows) + len(params)

    def body(*refs):
        res = fn(*[r[...] for r in refs[:n_in]])
        for o_ref, r in zip(refs[n_in:], res):
            o_ref[...] = r.astype(o_ref.dtype)

    return pl.pallas_call(
        body, name=name, grid=(seq // tile,),
        in_specs=_row_specs(rows, tile) + [pl.BlockSpec(p.shape, lambda i: (0, 0)) for p in params],
        out_specs=[pl.BlockSpec((tile, w), lambda i: (i, 0)) for (w, _) in outs],
        out_shape=[jax.ShapeDtypeStruct((seq, w), dt) for (w, dt) in outs],
        compiler_params=_params(("parallel",)),
    )(*[r[0] for r in rows], *params)


def _rowwise_bwd(name, fn, rows, params, cts, row_grads, tile):
    seq = rows[0][0].shape[0]
    tile = min(tile, seq)
    nr, npar, nct = len(rows), len(params), len(cts)
    n_in = nr + npar

    def body(*refs):
        vals = [r[...] for r in refs[:n_in]]
        res, vjp = jax.vjp(fn, *vals)
        grads = vjp(tuple(c[...].astype(r.dtype) for c, r in zip(refs[n_in:n_in + nct], res)))
        outs = refs[n_in + nct:]
        k = 0
        for idx, dt in enumerate(row_grads):
            if dt is not None:
                outs[k][...] = grads[idx].astype(dt)
                k += 1
        first = pl.program_id(0) == 0
        for j in range(npar):
            g = grads[nr + j].astype(F32)
            o_ref = outs[k + j]

            @pl.when(first)
            def _(o_ref=o_ref, g=g):
                o_ref[...] = g

            @pl.when(jnp.logical_not(first))
            def _(o_ref=o_ref, g=g):
                o_ref[...] += g

    want = [(rows[i][1], dt) for i, dt in enumerate(row_grads) if dt is not None]
    return pl.pallas_call(
        body, name=name, grid=(seq // tile,),
        in_specs=(_row_specs(rows, tile) + [pl.BlockSpec(p.shape, lambda i: (0, 0)) for p in params]
                  + [pl.BlockSpec((tile, c.shape[1]), lambda i: (i, 0)) for c in cts]),
        out_specs=([pl.BlockSpec((tile, w), lambda i: (i, 0)) for (w, _) in want]
                   + [pl.BlockSpec(p.shape, lambda i: (0, 0)) for p in params]),
        out_shape=([jax.ShapeDtypeStruct((seq, w), dt) for (w, dt) in want]
                   + [jax.ShapeDtypeStruct(p.shape, F32) for p in params]),
        compiler_params=_params(("arbitrary",)),
    )(*[r[0] for r in rows], *params, *cts)


def _rms(x, w):
    x = x.astype(F32)
    return x * lax.rsqrt(jnp.mean(x * x, axis=-1, keepdims=True) + NORM_EPS) * w


def _rms_pre_fn(x, w):
    return (_rms(x, w),)


def _rms_pre_res_fn(x, w):
    return (_rms(x, w), x)


def _rms_post_fn(out, x, w):
    return (x + _rms(out, w),)


def _rms_only_fn(out, w):
    return (_rms(out, w),)


def _ln_gate_fn(u, z, w, b):
    u = u.astype(F32)
    uc = u - jnp.mean(u, axis=-1, keepdims=True)
    y = uc * lax.rsqrt(jnp.mean(uc * uc, axis=-1, keepdims=True) + NORM_EPS) * w + b
    return (_silu(y) * _silu(z.astype(F32)),)


def _merge_fn(g0, g1, g2, b0, b1, b2):
    return (_sigmoid(g0) * b0 + _sigmoid(g1) * b1 + _sigmoid(g2) * b2,)


def _matmul(name, a, b, mode, out_dtype, tm, tn, tk):
    if mode == "nn":
        (m, kc), n = a.shape, b.shape[1]
    elif mode == "nt":
        (m, kc), n = a.shape, b.shape[0]
    else:
        (kc, m), n = a.shape, b.shape[1]
    tm, tn, tk = min(tm, m), min(tn, n), min(tk, kc)
    nk = kc // tk
    assert m % tm == 0 and n % tn == 0 and kc % tk == 0, (name, a.shape, b.shape)
    dims = {"nn": NN, "nt": NT, "tn": TN}[mode]
    a_spec = (pl.BlockSpec((tk, tm), lambda j, i, k: (k, i)) if mode == "tn"
              else pl.BlockSpec((tm, tk), lambda j, i, k: (i, k)))
    b_spec = (pl.BlockSpec((tn, tk), lambda j, i, k: (j, k)) if mode == "nt"
              else pl.BlockSpec((tk, tn), lambda j, i, k: (k, j)))
    use_acc = nk > 1 and out_dtype != F32

    def body(a_ref, b_ref, o_ref, *acc):
        p = _dot(a_ref[...], b_ref[...], dims)
        if nk == 1:
            o_ref[...] = p.astype(out_dtype)
            return
        k = pl.program_id(2)
        dst = acc[0] if use_acc else o_ref

        @pl.when(k == 0)
        def _():
            dst[...] = p

        @pl.when(k > 0)
        def _():
            dst[...] += p

        if use_acc:
            @pl.when(k == nk - 1)
            def _():
                o_ref[...] = dst[...].astype(out_dtype)

    return pl.pallas_call(
        body, name=name, grid=(n // tn, m // tm, nk),
        in_specs=[a_spec, b_spec],
        out_specs=pl.BlockSpec((tm, tn), lambda j, i, k: (i, j)),
        out_shape=jax.ShapeDtypeStruct((m, n), out_dtype),
        scratch_shapes=[pltpu.VMEM((tm, tn), F32)] if use_acc else [],
        compiler_params=_params(("parallel", "parallel", "arbitrary")),
    )(a, b)


HALO = 32
CONV_TC = 256
CONV_T = 1024


def _conv_fwd(name, x, x_off, ch, w, b, k_width, gate_off=None):
    seq = x.shape[0]
    t_blk = min(CONV_T, seq)
    tc = CONV_TC
    hb = t_blk // HALO
    xcb = x_off // tc
    has_gate = gate_off is not None

    def body(*refs):
        if has_gate:
            xm_ref, xh_ref, gm_ref, gh_ref, w_ref, b_ref, y_ref, win = refs
        else:
            xm_ref, xh_ref, w_ref, b_ref, y_ref, win = refs
        t = pl.program_id(1)
        xm, xh = xm_ref[...], xh_ref[...]
        if has_gate:
            xm = xm * _sigmoid(gm_ref[...])
            xh = xh * _sigmoid(gh_ref[...])
        win[0:HALO, :] = jnp.where(t == 0, 0.0, xh)
        win[HALO:HALO + t_blk, :] = xm
        acc = jnp.broadcast_to(b_ref[...], (t_blk, tc))
        for k in range(k_width):
            acc = acc + w_ref[k:k + 1, :] * win[HALO - (k_width - 1) + k:HALO - (k_width - 1) + k + t_blk, :]
        y_ref[...] = acc

    main = lambda off: pl.BlockSpec((t_blk, tc), lambda c, t: (t, off + c))
    halo = lambda off: pl.BlockSpec((HALO, tc), lambda c, t: (jnp.maximum(t * hb - 1, 0), off + c))
    ins, specs = [x, x], [main(xcb), halo(xcb)]
    if has_gate:
        gcb = gate_off // tc
        ins += [x, x]
        specs += [main(gcb), halo(gcb)]
    ins += [w, b]
    specs += [pl.BlockSpec((w.shape[0], tc), lambda c, t: (0, c)), pl.BlockSpec((1, tc), lambda c, t: (0, c))]
    return pl.pallas_call(
        body, name=name, grid=(ch // tc, seq // t_blk), in_specs=specs,
        out_specs=pl.BlockSpec((t_blk, tc), lambda c, t: (t, c)),
        out_shape=jax.ShapeDtypeStruct((seq, ch), F32),
        scratch_shapes=[pltpu.VMEM((HALO + t_blk, tc), F32)],
        compiler_params=_params(("parallel", "arbitrary")),
    )(*ins)


def _conv_bwd(name, dy, x, x_off, ch, w, k_width, gate_off=None):
    seq = x.shape[0]
    t_blk = min(CONV_T, seq)
    tc = CONV_TC
    hb = t_blk // HALO
    nt = seq // t_blk
    xcb = x_off // tc
    has_gate = gate_off is not None
    kp = w.shape[0]

    def body(*refs):
        if has_gate:
            dm_ref, dh_ref, xm_ref, xh_ref, gm_ref, gh_ref, w_ref, dv_ref, dg_ref, dw_ref, db_ref, winx, wind = refs
        else:
            dm_ref, dh_ref, xm_ref, xh_ref, w_ref, dx_ref, dw_ref, db_ref, winx, wind = refs
        t = pl.program_id(1)
        xm, xh = xm_ref[...], xh_ref[...]
        if has_gate:
            sg = _sigmoid(gm_ref[...])
            um = xm * sg
            uh = xh * _sigmoid(gh_ref[...])
        else:
            um, uh = xm, xh
        winx[0:HALO, :] = jnp.where(t == nt - 1, 0.0, uh)
        winx[HALO:HALO + t_blk, :] = um
        dm = dm_ref[...]
        wind[0:t_blk, :] = dm
        wind[t_blk:t_blk + HALO, :] = jnp.where(t == 0, 0.0, dh_ref[...])
        du = jnp.zeros((t_blk, tc), F32)
        for k in range(k_width):
            du = du + w_ref[k:k + 1, :] * wind[k_width - 1 - k:k_width - 1 - k + t_blk, :]
        if has_gate:
            dv_ref[...] = (du * sg).astype(dv_ref.dtype)
            dg_ref[...] = (du * xm * sg * (1.0 - sg)).astype(dg_ref.dtype)
        else:
            dx_ref[...] = du.astype(dx_ref.dtype)

        @pl.when(t == 0)
        def _():
            dw_ref[...] = jnp.zeros_like(dw_ref)
            db_ref[...] = jnp.zeros_like(db_ref)

        for k in range(k_width):
            s0 = HALO - (k_width - 1) + k
            dw_ref[k:k + 1, :] += jnp.sum(dm * winx[s0:s0 + t_blk, :], axis=0, keepdims=True)
        db_ref[...] += jnp.sum(dm, axis=0, keepdims=True)

    rt = lambda t: nt - 1 - t
    main = lambda off: pl.BlockSpec((t_blk, tc), lambda c, t: (rt(t), off + c))
    past = lambda off: pl.BlockSpec((HALO, tc), lambda c, t: (jnp.maximum(rt(t) * hb - 1, 0), off + c))
    future = pl.BlockSpec((HALO, tc), lambda c, t: (jnp.minimum((rt(t) + 1) * hb, seq // HALO - 1), c))
    ins, specs = [dy, dy, x, x], [main(0), future, main(xcb), past(xcb)]
    if has_gate:
        gcb = gate_off // tc
        ins += [x, x]
        specs += [main(gcb), past(gcb)]
    ins += [w]
    specs += [pl.BlockSpec((kp, tc), lambda c, t: (0, c))]
    blk = pl.BlockSpec((t_blk, tc), lambda c, t: (rt(t), c))
    n_dx = 2 if has_gate else 1
    return pl.pallas_call(
        body, name=name, grid=(ch // tc, nt), in_specs=specs,
        out_specs=[blk] * n_dx + [pl.BlockSpec((kp, tc), lambda c, t: (0, c)), pl.BlockSpec((1, tc), lambda c, t: (0, c))],
        out_shape=[jax.ShapeDtypeStruct((seq, ch), BF16)] * n_dx + [jax.ShapeDtypeStruct((kp, ch), F32),
                                                                    jax.ShapeDtypeStruct((1, ch), F32)],
        scratch_shapes=[pltpu.VMEM((HALO + t_blk, tc), F32), pltpu.VMEM((HALO + t_blk, tc), F32)],
        compiler_params=_params(("parallel", "arbitrary")),
    )(*ins)


@jax.custom_vjp
def _inv_unit_lower(lows):
    n = lows[0].shape[0]
    eye = (lax.broadcasted_iota(jnp.int32, (n, n), 0) == lax.broadcasted_iota(jnp.int32, (n, n), 1)).astype(F32)
    accs = [eye - low for low in lows]
    pws = list(lows)
    steps = 1
    while steps * 2 < n:
        pws = [_dot_hi(pw, pw) for pw in pws]
        accs = [acc + _dot_hi(acc, pw) for acc, pw in zip(accs, pws)]
        steps *= 2
    return tuple(accs)


def _inv_fwd(lows):
    ts = _inv_unit_lower(lows)
    return ts, ts


def _inv_bwd(ts, dts):
    left = [_dot_hi(t, dt, TN) for t, dt in zip(ts, dts)]
    return (tuple(-_dot_hi(l, t, NT) for l, t in zip(left, ts)),)


_inv_unit_lower.defvjp(_inv_fwd, _inv_bwd)


def _gdr_chunk(cqkv, z, sm, alog, dtb, onw, state):
    c = cqkv.shape[0]
    hs = range(HEADS)
    ri = lax.broadcasted_iota(jnp.int32, (c, c), 0)
    ci = lax.broadcasted_iota(jnp.int32, (c, c), 1)
    incl, strict = ri >= ci, ri > ci
    beta_all = _sigmoid(sm)
    la_all = -jnp.exp(alog) * _softplus(sm + dtb)
    g_cols = _dot_hi(incl.astype(F32), la_all)
    g_rows = _dot_hi(la_all, (ri <= ci).astype(F32), TN)
    g_end = jnp.sum(la_all, axis=0, keepdims=True)
    act = _silu(cqkv)
    sl = lambda base, h: slice(base + h * HEAD_DIM, base + (h + 1) * HEAD_DIM)
    q = [act[:, sl(0, h)] for h in hs]
    k = [act[:, sl(BRANCH, h)] for h in hs]
    v = [act[:, sl(2 * BRANCH, h)] for h in hs]
    q = [x * lax.rsqrt(jnp.sum(x * x, axis=-1, keepdims=True) + NORM_EPS) * (HEAD_DIM ** -0.5) for x in q]
    k = [x * lax.rsqrt(jnp.sum(x * x, axis=-1, keepdims=True) + NORM_EPS) for x in k]
    beta = [_colsel(beta_all, BETA_LANE + h) for h in hs]
    g = [_colsel(g_cols, ALPHA_LANE + h) for h in hs]
    g_row = [_rowsel(g_rows, ALPHA_LANE + h) for h in hs]
    g_last = [_colsel(g_end, ALPHA_LANE + h) for h in hs]
    decay = [jnp.where(incl, jnp.exp(jnp.where(incl, g[h] - g_row[h], 0.0)), 0.0) for h in hs]
    kk = [_dot(k[h], k[h], NT) for h in hs]
    qk = [_dot(q[h], k[h], NT) * decay[h] for h in hs]
    t_inv = _inv_unit_lower(tuple(jnp.where(strict, beta[h] * kk[h] * decay[h], 0.0) for h in hs))
    eg = [jnp.exp(g[h]) for h in hs]
    u0 = [_dot(t_inv[h], v[h] * beta[h]) for h in hs]
    w_cum = [_dot(t_inv[h], k[h] * (beta[h] * eg[h])) for h in hs]
    s_in = [state[h] for h in hs]
    u = [u0[h] - _dot(w_cum[h], s_in[h]) for h in hs]
    o = [_dot(q[h] * eg[h], s_in[h]) + _dot(qk[h], u[h]) for h in hs]
    s_out = [s_in[h] * jnp.exp(g_last[h]) + _dot(k[h] * jnp.exp(g_last[h] - g[h]), u[h], TN) for h in hs]
    o = [x * lax.rsqrt(jnp.mean(x * x, axis=-1, keepdims=True) + NORM_EPS) * onw for x in o]
    y = [o[h] * _silu(z[:, sl(0, h)]) for h in hs]
    return jnp.concatenate(y, axis=1), jnp.concatenate([s[None] for s in s_out], axis=0)


def _gdr_specs(nc, order):
    return [
        pl.BlockSpec((CHUNK, 3 * BRANCH), lambda n: (order(n), 0)),
        pl.BlockSpec((CHUNK, BRANCH), lambda n: (order(n), OFF_ZA // BRANCH)),
        pl.BlockSpec((CHUNK, LANE), lambda n: (order(n), OFF_SMALL // LANE)),
        pl.BlockSpec((1, LANE), lambda n: (0, 0)),
        pl.BlockSpec((1, LANE), lambda n: (0, 0)),
        pl.BlockSpec((1, LANE), lambda n: (0, 0)),
    ]


def _gdr_fwd(name, cqkv, proj, alog, dtb, onw):
    seq = cqkv.shape[0]
    nc = seq // CHUNK

    def body(c_ref, z_ref, sm_ref, al_ref, dt_ref, on_ref, y_ref, st_ref, state):
        @pl.when(pl.program_id(0) == 0)
        def _():
            state[...] = jnp.zeros_like(state)

        s_in = state[...]
        st_ref[0] = s_in
        y, s_out = _gdr_chunk(c_ref[...], z_ref[...], sm_ref[...], al_ref[...], dt_ref[...], on_ref[...], s_in)
        y_ref[...] = y.astype(y_ref.dtype)
        state[...] = s_out

    return pl.pallas_call(
        body, name=name, grid=(nc,), in_specs=_gdr_specs(nc, lambda n: n),
        out_specs=[pl.BlockSpec((CHUNK, BRANCH), lambda n: (n, 0)),
                   pl.BlockSpec((1, HEADS, HEAD_DIM, HEAD_DIM), lambda n: (n, 0, 0, 0))],
        out_shape=[jax.ShapeDtypeStruct((seq, BRANCH), BF16),
                   jax.ShapeDtypeStruct((nc, HEADS, HEAD_DIM, HEAD_DIM), F32)],
        scratch_shapes=[pltpu.VMEM((HEADS, HEAD_DIM, HEAD_DIM), F32)],
        compiler_params=_params(("arbitrary",)),
    )(cqkv, proj, proj, alog, dtb, onw)


def _gdr_bwd(name, dy, states, cqkv, proj, alog, dtb, onw):
    seq = cqkv.shape[0]
    nc = seq // CHUNK
    rev = lambda n: nc - 1 - n

    def body(c_ref, z_ref, sm_ref, al_ref, dt_ref, on_ref, dy_ref, st_ref,
             dc_ref, dz_ref, dsm_ref, dal_ref, ddt_ref, don_ref, dstate):
        first = pl.program_id(0) == 0

        @pl.when(first)
        def _():
            dstate[...] = jnp.zeros_like(dstate)
            dal_ref[...] = jnp.zeros_like(dal_ref)
            ddt_ref[...] = jnp.zeros_like(ddt_ref)
            don_ref[...] = jnp.zeros_like(don_ref)

        _, vjp = jax.vjp(_gdr_chunk, c_ref[...], z_ref[...], sm_ref[...], al_ref[...], dt_ref[...], on_ref[...],
                         st_ref[0])
        dc, dz, dsm, dal, ddt, don, ds = vjp((dy_ref[...].astype(F32), dstate[...]))
        dc_ref[...] = dc
        dz_ref[...] = dz.astype(dz_ref.dtype)
        dsm_ref[...] = dsm
        dal_ref[...] += dal
        ddt_ref[...] += ddt
        don_ref[...] += don
        dstate[...] = ds

    small = pl.BlockSpec((1, LANE), lambda n: (0, 0))
    return pl.pallas_call(
        body, name=name, grid=(nc,),
        in_specs=_gdr_specs(nc, rev) + [pl.BlockSpec((CHUNK, BRANCH), lambda n: (rev(n), 0)),
                                        pl.BlockSpec((1, HEADS, HEAD_DIM, HEAD_DIM), lambda n: (rev(n), 0, 0, 0))],
        out_specs=[pl.BlockSpec((CHUNK, 3 * BRANCH), lambda n: (rev(n), 0)),
                   pl.BlockSpec((CHUNK, BRANCH), lambda n: (rev(n), 0)),
                   pl.BlockSpec((CHUNK, LANE), lambda n: (rev(n), 0)), small, small, small],
        out_shape=[jax.ShapeDtypeStruct((seq, 3 * BRANCH), F32), jax.ShapeDtypeStruct((seq, BRANCH), BF16),
                   jax.ShapeDtypeStruct((seq, LANE), F32)] + [jax.ShapeDtypeStruct((1, LANE), F32)] * 3,
        scratch_shapes=[pltpu.VMEM((HEADS, HEAD_DIM, HEAD_DIM), F32)],
        compiler_params=_params(("arbitrary",)),
    )(cqkv, proj, proj, alog, dtb, onw, dy, states)


GATE_T = 512
ATT_T = 512


def _fox_gate_fwd(name, proj, fb):
    seq = proj.shape[0]
    tb = min(GATE_T, seq)

    def body(sm_ref, fb_ref, c_ref, ct_ref):
        tri = (lax.broadcasted_iota(jnp.int32, (tb, tb), 0) >= lax.broadcasted_iota(jnp.int32, (tb, tb), 1)).astype(F32)
        carry = jnp.zeros((1, LANE), F32)
        for i in range(seq // tb):
            lf = -_softplus(-(sm_ref[i * tb:(i + 1) * tb, :] + fb_ref[...]))
            cb = _dot_hi(tri, lf) + carry
            c_ref[i * tb:(i + 1) * tb, :] = cb
            ct_ref[:, i * tb:(i + 1) * tb] = cb.T
            carry = carry + jnp.sum(lf, axis=0, keepdims=True)

    return pl.pallas_call(
        body, name=name, grid=(1,),
        in_specs=[pl.BlockSpec((seq, LANE), lambda i: (0, OFF_SMALL // LANE)), pl.BlockSpec((1, LANE), lambda i: (0, 0))],
        out_specs=[pl.BlockSpec((seq, LANE), lambda i: (0, 0)), pl.BlockSpec((LANE, seq), lambda i: (0, 0))],
        out_shape=[jax.ShapeDtypeStruct((seq, LANE), F32), jax.ShapeDtypeStruct((LANE, seq), F32)],
        compiler_params=_params(("arbitrary",)),
    )(proj, fb)


def _fox_gate_bwd(name, dck, proj, fb):
    seq = proj.shape[0]
    tb = min(GATE_T, seq)
    nb = seq // tb

    def body(d_ref, sm_ref, fb_ref, o_ref, dfb_ref, pad):
        tri = (lax.broadcasted_iota(jnp.int32, (tb, tb), 0) >= lax.broadcasted_iota(jnp.int32, (tb, tb), 1)).astype(F32)
        pad[...] = jnp.zeros_like(pad)
        carry = jnp.zeros((HEADS, 1), F32)
        dfb = jnp.zeros((1, LANE), F32)
        for i in reversed(range(nb)):
            blk = d_ref[:, i * tb:(i + 1) * tb]
            pad[FORGET_LANE:FORGET_LANE + HEADS, :] = _dot_hi(blk, tri) + carry
            carry = carry + jnp.sum(blk, axis=1, keepdims=True)
            x = sm_ref[i * tb:(i + 1) * tb, :] + fb_ref[...]
            dsm = pad[...].T * _sigmoid(-x)
            o_ref[i * tb:(i + 1) * tb, :] = dsm
            dfb = dfb + jnp.sum(dsm, axis=0, keepdims=True)
        dfb_ref[...] = dfb

    return pl.pallas_call(
        body, name=name, grid=(1,),
        in_specs=[pl.BlockSpec((HEADS, seq), lambda i: (0, 0)), pl.BlockSpec((seq, LANE), lambda i: (0, OFF_SMALL // LANE)),
                  pl.BlockSpec((1, LANE), lambda i: (0, 0))],
        out_specs=[pl.BlockSpec((seq, LANE), lambda i: (0, 0)), pl.BlockSpec((1, LANE), lambda i: (0, 0))],
        out_shape=[jax.ShapeDtypeStruct((seq, LANE), F32), jax.ShapeDtypeStruct((1, LANE), F32)],
        scratch_shapes=[pltpu.VMEM((LANE, tb), F32)],
        compiler_params=_params(("arbitrary",)),
    )(dck, proj, fb)


def _att_scores(q, k, cq, ck_row, diagonal):
    s = _dot(q, k, NT) * (HEAD_DIM ** -0.5) + (cq - ck_row)
    if diagonal:
        keep = lax.broadcasted_iota(jnp.int32, s.shape, 0) >= lax.broadcasted_iota(jnp.int32, s.shape, 1)
        s = jnp.where(keep, s, -jnp.inf)
    return s


def _causal_pairs(nq, key_major):
    pairs = ([(i, j) for j in range(nq) for i in range(j, nq)] if key_major
             else [(i, j) for i in range(nq) for j in range(i + 1)])
    return jnp.asarray([p[0] for p in pairs], jnp.int32), jnp.asarray([p[1] for p in pairs], jnp.int32)


def _attn_specs(tq):
    qs = lambda off: pl.BlockSpec((tq, HEAD_DIM), lambda h, t, it, jt: (it[t], off + h))
    kv = lambda off: pl.BlockSpec((tq, HEAD_DIM), lambda h, t, it, jt: (jt[t], off + h))
    c_spec = pl.BlockSpec((tq, LANE), lambda h, t, it, jt: (it[t], 0))
    ct_spec = pl.BlockSpec((1, 1, tq), lambda h, t, it, jt: (FORGET_LANE + h, 0, jt[t]))
    lse_spec = pl.BlockSpec((1, tq, LANE), lambda h, t, it, jt: (h, it[t], 0))
    return qs, kv, c_spec, ct_spec, lse_spec


def _ride_along(comm, refs, n_in, n_out, n_scratch, first, last):
    n_c = len(comm[1]) if comm else 0
    ins, c_in = refs[:n_in], refs[n_in:n_in + n_c]
    outs, c_out = refs[n_in + n_c:n_in + n_c + n_out], refs[n_in + n_c + n_out:n_in + 2 * n_c + n_out]
    scratch = refs[n_in + 2 * n_c + n_out:n_in + 2 * n_c + n_out + n_scratch]
    sems = refs[n_in + 2 * n_c + n_out + n_scratch:]
    if not comm:
        return ins, outs, scratch, lambda: None
    start, finish = _comm_ops(comm[0], c_in, c_out, sems)
    pl.when(first)(start)
    return ins, outs, scratch, lambda: pl.when(last)(finish)


def _attn_fwd(name, proj, c, ct3, comm=None):
    seq = proj.shape[0]
    tq = min(ATT_T, seq)
    nq = seq // tq
    qb, zb = OFF_QKVC // HEAD_DIM, OFF_ZC // HEAD_DIM
    i_tab, j_tab = _causal_pairs(nq, False)
    n_pairs = i_tab.shape[0]
    c_arrs = list(comm[1]) if comm else []
    c_shapes, c_sems = _comm_plan(comm[0], c_arrs) if comm else ([], [])

    def body(it, jt, *refs):
        h, t = pl.program_id(0), pl.program_id(1)
        i, j = it[t], jt[t]
        ins, outs, scratch, finish = _ride_along(comm, refs, 6, 3, 3, (h == 0) & (t == 0),
                                                 (h == HEADS - 1) & (t == n_pairs - 1))
        q_ref, k_ref, v_ref, z_ref, c_ref, ct_ref = ins
        y_ref, o_ref, lse_ref = outs
        m_s, l_s, acc_s = scratch

        @pl.when(j == 0)
        def _():
            m_s[...] = jnp.full_like(m_s, -jnp.inf)
            l_s[...] = jnp.zeros_like(l_s)
            acc_s[...] = jnp.zeros_like(acc_s)

        def step(diagonal):
            cq = _colsel(c_ref[...], FORGET_LANE + h)
            s = _att_scores(q_ref[...], k_ref[...], cq, ct_ref[0], diagonal)
            m_old = m_s[...]
            m_new = jnp.maximum(m_old, jnp.max(s, axis=1, keepdims=True))
            p = jnp.exp(s - m_new)
            alpha = jnp.exp(m_old - m_new)
            l_s[...] = alpha * l_s[...] + jnp.sum(p, axis=1, keepdims=True)
            p_hi = p.astype(MXU_DTYPE).astype(F32)
            acc_s[...] = alpha * acc_s[...] + _dot(p_hi, v_ref[...]) + _dot(p - p_hi, v_ref[...])
            m_s[...] = m_new

        @pl.when(j < i)
        def _():
            step(False)

        @pl.when(j == i)
        def _():
            step(True)
            o = acc_s[...] / l_s[...]
            o_ref[...] = o
            y_ref[...] = (o * _silu(z_ref[...])).astype(y_ref.dtype)
            lse_ref[0] = jnp.broadcast_to(m_s[...] + jnp.log(l_s[...]), (tq, LANE))

        finish()

    qs, kv, c_spec, ct_spec, lse_spec = _attn_specs(tq)
    res = pl.pallas_call(
        body, name=name,
        grid_spec=pltpu.PrefetchScalarGridSpec(
            num_scalar_prefetch=2, grid=(HEADS, n_pairs),
            in_specs=[qs(qb), kv(qb + HEADS), kv(qb + 2 * HEADS), qs(zb), c_spec, ct_spec] + [HBM_SPEC] * len(c_arrs),
            out_specs=[qs(0), qs(0), lse_spec] + [HBM_SPEC] * len(c_arrs),
            scratch_shapes=[pltpu.VMEM((tq, 1), F32), pltpu.VMEM((tq, 1), F32), pltpu.VMEM((tq, HEAD_DIM), F32)]
            + c_sems),
        out_shape=[jax.ShapeDtypeStruct((seq, BRANCH), BF16), jax.ShapeDtypeStruct((seq, BRANCH), F32),
                   jax.ShapeDtypeStruct((HEADS, seq, LANE), F32)] + c_shapes,
        compiler_params=_params(("arbitrary", "arbitrary")),
    )(i_tab, j_tab, proj, proj, proj, proj, c, ct3, *c_arrs)
    return res[0], res[1], res[2], list(res[3:])


def _attn_dq(name, dy, o, lse, proj, c, ct3):
    seq = proj.shape[0]
    tq = min(ATT_T, seq)
    nq = seq // tq
    qb, zb = OFF_QKVC // HEAD_DIM, OFF_ZC // HEAD_DIM
    i_tab, j_tab = _causal_pairs(nq, False)

    def body(it, jt, q_ref, k_ref, v_ref, z_ref, c_ref, ct_ref, dy_ref, o_ref, lse_ref, dq_ref, dz_ref, do_s, dl_s, acc_s):
        h, t = pl.program_id(0), pl.program_id(1)
        i, j = it[t], jt[t]

        @pl.when(j == 0)
        def _():
            z = z_ref[...]
            sg = _sigmoid(z)
            dyv = dy_ref[...].astype(F32)
            do = dyv * z * sg
            do_s[...] = do
            dl_s[...] = jnp.sum(do.astype(MXU_DTYPE).astype(F32) * o_ref[...], axis=1, keepdims=True)
            dz_ref[...] = (dyv * o_ref[...] * sg * (1.0 + z * (1.0 - sg))).astype(dz_ref.dtype)
            acc_s[...] = jnp.zeros_like(acc_s)

        def step(diagonal):
            cq = _colsel(c_ref[...], FORGET_LANE + h)
            s = _att_scores(q_ref[...], k_ref[...], cq, ct_ref[0], diagonal)
            p = jnp.exp(s - jnp.max(lse_ref[0], axis=1, keepdims=True))
            dp = _dot(do_s[...], v_ref[...], NT)
            ds = p * (dp - dl_s[...])
            acc_s[...] += _dot(ds, k_ref[...])

        @pl.when(j < i)
        def _():
            step(False)

        @pl.when(j == i)
        def _():
            step(True)
            dq_ref[...] = (acc_s[...] * (HEAD_DIM ** -0.5)).astype(dq_ref.dtype)

    qs, kv, c_spec, ct_spec, lse_spec = _attn_specs(tq)
    return pl.pallas_call(
        body, name=name,
        grid_spec=pltpu.PrefetchScalarGridSpec(
            num_scalar_prefetch=2, grid=(HEADS, i_tab.shape[0]),
            in_specs=[qs(qb), kv(qb + HEADS), kv(qb + 2 * HEADS), qs(zb), c_spec, ct_spec, qs(0), qs(0), lse_spec],
            out_specs=[qs(0), qs(0)],
            scratch_shapes=[pltpu.VMEM((tq, HEAD_DIM), F32), pltpu.VMEM((tq, 1), F32), pltpu.VMEM((tq, HEAD_DIM), F32)]),
        out_shape=[jax.ShapeDtypeStruct((seq, BRANCH), BF16)] * 2,
        compiler_params=_params(("parallel", "arbitrary")),
    )(i_tab, j_tab, proj, proj, proj, proj, c, ct3, dy, o, lse)


def _attn_dkv(name, dy, o, lse, proj, c, ct3, comm=None):
    seq = proj.shape[0]
    tq = min(ATT_T, seq)
    nq = seq // tq
    qb, zb = OFF_QKVC // HEAD_DIM, OFF_ZC // HEAD_DIM
    i_tab, j_tab = _causal_pairs(nq, True)
    n_pairs = i_tab.shape[0]
    c_arrs = list(comm[1]) if comm else []
    c_shapes, c_sems = _comm_plan(comm[0], c_arrs) if comm else ([], [])

    def body(it, jt, *refs):
        h, t = pl.program_id(0), pl.program_id(1)
        i, j = it[t], jt[t]
        ins, outs, scratch, finish = _ride_along(comm, refs, 9, 3, 3, (h == 0) & (t == 0),
                                                 (h == HEADS - 1) & (t == n_pairs - 1))
        q_ref, k_ref, v_ref, z_ref, c_ref, ct_ref, dy_ref, o_ref, lse_ref = ins
        dk_ref, dv_ref, dc_ref = outs
        dk_s, dv_s, dc_s = scratch

        @pl.when(i == j)
        def _():
            dk_s[...] = jnp.zeros_like(dk_s)
            dv_s[...] = jnp.zeros_like(dv_s)
            dc_s[...] = jnp.zeros_like(dc_s)

        def step(diagonal):
            z = z_ref[...]
            do = dy_ref[...].astype(F32) * _silu(z)
            delta = jnp.sum(do.astype(MXU_DTYPE).astype(F32) * o_ref[...], axis=1, keepdims=True)
            cq = _colsel(c_ref[...], FORGET_LANE + h)
            s = _att_scores(q_ref[...], k_ref[...], cq, ct_ref[0], diagonal)
            p = jnp.exp(s - jnp.max(lse_ref[0], axis=1, keepdims=True))
            dv_s[...] += _dot(p, do, TN)
            ds = p * (_dot(do, v_ref[...], NT) - delta)
            dk_s[...] += _dot(ds, q_ref[...], TN)
            dc_s[...] -= jnp.sum(ds, axis=0, keepdims=True)

        @pl.when(i == j)
        def _():
            step(True)

        @pl.when(i > j)
        def _():
            step(False)

        @pl.when(i == nq - 1)
        def _():
            dk_ref[...] = (dk_s[...] * (HEAD_DIM ** -0.5)).astype(dk_ref.dtype)
            dv_ref[...] = dv_s[...].astype(dv_ref.dtype)
            dc_ref[0] = dc_s[...]

        finish()

    qs, kv, c_spec, ct_spec, lse_spec = _attn_specs(tq)
    res = pl.pallas_call(
        body, name=name,
        grid_spec=pltpu.PrefetchScalarGridSpec(
            num_scalar_prefetch=2, grid=(HEADS, n_pairs),
            in_specs=[qs(qb), kv(qb + HEADS), kv(qb + 2 * HEADS), qs(zb), c_spec, ct_spec, qs(0), qs(0), lse_spec]
            + [HBM_SPEC] * len(c_arrs),
            out_specs=[kv(0), kv(0), pl.BlockSpec((1, 1, tq), lambda h, t, it, jt: (h, 0, jt[t]))]
            + [HBM_SPEC] * len(c_arrs),
            scratch_shapes=[pltpu.VMEM((tq, HEAD_DIM), F32), pltpu.VMEM((tq, HEAD_DIM), F32), pltpu.VMEM((1, tq), F32)]
            + c_sems),
        out_shape=[jax.ShapeDtypeStruct((seq, BRANCH), BF16)] * 2 + [jax.ShapeDtypeStruct((HEADS, 1, seq), F32)]
        + c_shapes,
        compiler_params=_params(("arbitrary", "arbitrary")),
    )(i_tab, j_tab, proj, proj, proj, proj, c, ct3, dy, o, lse, *c_arrs)
    return res[0], res[1], res[2], list(res[3:])


def _loss_head(name, y, target):
    seq, d = y.shape
    tile = min(256, seq)

    def body(y_ref, t_ref, dy_ref, l_ref):
        err = y_ref[...] - t_ref[...]
        dy_ref[...] = err / d

        @pl.when(pl.program_id(0) == 0)
        def _():
            l_ref[...] = jnp.zeros_like(l_ref)

        l_ref[...] += 0.5 * jnp.sum(jnp.mean(err * err, axis=-1, keepdims=True), axis=0, keepdims=True)

    return pl.pallas_call(
        body, name=name, grid=(seq // tile,),
        in_specs=[pl.BlockSpec((tile, d), lambda i: (i, 0))] * 2,
        out_specs=[pl.BlockSpec((tile, d), lambda i: (i, 0)), pl.BlockSpec((8, LANE), lambda i: (0, 0))],
        out_shape=[jax.ShapeDtypeStruct((seq, d), F32), jax.ShapeDtypeStruct((8, LANE), F32)],
        compiler_params=_params(("arbitrary",)),
    )(y, target)


def _adamw(name, w, m, v, g_parts, tile):
    rows, cols = w.shape
    tile = min(tile, rows)
    assert rows % tile == 0, (name, w.shape)
    n_g = len(g_parts)

    def body(*refs):
        w_ref, m_ref, v_ref = refs[:3]
        g_refs = refs[3:3 + n_g]
        g_ref, d_ref, nm_ref, nv_ref = refs[3 + n_g:]
        g = None
        for r in g_refs:
            parts = [r[...]] if len(r.shape) == 2 else [r[i] for i in range(r.shape[0])]
            for p in parts:
                g = p.astype(F32) if g is None else g + p.astype(F32)
        m_new = ADAM_B1 * m_ref[...] + (1.0 - ADAM_B1) * g
        v_new = ADAM_B2 * v_ref[...] + (1.0 - ADAM_B2) * (g * g)
        m_hat = m_new / (1.0 - ADAM_B1 ** ADAM_STEP)
        v_hat = v_new / (1.0 - ADAM_B2 ** ADAM_STEP)
        g_ref[...] = g
        d_ref[...] = -ADAM_LR * (m_hat / (jnp.sqrt(v_hat) + ADAM_EPS) + ADAM_WD * w_ref[...])
        nm_ref[...] = m_new
        nv_ref[...] = v_new

    blk = pl.BlockSpec((tile, cols), lambda i: (i, 0))
    g_specs = [blk if p.ndim == 2 else pl.BlockSpec((p.shape[0], tile, cols), lambda i: (0, i, 0)) for p in g_parts]
    return pl.pallas_call(
        body, name=name, grid=(rows // tile,), in_specs=[blk] * 3 + g_specs, out_specs=[blk] * 4,
        out_shape=[jax.ShapeDtypeStruct((rows, cols), F32)] * 4,
        compiler_params=_params(("parallel",)),
    )(w, m, v, *g_parts)


_ORIG_SEGMENTS = (
    ("qkv_a", 0, 3072), ("z_a", 3072, 1024), ("beta", 4096, 8), ("alpha", 4104, 8), ("glu", 4112, 2048),
    ("z_b", 6160, 1024), ("qkv_c", 7184, 3072), ("z_c", 10256, 1024), ("forget", 11280, 8), ("gate", 11288, 6144))
_PAD_ORDER = ("gate", "qkv_a", "z_a", "glu", "z_b", "qkv_c", "z_c", "beta", "alpha", "forget")


def _pad_cols(w):
    seg = {n: w[..., s:s + k] for n, s, k in _ORIG_SEGMENTS}
    fill = jnp.zeros(w.shape[:-1] + (N_PAD - N_IN,), w.dtype)
    return jnp.concatenate([seg[n] for n in _PAD_ORDER] + [fill], axis=-1)


def _unpad_cols(g):
    off, seg = 0, {}
    widths = {n: k for n, _, k in _ORIG_SEGMENTS}
    for n in _PAD_ORDER:
        seg[n] = g[..., off:off + widths[n]]
        off += widths[n]
    return jnp.concatenate([seg[n] for n, _, _ in _ORIG_SEGMENTS], axis=-1)


def _lane_row(vals, lane0):
    return jnp.pad(vals.astype(F32), (lane0, LANE - HEADS - lane0))[None]


def _layer_fwd(x, p, comm=None):
    seq = x.shape[0]
    h = _rowwise("rms_pre", _rms_pre_fn, [(x, D_MODEL, 0)], [p["pre_w"]], [(D_MODEL, BF16)], 256)[0]
    proj = _matmul("mm_in", h, p["w_in"], "nn", F32, 512, 1280, 2048)
    ca = _conv_fwd("conv_a", proj, OFF_QKVA, 3 * BRANCH, p["w4"], jnp.zeros((1, 3 * BRANCH), F32), SHORT_CONV)
    y_a, states = _gdr_fwd("gdr_fwd", ca, proj, p["alog"], p["dtb"], p["onw"])
    u2 = _conv_fwd("conv_b", proj, OFF_VAL, BRANCH, p["w31"], p["cb"], CONF_CONV, gate_off=OFF_GLUG)
    y_b = _rowwise("ln_gate", _ln_gate_fn, [(u2, BRANCH, 0), (proj, BRANCH, OFF_ZB // BRANCH)],
                   [p["ln_w"], p["ln_b"]], [(BRANCH, BF16)], 256)[0]
    c, ct = _fox_gate_fwd("fox_gate", proj, p["fb"])
    ct3 = ct.reshape(LANE, 1, seq)
    y_c, o_c, lse, got = _attn_fwd("attn_fwd", proj, c, ct3, comm)
    ys = (y_a, y_b, y_c)
    br = [_matmul("mm_br", ys[n], p["wbr"][n], "nn", F32, 512, 2048, 1024) for n in range(N_BRANCH)]
    merged = _rowwise("merge", _merge_fn, [(proj, D_MODEL, n) for n in range(N_BRANCH)] + [(b, D_MODEL, 0) for b in br],
                      [], [(D_MODEL, BF16)], 256)[0]
    out = _matmul("mm_out", merged, p["wout"], "nn", F32, 512, 2048, 2048)
    x_new = _rowwise("rms_post", _rms_post_fn, [(out, D_MODEL, 0), (x, D_MODEL, 0)], [p["post_w"]],
                     [(D_MODEL, F32)], 256)[0]
    saved = dict(x=x, ht=h.T, proj=proj, ca=ca, states=states, u2=u2, c=c, ct3=ct3, o_c=o_c, lse=lse, ys=ys, br=br,
                 merged=merged, out=out)
    return x_new, saved, got


def _layer_bwd(dxn, p, sv, comm=None):
    x, proj = sv["x"], sv["proj"]
    seq = x.shape[0]
    g = {}
    d_out, g["post_w"] = _rowwise_bwd("rms_post_bwd", _rms_only_fn, [(sv["out"], D_MODEL, 0)], [p["post_w"]], [dxn],
                                      [BF16], 256)
    d_merged = _matmul("mm_out_dx", d_out, p["wout"], "nt", F32, 512, 2048, 2048)
    g["wout"] = _matmul("mm_out_dw", sv["merged"], d_out, "tn", F32, 1024, 1024, 512)
    rows = [(proj, D_MODEL, n) for n in range(N_BRANCH)] + [(b, D_MODEL, 0) for b in sv["br"]]
    d_gl0, d_gl1, d_gl2, d_b0, d_b1, d_b2 = _rowwise_bwd("merge_bwd", _merge_fn, rows, [], [d_merged], [BF16] * 6, 128)
    d_br = (d_b0, d_b1, d_b2)
    dys = [_matmul("mm_br_dx", d_br[n], p["wbr"][n], "nt", BF16, 512, 1024, 2048) for n in range(N_BRANCH)]
    g["wbr"] = jnp.stack([_matmul("mm_br_dw", sv["ys"][n], d_br[n], "tn", F32, 1024, 1024, 512)
                          for n in range(N_BRANCH)])
    dq, dzc = _attn_dq("attn_dq", dys[2], sv["o_c"], sv["lse"], proj, sv["c"], sv["ct3"])
    dk, dv, dck, got = _attn_dkv("attn_dkv", dys[2], sv["o_c"], sv["lse"], proj, sv["c"], sv["ct3"], comm)
    dsm_c, g["fb"] = _fox_gate_bwd("fox_gate_bwd", dck.reshape(HEADS, seq), proj, p["fb"])
    du2, dzb, g["ln_w"], g["ln_b"] = _rowwise_bwd(
        "ln_gate_bwd", _ln_gate_fn, [(sv["u2"], BRANCH, 0), (proj, BRANCH, OFF_ZB // BRANCH)], [p["ln_w"], p["ln_b"]],
        [dys[1]], [F32, BF16], 256)
    dval, dgate, g["w31"], g["cb"] = _conv_bwd("conv_b_bwd", du2, proj, OFF_VAL, BRANCH, p["w31"], CONF_CONV,
                                               gate_off=OFF_GLUG)
    dca, dza, dsm_a, g["alog"], g["dtb"], g["onw"] = _gdr_bwd("gdr_bwd", dys[0], sv["states"], sv["ca"], proj,
                                                              p["alog"], p["dtb"], p["onw"])
    dqkva, g["w4"], _ = _conv_bwd("conv_a_bwd", dca, proj, OFF_QKVA, 3 * BRANCH, p["w4"], SHORT_CONV)
    d_small = jnp.pad((dsm_a + dsm_c).astype(BF16), ((0, 0), (0, N_PAD - OFF_SMALL - LANE)))
    d_proj = jnp.concatenate([d_gl0, d_gl1, d_gl2, dqkva, dza, dval, dgate, dzb, dq, dk, dv, dzc, d_small], axis=1)
    dh = _matmul("mm_in_dx", d_proj, p["w_in"], "nt", F32, 512, 2048, 1280)
    g["w_in"] = _matmul("mm_in_dw", sv["ht"], d_proj, "nn", F32, 2048, 640, 512)
    dx, g["pre_w"] = _rowwise_bwd("rms_pre_bwd", _rms_pre_res_fn, [(x, D_MODEL, 0)], [p["pre_w"]], [dh, dxn], [F32], 256)
    return dx, g, got


N_CHIPS = 4
N_DEV = 8
HBM_SPEC = pl.BlockSpec(memory_space=pltpu.HBM)


def _mesh_pos():
    return lax.axis_index("x"), lax.axis_index("y"), lax.axis_index("c")


def _other_chips(x, y):
    return [(1 - x, y), (x, 1 - y), (1 - x, 1 - y)]


def _comm_plan(kind, arrs):
    n = len(arrs)
    if kind == "allgather":
        return ([jax.ShapeDtypeStruct((N_CHIPS,) + a.shape, a.dtype) for a in arrs],
                [pltpu.SemaphoreType.DMA((3 * n,))] * 4 + [pltpu.SemaphoreType.DMA((n,))])
    return ([jax.ShapeDtypeStruct(a.shape, a.dtype) for a in arrs],
            [pltpu.SemaphoreType.DMA((3 * n,))] * 2 + [pltpu.SemaphoreType.DMA((n,))])


def _comm_ops(kind, ins, outs, sems):
    return (_allgather_ops if kind == "allgather" else _reduce_scatter_ops)(ins, outs, sems)


def _allgather_ops(ins, outs, sems):
    send_sems, recv_sems, pass_send, pass_recv, local_sems = sems
    n = len(ins)
    x, y, c = _mesh_pos()
    me = 2 * x + y
    chips = [2 * px + py for px, py in _other_chips(x, y)]

    def part(a, core):
        half = ins[a].shape[0] // 2
        return pl.ds(half * core, half)

    def ici(a, j, slot):
        px, py = _other_chips(x, y)[j]
        return pltpu.make_async_remote_copy(
            src_ref=ins[a].at[part(a, c)], dst_ref=outs[a].at[slot, part(a, c)], send_sem=send_sems.at[3 * a + j],
            recv_sem=recv_sems.at[3 * a + j], device_id=(px, py, c), device_id_type=MESH)

    def d2d(a, j, core):
        blk = outs[a].at[chips[j], part(a, core)]
        return pltpu.make_async_remote_copy(
            src_ref=blk, dst_ref=blk, send_sem=pass_send.at[3 * a + j], recv_sem=pass_recv.at[3 * a + j],
            device_id=(x, y, 1 - c), device_id_type=MESH)

    local = [pltpu.make_async_copy(ins[a], outs[a].at[me], local_sems.at[a]) for a in range(n)]

    def start():
        for a in range(n):
            local[a].start()
            for j in range(3):
                ici(a, j, me).start()

    def finish():
        for a in range(n):
            for j in range(3):
                ici(a, j, chips[j]).wait_recv()
                d2d(a, j, c).start()
        for a in range(n):
            for j in range(3):
                d2d(a, j, 1 - c).wait_recv()
        for a in range(n):
            for j in range(3):
                ici(a, j, me).wait_send()
                d2d(a, j, c).wait_send()
            local[a].wait()

    return start, finish


def _reduce_scatter_ops(ins, outs, sems):
    send_sems, recv_sems, local_sems = sems
    n = len(ins)
    x, y, c = _mesh_pos()
    me = 2 * x + y

    def remote(a, j, slot):
        px, py = _other_chips(x, y)[j]
        return pltpu.make_async_remote_copy(
            src_ref=ins[a].at[2 * px + py], dst_ref=outs[a].at[slot], send_sem=send_sems.at[3 * a + j],
            recv_sem=recv_sems.at[3 * a + j], device_id=(px, py, c), device_id_type=MESH)

    local = [pltpu.make_async_copy(ins[a].at[me], outs[a].at[me], local_sems.at[a]) for a in range(n)]

    def start():
        for a in range(n):
            local[a].start()
            for j in range(3):
                remote(a, j, me).start()

    def finish():
        for a in range(n):
            for j, (px, py) in enumerate(_other_chips(x, y)):
                remote(a, j, 2 * px + py).wait_recv()
        for a in range(n):
            for j in range(3):
                remote(a, j, me).wait_send()
            local[a].wait()

    return start, finish


def _exchange_chips(name, kind, arrs):
    n = len(arrs)
    shapes, sems = _comm_plan(kind, arrs)

    def body(*refs):
        start, finish = _comm_ops(kind, refs[:n], refs[n:2 * n], refs[2 * n:])
        start()
        finish()

    return pl.pallas_call(body, name=name, in_specs=[HBM_SPEC] * n, out_specs=[HBM_SPEC] * n, out_shape=shapes,
                          scratch_shapes=sems)(*arrs)


def _halves_to_sibling(name, arrs):
    n = len(arrs)

    def body(*refs):
        ins, outs = refs[:n], refs[n:2 * n]
        send_sems, recv_sems = refs[2 * n:]
        x, y, c = _mesh_pos()
        copies = []
        for a in range(n):
            rows = ins[a].shape[1] // 2
            for s in range(N_CHIPS):
                copies.append(pltpu.make_async_remote_copy(
                    src_ref=ins[a].at[s, pl.ds((1 - c) * rows, rows)], dst_ref=outs[a].at[s],
                    send_sem=send_sems.at[N_CHIPS * a + s], recv_sem=recv_sems.at[N_CHIPS * a + s],
                    device_id=(x, y, 1 - c), device_id_type=MESH))
        for cp in copies:
            cp.start()
        for cp in copies:
            cp.wait_recv()
        for cp in copies:
            cp.wait_send()

    return pl.pallas_call(
        body, name=name, in_specs=[HBM_SPEC] * n, out_specs=[HBM_SPEC] * n,
        out_shape=[jax.ShapeDtypeStruct((a.shape[0], a.shape[1] // 2, a.shape[2]), a.dtype) for a in arrs],
        scratch_shapes=[pltpu.SemaphoreType.DMA((N_CHIPS * n,)), pltpu.SemaphoreType.DMA((N_CHIPS * n,))],
    )(*arrs)


def _add_own_half(name, full, other, core, tile):
    n, rows, cols = other.shape
    tile = min(tile, rows)
    assert rows % tile == 0, (name, other.shape)
    nb = rows // tile

    def body(c_ref, f_ref, o_ref, out_ref):
        out_ref[...] = f_ref[...] + o_ref[...]

    return pl.pallas_call(
        body, name=name,
        grid_spec=pltpu.PrefetchScalarGridSpec(
            num_scalar_prefetch=1, grid=(n, nb),
            in_specs=[pl.BlockSpec((1, tile, cols), lambda s, i, c_ref: (s, c_ref[0] * nb + i, 0)),
                      pl.BlockSpec((1, tile, cols), lambda s, i, c_ref: (s, i, 0))],
            out_specs=pl.BlockSpec((1, tile, cols), lambda s, i, c_ref: (s, i, 0))),
        out_shape=jax.ShapeDtypeStruct(other.shape, F32),
        compiler_params=_params(("parallel", "parallel")),
    )(core, full, other)


def _join_cores(name, arrs):
    n = len(arrs)

    def body(*refs):
        ins, outs = refs[:n], refs[n:2 * n]
        send_sems, recv_sems, local_sems = refs[2 * n:]
        x, y, c = _mesh_pos()
        local = [pltpu.make_async_copy(ins[a], outs[a].at[c], local_sems.at[a]) for a in range(n)]
        remote = [pltpu.make_async_remote_copy(
            src_ref=ins[a], dst_ref=outs[a].at[c], send_sem=send_sems.at[a], recv_sem=recv_sems.at[a],
            device_id=(x, y, 1 - c), device_id_type=MESH) for a in range(n)]
        for a in range(n):
            local[a].start()
            remote[a].start()
        for a in range(n):
            pltpu.make_async_remote_copy(
                src_ref=ins[a], dst_ref=outs[a].at[1 - c], send_sem=send_sems.at[a], recv_sem=recv_sems.at[a],
                device_id=(x, y, 1 - c), device_id_type=MESH).wait_recv()
        for a in range(n):
            remote[a].wait_send()
            local[a].wait()

    return pl.pallas_call(
        body, name=name, in_specs=[HBM_SPEC] * n, out_specs=[HBM_SPEC] * n,
        out_shape=[jax.ShapeDtypeStruct((2,) + a.shape, a.dtype) for a in arrs],
        scratch_shapes=[pltpu.SemaphoreType.DMA((n,)), pltpu.SemaphoreType.DMA((n,)), pltpu.SemaphoreType.DMA((n,))],
    )(*arrs)


def _allgather_devices(name, buf):
    def body(in_ref, out_ref, send_sems, recv_sems):
        x, y, c = _mesh_pos()
        me = 4 * x + 2 * y + c
        out_ref[me] = in_ref[...]

        def remote(k, slot):
            peer = (x ^ (k >> 2), y ^ ((k >> 1) & 1), c ^ (k & 1))
            return pltpu.make_async_remote_copy(
                src_ref=in_ref, dst_ref=out_ref.at[slot], send_sem=send_sems.at[k - 1], recv_sem=recv_sems.at[k - 1],
                device_id=peer, device_id_type=MESH)

        for k in range(1, N_DEV):
            remote(k, me).start()
        for k in range(1, N_DEV):
            remote(k, me ^ k).wait_recv()
        for k in range(1, N_DEV):
            remote(k, me).wait_send()

    vmem = pl.BlockSpec(memory_space=pltpu.VMEM)
    return pl.pallas_call(
        body, name=name, in_specs=[vmem], out_specs=vmem,
        out_shape=jax.ShapeDtypeStruct((N_DEV,) + buf.shape, buf.dtype),
        scratch_shapes=[pltpu.SemaphoreType.DMA((N_DEV - 1,)), pltpu.SemaphoreType.DMA((N_DEV - 1,))],
    )(buf)


def _sum_slots(name, a, tile):
    n, rows, cols = a.shape
    tile = min(tile, rows)
    assert rows % tile == 0, (name, a.shape)

    def body(a_ref, o_ref):
        acc = a_ref[0].astype(F32)
        for i in range(1, n):
            acc = acc + a_ref[i].astype(F32)
        o_ref[...] = acc

    return pl.pallas_call(
        body, name=name, grid=(rows // tile,),
        in_specs=[pl.BlockSpec((n, tile, cols), lambda i: (0, i, 0))],
        out_specs=pl.BlockSpec((tile, cols), lambda i: (i, 0)),
        out_shape=jax.ShapeDtypeStruct((rows, cols), F32),
        compiler_params=_params(("parallel",)),
    )(a)


_SMALL = (
    ("pre_norm_w", (DEPTH, D_MODEL)), ("post_norm_w", (DEPTH, D_MODEL)), ("a_log", (DEPTH, HEADS)),
    ("dt_bias", (DEPTH, HEADS)), ("o_norm_w", (DEPTH, HEAD_DIM)), ("conv_b", (DEPTH, BRANCH)), ("ln_w", (DEPTH, BRANCH)),
    ("ln_b", (DEPTH, BRANCH)), ("f_bias", (DEPTH, HEADS)), ("conv_qkv_w", (DEPTH, SHORT_CONV, 3 * BRANCH)),
    ("conv_w", (DEPTH, CONF_CONV, BRANCH)), ("loss", (1,)))
_SHARDED_SMALL = {"conv_qkv_w": 3 * BRANCH // N_CHIPS, "conv_w": BRANCH // N_CHIPS}
_WEIGHTS = ("pre_norm_w", "post_norm_w", "w_in", "conv_qkv_w", "a_log", "dt_bias", "o_norm_w", "conv_w", "conv_b",
            "ln_w", "ln_b", "f_bias", "w_branch", "w_out")


def _pack(parts):
    flat = jnp.concatenate([p.reshape(-1).astype(F32) for p in parts])
    rows = -(-flat.shape[0] // (8 * LANE)) * 8
    return jnp.pad(flat, (0, rows * LANE - flat.shape[0])).reshape(rows, LANE)


def _unpack(buf, shapes):
    flat, out, off = buf.reshape(-1), [], 0
    for shp in shapes:
        size = 1
        for s in shp:
            size *= s
        out.append(flat[off:off + size].reshape(shp))
        off += size
    return out


def kernel(x, pre_norm_w, post_norm_w, w_in, conv_qkv_w, a_log, dt_bias, o_norm_w, conv_w, conv_b, ln_w, ln_b, f_bias, w_branch, w_out, loss_target, m_pre_norm_w, m_post_norm_w, m_w_in, m_conv_qkv_w, m_a_log, m_dt_bias, m_o_norm_w, m_conv_w, m_conv_b, m_ln_w, m_ln_b, m_f_bias, m_w_branch, m_w_out, v_pre_norm_w, v_post_norm_w, v_w_in, v_conv_qkv_w, v_a_log, v_dt_bias, v_o_norm_w, v_conv_w, v_conv_b, v_ln_w, v_ln_b, v_f_bias, v_w_branch, v_w_out):
    weights = dict(pre_norm_w=pre_norm_w, post_norm_w=post_norm_w, w_in=w_in, conv_qkv_w=conv_qkv_w, a_log=a_log,
                   dt_bias=dt_bias, o_norm_w=o_norm_w, conv_w=conv_w, conv_b=conv_b, ln_w=ln_w, ln_b=ln_b, f_bias=f_bias,
                   w_branch=w_branch, w_out=w_out)
    mom1 = dict(pre_norm_w=m_pre_norm_w, post_norm_w=m_post_norm_w, w_in=m_w_in, conv_qkv_w=m_conv_qkv_w, a_log=m_a_log,
                dt_bias=m_dt_bias, o_norm_w=m_o_norm_w, conv_w=m_conv_w, conv_b=m_conv_b, ln_w=m_ln_w, ln_b=m_ln_b,
                f_bias=m_f_bias, w_branch=m_w_branch, w_out=m_w_out)
    mom2 = dict(pre_norm_w=v_pre_norm_w, post_norm_w=v_post_norm_w, w_in=v_w_in, conv_qkv_w=v_conv_qkv_w, a_log=v_a_log,
                dt_bias=v_dt_bias, o_norm_w=v_o_norm_w, conv_w=v_conv_w, conv_b=v_conv_b, ln_w=v_ln_w, ln_b=v_ln_b,
                f_bias=v_f_bias, w_branch=v_w_branch, w_out=v_w_out)
    chip = 2 * lax.axis_index("x") + lax.axis_index("y")
    core = lax.axis_index("c").astype(jnp.int32).reshape(1)

    w_in_b, w_out_b = w_in.astype(BF16), w_out.astype(BF16)
    w_br_b = w_branch.astype(BF16).reshape(DEPTH, N_BRANCH * BRANCH, D_MODEL // N_CHIPS)
    shards = lambda l: [w_in_b[l], w_br_b[l], w_out_b[l]]
    cat_last = lambda g: jnp.concatenate([g[s] for s in range(N_CHIPS)], axis=-1)
    first = _exchange_chips("allgather_weights", "allgather", shards(0) + [conv_qkv_w, conv_w])
    c4_full = jnp.pad(cat_last(first[3]), ((0, 0), (0, 8 - SHORT_CONV), (0, 0)))
    c31_full = jnp.pad(cat_last(first[4]), ((0, 0), (0, 32 - CONF_CONV), (0, 0)))

    def layer_params(l, got):
        return dict(
            pre_w=pre_norm_w[l][None], post_w=post_norm_w[l][None], w_in=_pad_cols(cat_last(got[0])), w4=c4_full[l],
            alog=_lane_row(a_log[l], ALPHA_LANE), dtb=_lane_row(dt_bias[l], ALPHA_LANE), onw=o_norm_w[l][None],
            w31=c31_full[l], cb=conv_b[l][None], ln_w=ln_w[l][None], ln_b=ln_b[l][None],
            fb=_lane_row(f_bias[l], FORGET_LANE), wbr=cat_last(got[1]).reshape(N_BRANCH, BRANCH, D_MODEL),
            wout=got[2].reshape(D_MODEL, D_MODEL))

    act = x[0]
    got = first[:3]
    layers, saved = [], []
    for l in range(DEPTH):
        layers.append(layer_params(l, got))
        act, sv, got = _layer_fwd(act, layers[l], ("allgather", shards(l + 1)) if l + 1 < DEPTH else None)
        saved.append(sv)
    d_act, loss_blk = _loss_head("loss_head", act, loss_target[0])

    parts_in, parts_br, parts_out, small_g = [], [], [], []

    def finish_reduce(recv):
        mine = [_sum_slots("sum_chips", r, t) for r, t in zip(recv, (64, 512, 128))]
        joined = _join_cores("join_cores", mine)
        for lst, j in zip((parts_in, parts_br, parts_out), joined):
            lst.append(j.reshape(2 * j.shape[1], j.shape[2]))

    pending = None
    for l in reversed(range(DEPTH)):
        d_act, g, recv = _layer_bwd(d_act, layers[l], saved[l], ("reduce_scatter", pending) if pending else None)
        if pending:
            finish_reduce(recv)
        g_in = _unpad_cols(g["w_in"]).reshape(D_MODEL, N_CHIPS, N_IN // N_CHIPS).transpose(1, 0, 2)
        g_br = g["wbr"].reshape(N_BRANCH * BRANCH, N_CHIPS, D_MODEL // N_CHIPS).transpose(1, 0, 2)
        g_out = g["wout"].reshape(N_CHIPS, D_MODEL // N_CHIPS, D_MODEL)
        parts = [g_in, g_br, g_out]
        other = _halves_to_sibling("halves_to_sibling", parts)
        pending = [_add_own_half("add_own_half", f, o, core, t) for f, o, t in zip(parts, other, (64, 512, 128))]
        small_g.append(g)
    finish_reduce(_exchange_chips("reduce_scatter_grads", "reduce_scatter", pending))
    small_g = small_g[::-1]
    parts_in, parts_br, parts_out = parts_in[::-1], parts_br[::-1], parts_out[::-1]

    stack = lambda key, f=lambda a: a: jnp.stack([f(g[key]) for g in small_g])
    small = dict(
        pre_norm_w=stack("pre_w", lambda a: a[0]), post_norm_w=stack("post_w", lambda a: a[0]),
        a_log=stack("alog", lambda a: a[0, ALPHA_LANE:ALPHA_LANE + HEADS]),
        dt_bias=stack("dtb", lambda a: a[0, ALPHA_LANE:ALPHA_LANE + HEADS]), o_norm_w=stack("onw", lambda a: a[0]),
        conv_b=stack("cb", lambda a: a[0]), ln_w=stack("ln_w", lambda a: a[0]), ln_b=stack("ln_b", lambda a: a[0]),
        f_bias=stack("fb", lambda a: a[0, FORGET_LANE:FORGET_LANE + HEADS]),
        conv_qkv_w=stack("w4", lambda a: a[:SHORT_CONV]), conv_w=stack("w31", lambda a: a[:CONF_CONV]),
        loss=loss_blk[0, 0:1])
    gathered = _allgather_devices("allgather_small", _pack([small[n] for n, _ in _SMALL]))
    total = _unpack(_sum_slots("sum_devices", gathered, gathered.shape[1]), [s for _, s in _SMALL])
    total = {n: t for (n, _), t in zip(_SMALL, total)}
    loss = total.pop("loss")[0]
    for n, width in _SHARDED_SMALL.items():
        total[n] = lax.dynamic_slice_in_dim(total[n], chip * width, width, axis=2)

    names = list(total)
    packed = [_pack([d[n] for n in names]) for d in (weights, mom1, mom2)]
    res = _adamw("adamw_small", packed[0], packed[1], packed[2], [_pack([total[n] for n in names])], packed[0].shape[0])
    shapes = [weights[n].shape for n in names]
    grads, delta, new_m, new_v = [dict(zip(names, _unpack(r, shapes))) for r in res]
    big = (("w_in", parts_in, (DEPTH * D_MODEL, N_IN // N_CHIPS), 64),
           ("w_branch", parts_br, (DEPTH * N_BRANCH * BRANCH, D_MODEL // N_CHIPS), 512),
           ("w_out", parts_out, (DEPTH * D_MODEL // N_CHIPS, D_MODEL), 128))
    for n, parts, shape2, tile in big:
        res = _adamw("adamw_" + n, weights[n].reshape(shape2), mom1[n].reshape(shape2), mom2[n].reshape(shape2),
                     [jnp.concatenate(parts, axis=0)], tile)
        grads[n], delta[n], new_m[n], new_v[n] = [r.reshape(weights[n].shape) for r in res]

    outs = [loss, d_act[None]]
    for d in (grads, delta, new_m, new_v):
        outs += [d[n] for n in _WEIGHTS]
    return tuple(outs)
```

```python
import functools

import jax
import jax.numpy as jnp
from jax import lax
from jax.experimental import pallas as pl
from jax.experimental.pallas import tpu as pltpu

F32 = jnp.float32
BF16 = jnp.bfloat16
MXU_DTYPE = jnp.bfloat16

D_MODEL = 2048
DEPTH = 4
BRANCH = 1024
HEAD_DIM = 128
HEADS = 8
CHUNK = 64
SHORT_CONV = 4
CONF_CONV = 31
N_BRANCH = 3
NORM_EPS = 1e-6
N_IN = 17432

OFF_GATE = 0
OFF_QKVA = 6144
OFF_ZA = 9216
OFF_VAL = 10240
OFF_GLUG = 11264
OFF_ZB = 12288
OFF_QKVC = 13312
OFF_ZC = 16384
OFF_SMALL = 17408
N_PAD = 17920
LANE = 128
BETA_LANE, ALPHA_LANE, FORGET_LANE = 0, 8, 16

ADAM_LR = 0.001
ADAM_B1 = 0.9
ADAM_B2 = 0.999
ADAM_EPS = 1e-08
ADAM_WD = 0.01
ADAM_STEP = 10

VMEM_LIMIT = 56 * 1024 * 1024

NN = (((1,), (0,)), ((), ()))
NT = (((1,), (1,)), ((), ()))
TN = (((0,), (0,)), ((), ()))
MESH = pl.DeviceIdType.MESH


def _params(sem=None):
    return pltpu.CompilerParams(dimension_semantics=sem, vmem_limit_bytes=VMEM_LIMIT)


def _dot(a, b, dims=NN):
    return lax.dot_general(a.astype(MXU_DTYPE), b.astype(MXU_DTYPE), dims, preferred_element_type=F32)


def _dot_hi(a, b, dims=NN):
    return lax.dot_general(a, b, dims, precision=lax.Precision.HIGHEST, preferred_element_type=F32)


def _sigmoid(x):
    return 1.0 / (1.0 + jnp.exp(-x))


def _silu(x):
    return x * _sigmoid(x)


def _softplus(x):
    return jnp.maximum(x, 0.0) + jnp.log(1.0 + jnp.exp(-jnp.abs(x)))


def _colsel(m, j):
    lane = lax.broadcasted_iota(jnp.int32, m.shape, 1)
    return jnp.sum(jnp.where(lane == j, m, 0.0), axis=1, keepdims=True)


def _rowsel(m, j):
    sub = lax.broadcasted_iota(jnp.int32, m.shape, 0)
    return jnp.sum(jnp.where(sub == j, m, 0.0), axis=0, keepdims=True)


def _row_specs(rows, tile):
    return [pl.BlockSpec((tile, w), functools.partial(lambda i, cb: (i, cb), cb=cb)) for (_, w, cb) in rows]


def _rowwise(name, fn, rows, params, outs, tile):
    seq = rows[0][0].shape[0]
    tile = min(tile, seq)
    n_in = len(rows) + len(params)

    def body(*refs):
        res = fn(*[r[...] for r in refs[:n_in]])
        for o_ref, r in zip(refs[n_in:], res):
            o_ref[...] = r.astype(o_ref.dtype)

    return pl.pallas_call(
        body, name=name, grid=(seq // tile,),
        in_specs=_row_specs(rows, tile) + [pl.BlockSpec(p.shape, lambda i: (0, 0)) for p in params],
        out_specs=[pl.BlockSpec((tile, w), lambda i: (i, 0)) for (w, _) in outs],
        out_shape=[jax.ShapeDtypeStruct((seq, w), dt) for (w, dt) in outs],
        compiler_params=_params(("parallel",)),
    )(*[r[0] for r in rows], *params)


def _rowwise_bwd(name, fn, rows, params, cts, row_grads, tile):
    seq = rows[0][0].shape[0]
    tile = min(tile, seq)
    nr, npar, nct = len(rows), len(params), len(cts)
    n_in = nr + npar

    def body(*refs):
        vals = [r[...] for r in refs[:n_in]]
        res, vjp = jax.vjp(fn, *vals)
        grads = vjp(tuple(c[...].astype(r.dtype) for c, r in zip(refs[n_in:n_in + nct], res)))
        outs = refs[n_in + nct:]
        k = 0
        for idx, dt in enumerate(row_grads):
            if dt is not None:
                outs[k][...] = grads[idx].astype(dt)
                k += 1
        first = pl.program_id(0) == 0
        for j in range(npar):
            g = grads[nr + j].astype(F32)
            o_ref = outs[k + j]

            @pl.when(first)
            def _(o_ref=o_ref, g=g):
                o_ref[...] = g

            @pl.when(jnp.logical_not(first))
            def _(o_ref=o_ref, g=g):
                o_ref[...] += g

    want = [(rows[i][1], dt) for i, dt in enumerate(row_grads) if dt is not None]
    return pl.pallas_call(
        body, name=name, grid=(seq // tile,),
        in_specs=(_row_specs(rows, tile) + [pl.BlockSpec(p.shape, lambda i: (0, 0)) for p in params]
                  + [pl.BlockSpec((tile, c.shape[1]), lambda i: (i, 0)) for c in cts]),
        out_specs=([pl.BlockSpec((tile, w), lambda i: (i, 0)) for (w, _) in want]
                   + [pl.BlockSpec(p.shape, lambda i: (0, 0)) for p in params]),
        out_shape=([jax.ShapeDtypeStruct((seq, w), dt) for (w, dt) in want]
                   + [jax.ShapeDtypeStruct(p.shape, F32) for p in params]),
        compiler_params=_params(("arbitrary",)),
    )(*[r[0] for r in rows], *params, *cts)


def _rms(x, w):
    x = x.astype(F32)
    return x * lax.rsqrt(jnp.mean(x * x, axis=-1, keepdims=True) + NORM_EPS) * w


def _rms_pre_fn(x, w):
    return (_rms(x, w),)


def _rms_pre_res_fn(x, w):
    return (_rms(x, w), x)


def _rms_post_fn(out, x, w):
    return (x + _rms(out, w),)


def _rms_only_fn(out, w):
    return (_rms(out, w),)


def _ln_gate_fn(u, z, w, b):
    u = u.astype(F32)
    uc = u - jnp.mean(u, axis=-1, keepdims=True)
    y = uc * lax.rsqrt(jnp.mean(uc * uc, axis=-1, keepdims=True) + NORM_EPS) * w + b
    return (_silu(y) * _silu(z.astype(F32)),)


def _merge_fn(g0, g1, g2, b0, b1, b2):
    return (_sigmoid(g0) * b0 + _sigmoid(g1) * b1 + _sigmoid(g2) * b2,)


def _matmul(name, a, b, mode, out_dtype, tm, tn, tk):
    if mode == "nn":
        (m, kc), n = a.shape, b.shape[1]
    elif mode == "nt":
        (m, kc), n = a.shape, b.shape[0]
    else:
        (kc, m), n = a.shape, b.shape[1]
    tm, tn, tk = min(tm, m), min(tn, n), min(tk, kc)
    nk = kc // tk
    assert m % tm == 0 and n % tn == 0 and kc % tk == 0, (name, a.shape, b.shape)
    dims = {"nn": NN, "nt": NT, "tn": TN}[mode]
    a_spec = (pl.BlockSpec((tk, tm), lambda j, i, k: (k, i)) if mode == "tn"
              else pl.BlockSpec((tm, tk), lambda j, i, k: (i, k)))
    b_spec = (pl.BlockSpec((tn, tk), lambda j, i, k: (j, k)) if mode == "nt"
              else pl.BlockSpec((tk, tn), lambda j, i, k: (k, j)))
    use_acc = nk > 1 and out_dtype != F32

    def body(a_ref, b_ref, o_ref, *acc):
        p = _dot(a_ref[...], b_ref[...], dims)
        if nk == 1:
            o_ref[...] = p.astype(out_dtype)
            return
        k = pl.program_id(2)
        dst = acc[0] if use_acc else o_ref

        @pl.when(k == 0)
        def _():
            dst[...] = p

        @pl.when(k > 0)
        def _():
            dst[...] += p

        if use_acc:
            @pl.when(k == nk - 1)
            def _():
                o_ref[...] = dst[...].astype(out_dtype)

    return pl.pallas_call(
        body, name=name, grid=(n // tn, m // tm, nk),
        in_specs=[a_spec, b_spec],
        out_specs=pl.BlockSpec((tm, tn), lambda j, i, k: (i, j)),
        out_shape=jax.ShapeDtypeStruct((m, n), out_dtype),
        scratch_shapes=[pltpu.VMEM((tm, tn), F32)] if use_acc else [],
        compiler_params=_params(("parallel", "parallel", "arbitrary")),
    )(a, b)


HALO = 32
CONV_TC = 256
CONV_T = 1024


def _conv_fwd(name, x, x_off, ch, w, b, k_width, gate_off=None):
    seq = x.shape[0]
    t_blk = min(CONV_T, seq)
    tc = CONV_TC
    hb = t_blk // HALO
    xcb = x_off // tc
    has_gate = gate_off is not None

    def body(*refs):
        if has_gate:
            xm_ref, xh_ref, gm_ref, gh_ref, w_ref, b_ref, y_ref, win = refs
        else:
            xm_ref, xh_ref, w_ref, b_ref, y_ref, win = refs
        t = pl.program_id(1)
        xm, xh = xm_ref[...], xh_ref[...]
        if has_gate:
            xm = xm * _sigmoid(gm_ref[...])
            xh = xh * _sigmoid(gh_ref[...])
        win[0:HALO, :] = jnp.where(t == 0, 0.0, xh)
        win[HALO:HALO + t_blk, :] = xm
        acc = jnp.broadcast_to(b_ref[...], (t_blk, tc))
        for k in range(k_width):
            acc = acc + w_ref[k:k + 1, :] * win[HALO - (k_width - 1) + k:HALO - (k_width - 1) + k + t_blk, :]
        y_ref[...] = acc

    main = lambda off: pl.BlockSpec((t_blk, tc), lambda c, t: (t, off + c))
    halo = lambda off: pl.BlockSpec((HALO, tc), lambda c, t: (jnp.maximum(t * hb - 1, 0), off + c))
    ins, specs = [x, x], [main(xcb), halo(xcb)]
    if has_gate:
        gcb = gate_off // tc
        ins += [x, x]
        specs += [main(gcb), halo(gcb)]
    ins += [w, b]
    specs += [pl.BlockSpec((w.shape[0], tc), lambda c, t: (0, c)), pl.BlockSpec((1, tc), lambda c, t: (0, c))]
    return pl.pallas_call(
        body, name=name, grid=(ch // tc, seq // t_blk), in_specs=specs,
        out_specs=pl.BlockSpec((t_blk, tc), lambda c, t: (t, c)),
        out_shape=jax.ShapeDtypeStruct((seq, ch), F32),
        scratch_shapes=[pltpu.VMEM((HALO + t_blk, tc), F32)],
        compiler_params=_params(("parallel", "arbitrary")),
    )(*ins)


def _conv_bwd(name, dy, x, x_off, ch, w, k_width, gate_off=None):
    seq = x.shape[0]
    t_blk = min(CONV_T, seq)
    tc = CONV_TC
    hb = t_blk // HALO
    nt = seq // t_blk
    xcb = x_off // tc
    has_gate = gate_off is not None
    kp = w.shape[0]

    def body(*refs):
        if has_gate:
            dm_ref, dh_ref, xm_ref, xh_ref, gm_ref, gh_ref, w_ref, dv_ref, dg_ref, dw_ref, db_ref, winx, wind = refs
        else:
            dm_ref, dh_ref, xm_ref, xh_ref, w_ref, dx_ref, dw_ref, db_ref, winx, wind = refs
        t = pl.program_id(1)
        xm, xh = xm_ref[...], xh_ref[...]
        if has_gate:
            sg = _sigmoid(gm_ref[...])
            um = xm * sg
            uh = xh * _sigmoid(gh_ref[...])
        else:
            um, uh = xm, xh
        winx[0:HALO, :] = jnp.where(t == nt - 1, 0.0, uh)
        winx[HALO:HALO + t_blk, :] = um
        dm = dm_ref[...]
        wind[0:t_blk, :] = dm
        wind[t_blk:t_blk + HALO, :] = jnp.where(t == 0, 0.0, dh_ref[...])
        du = jnp.zeros((t_blk, tc), F32)
        for k in range(k_width):
            du = du + w_ref[k:k + 1, :] * wind[k_width - 1 - k:k_width - 1 - k + t_blk, :]
        if has_gate:
            dv_ref[...] = (du * sg).astype(dv_ref.dtype)
            dg_ref[...] = (du * xm * sg * (1.0 - sg)).astype(dg_ref.dtype)
        else:
            dx_ref[...] = du.astype(dx_ref.dtype)

        @pl.when(t == 0)
        def _():
            dw_ref[...] = jnp.zeros_like(dw_ref)
            db_ref[...] = jnp.zeros_like(db_ref)

        for k in range(k_width):
            s0 = HALO - (k_width - 1) + k
            dw_ref[k:k + 1, :] += jnp.sum(dm * winx[s0:s0 + t_blk, :], axis=0, keepdims=True)
        db_ref[...] += jnp.sum(dm, axis=0, keepdims=True)

    rt = lambda t: nt - 1 - t
    main = lambda off: pl.BlockSpec((t_blk, tc), lambda c, t: (rt(t), off + c))
    past = lambda off: pl.BlockSpec((HALO, tc), lambda c, t: (jnp.maximum(rt(t) * hb - 1, 0), off + c))
    future = pl.BlockSpec((HALO, tc), lambda c, t: (jnp.minimum((rt(t) + 1) * hb, seq // HALO - 1), c))
    ins, specs = [dy, dy, x, x], [main(0), future, main(xcb), past(xcb)]
    if has_gate:
        gcb = gate_off // tc
        ins += [x, x]
        specs += [main(gcb), past(gcb)]
    ins += [w]
    specs += [pl.BlockSpec((kp, tc), lambda c, t: (0, c))]
    blk = pl.BlockSpec((t_blk, tc), lambda c, t: (rt(t), c))
    n_dx = 2 if has_gate else 1
    return pl.pallas_call(
        body, name=name, grid=(ch // tc, nt), in_specs=specs,
        out_specs=[blk] * n_dx + [pl.BlockSpec((kp, tc), lambda c, t: (0, c)), pl.BlockSpec((1, tc), lambda c, t: (0, c))],
        out_shape=[jax.ShapeDtypeStruct((seq, ch), BF16)] * n_dx + [jax.ShapeDtypeStruct((kp, ch), F32),
                                                                    jax.ShapeDtypeStruct((1, ch), F32)],
        scratch_shapes=[pltpu.VMEM((HALO + t_blk, tc), F32), pltpu.VMEM((HALO + t_blk, tc), F32)],
        compiler_params=_params(("parallel", "arbitrary")),
    )(*ins)


@jax.custom_vjp
def _inv_unit_lower(lows):
    n = lows[0].shape[0]
    eye = (lax.broadcasted_iota(jnp.int32, (n, n), 0) == lax.broadcasted_iota(jnp.int32, (n, n), 1)).astype(F32)
    accs = [eye - low for low in lows]
    pws = list(lows)
    steps = 1
    while steps * 2 < n:
        pws = [_dot_hi(pw, pw) for pw in pws]
        accs = [acc + _dot_hi(acc, pw) for acc, pw in zip(accs, pws)]
        steps *= 2
    return tuple(accs)


def _inv_fwd(lows):
    ts = _inv_unit_lower(lows)
    return ts, ts


def _inv_bwd(ts, dts):
    left = [_dot_hi(t, dt, TN) for t, dt in zip(ts, dts)]
    return (tuple(-_dot_hi(l, t, NT) for l, t in zip(left, ts)),)


_inv_unit_lower.defvjp(_inv_fwd, _inv_bwd)


def _gdr_chunk(cqkv, z, sm, alog, dtb, onw, state):
    c = cqkv.shape[0]
    hs = range(HEADS)
    ri = lax.broadcasted_iota(jnp.int32, (c, c), 0)
    ci = lax.broadcasted_iota(jnp.int32, (c, c), 1)
    incl, strict = ri >= ci, ri > ci
    beta_all = _sigmoid(sm)
    la_all = -jnp.exp(alog) * _softplus(sm + dtb)
    g_cols = _dot_hi(incl.astype(F32), la_all)
    g_rows = _dot_hi(la_all, (ri <= ci).astype(F32), TN)
    g_end = jnp.sum(la_all, axis=0, keepdims=True)
    act = _silu(cqkv)
    sl = lambda base, h: slice(base + h * HEAD_DIM, base + (h + 1) * HEAD_DIM)
    q = [act[:, sl(0, h)] for h in hs]
    k = [act[:, sl(BRANCH, h)] for h in hs]
    v = [act[:, sl(2 * BRANCH, h)] for h in hs]
    q = [x * lax.rsqrt(jnp.sum(x * x, axis=-1, keepdims=True) + NORM_EPS) * (HEAD_DIM ** -0.5) for x in q]
    k = [x * lax.rsqrt(jnp.sum(x * x, axis=-1, keepdims=True) + NORM_EPS) for x in k]
    beta = [_colsel(beta_all, BETA_LANE + h) for h in hs]
    g = [_colsel(g_cols, ALPHA_LANE + h) for h in hs]
    g_row = [_rowsel(g_rows, ALPHA_LANE + h) for h in hs]
    g_last = [_colsel(g_end, ALPHA_LANE + h) for h in hs]
    decay = [jnp.where(incl, jnp.exp(jnp.where(incl, g[h] - g_row[h], 0.0)), 0.0) for h in hs]
    kk = [_dot(k[h], k[h], NT) for h in hs]
    qk = [_dot(q[h], k[h], NT) * decay[h] for h in hs]
    t_inv = _inv_unit_lower(tuple(jnp.where(strict, beta[h] * kk[h] * decay[h], 0.0) for h in hs))
    eg = [jnp.exp(g[h]) for h in hs]
    u0 = [_dot(t_inv[h], v[h] * beta[h]) for h in hs]
    w_cum = [_dot(t_inv[h], k[h] * (beta[h] * eg[h])) for h in hs]
    s_in = [state[h] for h in hs]
    u = [u0[h] - _dot(w_cum[h], s_in[h]) for h in hs]
    o = [_dot(q[h] * eg[h], s_in[h]) + _dot(qk[h], u[h]) for h in hs]
    s_out = [s_in[h] * jnp.exp(g_last[h]) + _dot(k[h] * jnp.exp(g_last[h] - g[h]), u[h], TN) for h in hs]
    o = [x * lax.rsqrt(jnp.mean(x * x, axis=-1, keepdims=True) + NORM_EPS) * onw for x in o]
    y = [o[h] * _silu(z[:, sl(0, h)]) for h in hs]
    return jnp.concatenate(y, axis=1), jnp.concatenate([s[None] for s in s_out], axis=0)


def _gdr_specs(nc, order):
    return [
        pl.BlockSpec((CHUNK, 3 * BRANCH), lambda n: (order(n), 0)),
        pl.BlockSpec((CHUNK, BRANCH), lambda n: (order(n), OFF_ZA // BRANCH)),
        pl.BlockSpec((CHUNK, LANE), lambda n: (order(n), OFF_SMALL // LANE)),
        pl.BlockSpec((1, LANE), lambda n: (0, 0)),
        pl.BlockSpec((1, LANE), lambda n: (0, 0)),
        pl.BlockSpec((1, LANE), lambda n: (0, 0)),
    ]


def _gdr_fwd(name, cqkv, proj, alog, dtb, onw):
    seq = cqkv.shape[0]
    nc = seq // CHUNK

    def body(c_ref, z_ref, sm_ref, al_ref, dt_ref, on_ref, y_ref, st_ref, state):
        @pl.when(pl.program_id(0) == 0)
        def _():
            state[...] = jnp.zeros_like(state)

        s_in = state[...]
        st_ref[0] = s_in
        y, s_out = _gdr_chunk(c_ref[...], z_ref[...], sm_ref[...], al_ref[...], dt_ref[...], on_ref[...], s_in)
        y_ref[...] = y.astype(y_ref.dtype)
        state[...] = s_out

    return pl.pallas_call(
        body, name=name, grid=(nc,), in_specs=_gdr_specs(nc, lambda n: n),
        out_specs=[pl.BlockSpec((CHUNK, BRANCH), lambda n: (n, 0)),
                   pl.BlockSpec((1, HEADS, HEAD_DIM, HEAD_DIM), lambda n: (n, 0, 0, 0))],
        out_shape=[jax.ShapeDtypeStruct((seq, BRANCH), BF16),
                   jax.ShapeDtypeStruct((nc, HEADS, HEAD_DIM, HEAD_DIM), F32)],
        scratch_shapes=[pltpu.VMEM((HEADS, HEAD_DIM, HEAD_DIM), F32)],
        compiler_params=_params(("arbitrary",)),
    )(cqkv, proj, proj, alog, dtb, onw)


def _gdr_bwd(name, dy, states, cqkv, proj, alog, dtb, onw):
    seq = cqkv.shape[0]
    nc = seq // CHUNK
    rev = lambda n: nc - 1 - n

    def body(c_ref, z_ref, sm_ref, al_ref, dt_ref, on_ref, dy_ref, st_ref,
             dc_ref, dz_ref, dsm_ref, dal_ref, ddt_ref, don_ref, dstate):
        first = pl.program_id(0) == 0

        @pl.when(first)
        def _():
            dstate[...] = jnp.zeros_like(dstate)
            dal_ref[...] = jnp.zeros_like(dal_ref)
            ddt_ref[...] = jnp.zeros_like(ddt_ref)
            don_ref[...] = jnp.zeros_like(don_ref)

        _, vjp = jax.vjp(_gdr_chunk, c_ref[...], z_ref[...], sm_ref[...], al_ref[...], dt_ref[...], on_ref[...],
                         st_ref[0])
        dc, dz, dsm, dal, ddt, don, ds = vjp((dy_ref[...].astype(F32), dstate[...]))
        dc_ref[...] = dc
        dz_ref[...] = dz.astype(dz_ref.dtype)
        dsm_ref[...] = dsm
        dal_ref[...] += dal
        ddt_ref[...] += ddt
        don_ref[...] += don
        dstate[...] = ds

    small = pl.BlockSpec((1, LANE), lambda n: (0, 0))
    return pl.pallas_call(
        body, name=name, grid=(nc,),
        in_specs=_gdr_specs(nc, rev) + [pl.BlockSpec((CHUNK, BRANCH), lambda n: (rev(n), 0)),
                                        pl.BlockSpec((1, HEADS, HEAD_DIM, HEAD_DIM), lambda n: (rev(n), 0, 0, 0))],
        out_specs=[pl.BlockSpec((CHUNK, 3 * BRANCH), lambda n: (rev(n), 0)),
                   pl.BlockSpec((CHUNK, BRANCH), lambda n: (rev(n), 0)),
                   pl.BlockSpec((CHUNK, LANE), lambda n: (rev(n), 0)), small, small, small],
        out_shape=[jax.ShapeDtypeStruct((seq, 3 * BRANCH), F32), jax.ShapeDtypeStruct((seq, BRANCH), BF16),
                   jax.ShapeDtypeStruct((seq, LANE), F32)] + [jax.ShapeDtypeStruct((1, LANE), F32)] * 3,
        scratch_shapes=[pltpu.VMEM((HEADS, HEAD_DIM, HEAD_DIM), F32)],
        compiler_params=_params(("arbitrary",)),
    )(cqkv, proj, proj, alog, dtb, onw, dy, states)


GATE_T = 512
ATT_T = 1024


def _fox_gate_fwd(name, proj, fb):
    seq = proj.shape[0]
    tb = min(GATE_T, seq)

    def body(sm_ref, fb_ref, c_ref, ct_ref):
        tri = (lax.broadcasted_iota(jnp.int32, (tb, tb), 0) >= lax.broadcasted_iota(jnp.int32, (tb, tb), 1)).astype(F32)
        carry = jnp.zeros((1, LANE), F32)
        for i in range(seq // tb):
            lf = -_softplus(-(sm_ref[i * tb:(i + 1) * tb, :] + fb_ref[...]))
            cb = _dot_hi(tri, lf) + carry
            c_ref[i * tb:(i + 1) * tb, :] = cb
            ct_ref[:, i * tb:(i + 1) * tb] = cb.T
            carry = carry + jnp.sum(lf, axis=0, keepdims=True)

    return pl.pallas_call(
        body, name=name, grid=(1,),
        in_specs=[pl.BlockSpec((seq, LANE), lambda i: (0, OFF_SMALL // LANE)), pl.BlockSpec((1, LANE), lambda i: (0, 0))],
        out_specs=[pl.BlockSpec((seq, LANE), lambda i: (0, 0)), pl.BlockSpec((LANE, seq), lambda i: (0, 0))],
        out_shape=[jax.ShapeDtypeStruct((seq, LANE), F32), jax.ShapeDtypeStruct((LANE, seq), F32)],
        compiler_params=_params(("arbitrary",)),
    )(proj, fb)


def _fox_gate_bwd(name, dck, proj, fb):
    seq = proj.shape[0]
    tb = min(GATE_T, seq)
    nb = seq // tb

    def body(d_ref, sm_ref, fb_ref, o_ref, dfb_ref, pad):
        tri = (lax.broadcasted_iota(jnp.int32, (tb, tb), 0) >= lax.broadcasted_iota(jnp.int32, (tb, tb), 1)).astype(F32)
        pad[...] = jnp.zeros_like(pad)
        carry = jnp.zeros((HEADS, 1), F32)
        dfb = jnp.zeros((1, LANE), F32)
        for i in reversed(range(nb)):
            blk = d_ref[:, i * tb:(i + 1) * tb]
            pad[FORGET_LANE:FORGET_LANE + HEADS, :] = _dot_hi(blk, tri) + carry
            carry = carry + jnp.sum(blk, axis=1, keepdims=True)
            x = sm_ref[i * tb:(i + 1) * tb, :] + fb_ref[...]
            dsm = pad[...].T * _sigmoid(-x)
            o_ref[i * tb:(i + 1) * tb, :] = dsm
            dfb = dfb + jnp.sum(dsm, axis=0, keepdims=True)
        dfb_ref[...] = dfb

    return pl.pallas_call(
        body, name=name, grid=(1,),
        in_specs=[pl.BlockSpec((HEADS, seq), lambda i: (0, 0)), pl.BlockSpec((seq, LANE), lambda i: (0, OFF_SMALL // LANE)),
                  pl.BlockSpec((1, LANE), lambda i: (0, 0))],
        out_specs=[pl.BlockSpec((seq, LANE), lambda i: (0, 0)), pl.BlockSpec((1, LANE), lambda i: (0, 0))],
        out_shape=[jax.ShapeDtypeStruct((seq, LANE), F32), jax.ShapeDtypeStruct((1, LANE), F32)],
        scratch_shapes=[pltpu.VMEM((LANE, tb), F32)],
        compiler_params=_params(("arbitrary",)),
    )(dck, proj, fb)


def _att_scores(q, k, cq, ck_row, diagonal):
    s = _dot(q, k, NT) * (HEAD_DIM ** -0.5) + (cq - ck_row)
    if diagonal:
        keep = lax.broadcasted_iota(jnp.int32, s.shape, 0) >= lax.broadcasted_iota(jnp.int32, s.shape, 1)
        s = jnp.where(keep, s, -jnp.inf)
    return s


def _causal_pairs(nq, key_major):
    pairs = ([(i, j) for j in range(nq) for i in range(j, nq)] if key_major
             else [(i, j) for i in range(nq) for j in range(i + 1)])
    return jnp.asarray([p[0] for p in pairs], jnp.int32), jnp.asarray([p[1] for p in pairs], jnp.int32)


def _attn_specs(tq):
    qs = lambda off: pl.BlockSpec((tq, HEAD_DIM), lambda h, t, it, jt: (it[t], off + h))
    kv = lambda off: pl.BlockSpec((tq, HEAD_DIM), lambda h, t, it, jt: (jt[t], off + h))
    c_spec = pl.BlockSpec((tq, LANE), lambda h, t, it, jt: (it[t], 0))
    ct_spec = pl.BlockSpec((1, 1, tq), lambda h, t, it, jt: (FORGET_LANE + h, 0, jt[t]))
    lse_spec = pl.BlockSpec((1, tq, LANE), lambda h, t, it, jt: (h, it[t], 0))
    return qs, kv, c_spec, ct_spec, lse_spec


def _ride_along(comm, refs, n_in, n_out, n_scratch, first, last):
    n_c = len(comm[1]) if comm else 0
    ins, c_in = refs[:n_in], refs[n_in:n_in + n_c]
    outs, c_out = refs[n_in + n_c:n_in + n_c + n_out], refs[n_in + n_c + n_out:n_in + 2 * n_c + n_out]
    scratch = refs[n_in + 2 * n_c + n_out:n_in + 2 * n_c + n_out + n_scratch]
    sems = refs[n_in + 2 * n_c + n_out + n_scratch:]
    if not comm:
        return ins, outs, scratch, lambda: None
    start, finish = _comm_ops(comm[0], c_in, c_out, sems)
    pl.when(first)(start)
    return ins, outs, scratch, lambda: pl.when(last)(finish)


def _attn_fwd(name, proj, c, ct3, comm=None):
    seq = proj.shape[0]
    tq = min(ATT_T, seq)
    nq = seq // tq
    qb, zb = OFF_QKVC // HEAD_DIM, OFF_ZC // HEAD_DIM
    i_tab, j_tab = _causal_pairs(nq, False)
    n_pairs = i_tab.shape[0]
    c_arrs = list(comm[1]) if comm else []
    c_shapes, c_sems = _comm_plan(comm[0], c_arrs) if comm else ([], [])

    def body(it, jt, *refs):
        h, t = pl.program_id(0), pl.program_id(1)
        i, j = it[t], jt[t]
        ins, outs, scratch, finish = _ride_along(comm, refs, 6, 3, 3, (h == 0) & (t == 0),
                                                 (h == HEADS - 1) & (t == n_pairs - 1))
        q_ref, k_ref, v_ref, z_ref, c_ref, ct_ref = ins
        y_ref, o_ref, lse_ref = outs
        m_s, l_s, acc_s = scratch

        @pl.when(j == 0)
        def _():
            m_s[...] = jnp.full_like(m_s, -jnp.inf)
            l_s[...] = jnp.zeros_like(l_s)
            acc_s[...] = jnp.zeros_like(acc_s)

        def step(diagonal):
            cq = _colsel(c_ref[...], FORGET_LANE + h)
            s = _att_scores(q_ref[...], k_ref[...], cq, ct_ref[0], diagonal)
            m_old = m_s[...]
            m_new = jnp.maximum(m_old, jnp.max(s, axis=1, keepdims=True))
            p = jnp.exp(s - m_new)
            alpha = jnp.exp(m_old - m_new)
            l_s[...] = alpha * l_s[...] + jnp.sum(p, axis=1, keepdims=True)
            p_hi = p.astype(MXU_DTYPE).astype(F32)
            acc_s[...] = alpha * acc_s[...] + _dot(p_hi, v_ref[...]) + _dot(p - p_hi, v_ref[...])
            m_s[...] = m_new

        @pl.when(j < i)
        def _():
            step(False)

        @pl.when(j == i)
        def _():
            step(True)
            o = acc_s[...] / l_s[...]
            o_ref[...] = o
            y_ref[...] = (o * _silu(z_ref[...])).astype(y_ref.dtype)
            lse_ref[0] = jnp.broadcast_to(m_s[...] + jnp.log(l_s[...]), (tq, LANE))

        finish()

    qs, kv, c_spec, ct_spec, lse_spec = _attn_specs(tq)
    res = pl.pallas_call(
        body, name=name,
        grid_spec=pltpu.PrefetchScalarGridSpec(
            num_scalar_prefetch=2, grid=(HEADS, n_pairs),
            in_specs=[qs(qb), kv(qb + HEADS), kv(qb + 2 * HEADS), qs(zb), c_spec, ct_spec] + [HBM_SPEC] * len(c_arrs),
            out_specs=[qs(0), qs(0), lse_spec] + [HBM_SPEC] * len(c_arrs),
            scratch_shapes=[pltpu.VMEM((tq, 1), F32), pltpu.VMEM((tq, 1), F32), pltpu.VMEM((tq, HEAD_DIM), F32)]
            + c_sems),
        out_shape=[jax.ShapeDtypeStruct((seq, BRANCH), BF16), jax.ShapeDtypeStruct((seq, BRANCH), F32),
                   jax.ShapeDtypeStruct((HEADS, seq, LANE), F32)] + c_shapes,
        compiler_params=_params(("arbitrary", "arbitrary")),
    )(i_tab, j_tab, proj, proj, proj, proj, c, ct3, *c_arrs)
    return res[0], res[1], res[2], list(res[3:])


def _attn_dq(name, dy, o, lse, proj, c, ct3):
    seq = proj.shape[0]
    tq = min(ATT_T, seq)
    nq = seq // tq
    qb, zb = OFF_QKVC // HEAD_DIM, OFF_ZC // HEAD_DIM
    i_tab, j_tab = _causal_pairs(nq, False)

    def body(it, jt, q_ref, k_ref, v_ref, z_ref, c_ref, ct_ref, dy_ref, o_ref, lse_ref, dq_ref, dz_ref, do_s, dl_s, acc_s):
        h, t = pl.program_id(0), pl.program_id(1)
        i, j = it[t], jt[t]

        @pl.when(j == 0)
        def _():
            z = z_ref[...]
            sg = _sigmoid(z)
            dyv = dy_ref[...].astype(F32)
            do = dyv * z * sg
            do_s[...] = do
            dl_s[...] = jnp.sum(do.astype(MXU_DTYPE).astype(F32) * o_ref[...], axis=1, keepdims=True)
            dz_ref[...] = (dyv * o_ref[...] * sg * (1.0 + z * (1.0 - sg))).astype(dz_ref.dtype)
            acc_s[...] = jnp.zeros_like(acc_s)

        def step(diagonal):
            cq = _colsel(c_ref[...], FORGET_LANE + h)
            s = _att_scores(q_ref[...], k_ref[...], cq, ct_ref[0], diagonal)
            p = jnp.exp(s - jnp.max(lse_ref[0], axis=1, keepdims=True))
            dp = _dot(do_s[...], v_ref[...], NT)
            ds = p * (dp - dl_s[...])
            acc_s[...] += _dot(ds, k_ref[...])

        @pl.when(j < i)
        def _():
            step(False)

        @pl.when(j == i)
        def _():
            step(True)
            dq_ref[...] = (acc_s[...] * (HEAD_DIM ** -0.5)).astype(dq_ref.dtype)

    qs, kv, c_spec, ct_spec, lse_spec = _attn_specs(tq)
    return pl.pallas_call(
        body, name=name,
        grid_spec=pltpu.PrefetchScalarGridSpec(
            num_scalar_prefetch=2, grid=(HEADS, i_tab.shape[0]),
            in_specs=[qs(qb), kv(qb + HEADS), kv(qb + 2 * HEADS), qs(zb), c_spec, ct_spec, qs(0), qs(0), lse_spec],
            out_specs=[qs(0), qs(0)],
            scratch_shapes=[pltpu.VMEM((tq, HEAD_DIM), F32), pltpu.VMEM((tq, 1), F32), pltpu.VMEM((tq, HEAD_DIM), F32)]),
        out_shape=[jax.ShapeDtypeStruct((seq, BRANCH), BF16)] * 2,
        compiler_params=_params(("parallel", "arbitrary")),
    )(i_tab, j_tab, proj, proj, proj, proj, c, ct3, dy, o, lse)


def _attn_dkv(name, dy, o, lse, proj, c, ct3, comm=None):
    seq = proj.shape[0]
    tq = min(ATT_T, seq)
    nq = seq // tq
    qb, zb = OFF_QKVC // HEAD_DIM, OFF_ZC // HEAD_DIM
    i_tab, j_tab = _causal_pairs(nq, True)
    n_pairs = i_tab.shape[0]
    c_arrs = list(comm[1]) if comm else []
    c_shapes, c_sems = _comm_plan(comm[0], c_arrs) if comm else ([], [])

    def body(it, jt, *refs):
        h, t = pl.program_id(0), pl.program_id(1)
        i, j = it[t], jt[t]
        ins, outs, scratch, finish = _ride_along(comm, refs, 9, 3, 3, (h == 0) & (t == 0),
                                                 (h == HEADS - 1) & (t == n_pairs - 1))
        q_ref, k_ref, v_ref, z_ref, c_ref, ct_ref, dy_ref, o_ref, lse_ref = ins
        dk_ref, dv_ref, dc_ref = outs
        dk_s, dv_s, dc_s = scratch

        @pl.when(i == j)
        def _():
            dk_s[...] = jnp.zeros_like(dk_s)
            dv_s[...] = jnp.zeros_like(dv_s)
            dc_s[...] = jnp.zeros_like(dc_s)

        def step(diagonal):
            z = z_ref[...]
            do = dy_ref[...].astype(F32) * _silu(z)
            delta = jnp.sum(do.astype(MXU_DTYPE).astype(F32) * o_ref[...], axis=1, keepdims=True)
            cq = _colsel(c_ref[...], FORGET_LANE + h)
            s = _att_scores(q_ref[...], k_ref[...], cq, ct_ref[0], diagonal)
            p = jnp.exp(s - jnp.max(lse_ref[0], axis=1, keepdims=True))
            dv_s[...] += _dot(p, do, TN)
            ds = p * (_dot(do, v_ref[...], NT) - delta)
            dk_s[...] += _dot(ds, q_ref[...], TN)
            dc_s[...] -= jnp.sum(ds, axis=0, keepdims=True)

        @pl.when(i == j)
        def _():
            step(True)

        @pl.when(i > j)
        def _():
            step(False)

        @pl.when(i == nq - 1)
        def _():
            dk_ref[...] = (dk_s[...] * (HEAD_DIM ** -0.5)).astype(dk_ref.dtype)
            dv_ref[...] = dv_s[...].astype(dv_ref.dtype)
            dc_ref[0] = dc_s[...]

        finish()

    qs, kv, c_spec, ct_spec, lse_spec = _attn_specs(tq)
    res = pl.pallas_call(
        body, name=name,
        grid_spec=pltpu.PrefetchScalarGridSpec(
            num_scalar_prefetch=2, grid=(HEADS, n_pairs),
            in_specs=[qs(qb), kv(qb + HEADS), kv(qb + 2 * HEADS), qs(zb), c_spec, ct_spec, qs(0), qs(0), lse_spec]
            + [HBM_SPEC] * len(c_arrs),
            out_specs=[kv(0), kv(0), pl.BlockSpec((1, 1, tq), lambda h, t, it, jt: (h, 0, jt[t]))]
            + [HBM_SPEC] * len(c_arrs),
            scratch_shapes=[pltpu.VMEM((tq, HEAD_DIM), F32), pltpu.VMEM((tq, HEAD_DIM), F32), pltpu.VMEM((1, tq), F32)]
            + c_sems),
        out_shape=[jax.ShapeDtypeStruct((seq, BRANCH), BF16)] * 2 + [jax.ShapeDtypeStruct((HEADS, 1, seq), F32)]
        + c_shapes,
        compiler_params=_params(("arbitrary", "arbitrary")),
    )(i_tab, j_tab, proj, proj, proj, proj, c, ct3, dy, o, lse, *c_arrs)
    return res[0], res[1], res[2], list(res[3:])


def _loss_head(name, y, target):
    seq, d = y.shape
    tile = min(256, seq)

    def body(y_ref, t_ref, dy_ref, l_ref):
        err = y_ref[...] - t_ref[...]
        dy_ref[...] = err / d

        @pl.when(pl.program_id(0) == 0)
        def _():
            l_ref[...] = jnp.zeros_like(l_ref)

        l_ref[...] += 0.5 * jnp.sum(jnp.mean(err * err, axis=-1, keepdims=True), axis=0, keepdims=True)

    return pl.pallas_call(
        body, name=name, grid=(seq // tile,),
        in_specs=[pl.BlockSpec((tile, d), lambda i: (i, 0))] * 2,
        out_specs=[pl.BlockSpec((tile, d), lambda i: (i, 0)), pl.BlockSpec((8, LANE), lambda i: (0, 0))],
        out_shape=[jax.ShapeDtypeStruct((seq, d), F32), jax.ShapeDtypeStruct((8, LANE), F32)],
        compiler_params=_params(("arbitrary",)),
    )(y, target)


def _adamw(name, w, m, v, g_parts, tile):
    rows, cols = w.shape
    tile = min(tile, rows)
    assert rows % tile == 0, (name, w.shape)
    n_g = len(g_parts)

    def body(*refs):
        w_ref, m_ref, v_ref = refs[:3]
        g_refs = refs[3:3 + n_g]
        g_ref, d_ref, nm_ref, nv_ref = refs[3 + n_g:]
        g = None
        for r in g_refs:
            parts = [r[...]] if len(r.shape) == 2 else [r[i] for i in range(r.shape[0])]
            for p in parts:
                g = p.astype(F32) if g is None else g + p.astype(F32)
        m_new = ADAM_B1 * m_ref[...] + (1.0 - ADAM_B1) * g
        v_new = ADAM_B2 * v_ref[...] + (1.0 - ADAM_B2) * (g * g)
        m_hat = m_new / (1.0 - ADAM_B1 ** ADAM_STEP)
        v_hat = v_new / (1.0 - ADAM_B2 ** ADAM_STEP)
        g_ref[...] = g
        d_ref[...] = -ADAM_LR * (m_hat / (jnp.sqrt(v_hat) + ADAM_EPS) + ADAM_WD * w_ref[...])
        nm_ref[...] = m_new
        nv_ref[...] = v_new

    blk = pl.BlockSpec((tile, cols), lambda i: (i, 0))
    g_specs = [blk if p.ndim == 2 else pl.BlockSpec((p.shape[0], tile, cols), lambda i: (0, i, 0)) for p in g_parts]
    return pl.pallas_call(
        body, name=name, grid=(rows // tile,), in_specs=[blk] * 3 + g_specs, out_specs=[blk] * 4,
        out_shape=[jax.ShapeDtypeStruct((rows, cols), F32)] * 4,
        compiler_params=_params(("parallel",)),
    )(w, m, v, *g_parts)


_ORIG_SEGMENTS = (
    ("qkv_a", 0, 3072), ("z_a", 3072, 1024), ("beta", 4096, 8), ("alpha", 4104, 8), ("glu", 4112, 2048),
    ("z_b", 6160, 1024), ("qkv_c", 7184, 3072), ("z_c", 10256, 1024), ("forget", 11280, 8), ("gate", 11288, 6144))
_PAD_ORDER = ("gate", "qkv_a", "z_a", "glu", "z_b", "qkv_c", "z_c", "beta", "alpha", "forget")


def _pad_cols(w):
    seg = {n: w[..., s:s + k] for n, s, k in _ORIG_SEGMENTS}
    fill = jnp.zeros(w.shape[:-1] + (N_PAD - N_IN,), w.dtype)
    return jnp.concatenate([seg[n] for n in _PAD_ORDER] + [fill], axis=-1)


def _unpad_cols(g):
    off, seg = 0, {}
    widths = {n: k for n, _, k in _ORIG_SEGMENTS}
    for n in _PAD_ORDER:
        seg[n] = g[..., off:off + widths[n]]
        off += widths[n]
    return jnp.concatenate([seg[n] for n, _, _ in _ORIG_SEGMENTS], axis=-1)


def _lane_row(vals, lane0):
    return jnp.pad(vals.astype(F32), (lane0, LANE - HEADS - lane0))[None]


def _layer_fwd(x, p, comm=None):
    seq = x.shape[0]
    h = _rowwise("rms_pre", _rms_pre_fn, [(x, D_MODEL, 0)], [p["pre_w"]], [(D_MODEL, BF16)], 256)[0]
    proj = _matmul("mm_in", h, p["w_in"], "nn", F32, 512, 1280, 2048)
    ca = _conv_fwd("conv_a", proj, OFF_QKVA, 3 * BRANCH, p["w4"], jnp.zeros((1, 3 * BRANCH), F32), SHORT_CONV)
    y_a, states = _gdr_fwd("gdr_fwd", ca, proj, p["alog"], p["dtb"], p["onw"])
    u2 = _conv_fwd("conv_b", proj, OFF_VAL, BRANCH, p["w31"], p["cb"], CONF_CONV, gate_off=OFF_GLUG)
    y_b = _rowwise("ln_gate", _ln_gate_fn, [(u2, BRANCH, 0), (proj, BRANCH, OFF_ZB // BRANCH)],
                   [p["ln_w"], p["ln_b"]], [(BRANCH, BF16)], 256)[0]
    c, ct = _fox_gate_fwd("fox_gate", proj, p["fb"])
    ct3 = ct.reshape(LANE, 1, seq)
    y_c, o_c, lse, got = _attn_fwd("attn_fwd", proj, c, ct3, comm)
    ys = (y_a, y_b, y_c)
    br = [_matmul("mm_br", ys[n], p["wbr"][n], "nn", F32, 512, 2048, 1024) for n in range(N_BRANCH)]
    merged = _rowwise("merge", _merge_fn, [(proj, D_MODEL, n) for n in range(N_BRANCH)] + [(b, D_MODEL, 0) for b in br],
                      [], [(D_MODEL, BF16)], 256)[0]
    out = _matmul("mm_out", merged, p["wout"], "nn", F32, 512, 2048, 2048)
    x_new = _rowwise("rms_post", _rms_post_fn, [(out, D_MODEL, 0), (x, D_MODEL, 0)], [p["post_w"]],
                     [(D_MODEL, F32)], 256)[0]
    saved = dict(x=x, ht=h.T, proj=proj, ca=ca, states=states, u2=u2, c=c, ct3=ct3, o_c=o_c, lse=lse, ys=ys, br=br,
                 merged=merged, out=out)
    return x_new, saved, got


def _layer_bwd(dxn, p, sv, comm=None):
    x, proj = sv["x"], sv["proj"]
    seq = x.shape[0]
    g = {}
    d_out, g["post_w"] = _rowwise_bwd("rms_post_bwd", _rms_only_fn, [(sv["out"], D_MODEL, 0)], [p["post_w"]], [dxn],
                                      [BF16], 256)
    d_merged = _matmul("mm_out_dx", d_out, p["wout"], "nt", F32, 512, 2048, 2048)
    g["wout"] = _matmul("mm_out_dw", sv["merged"], d_out, "tn", F32, 1024, 1024, 512)
    rows = [(proj, D_MODEL, n) for n in range(N_BRANCH)] + [(b, D_MODEL, 0) for b in sv["br"]]
    d_gl0, d_gl1, d_gl2, d_b0, d_b1, d_b2 = _rowwise_bwd("merge_bwd", _merge_fn, rows, [], [d_merged], [BF16] * 6, 128)
    d_br = (d_b0, d_b1, d_b2)
    dys = [_matmul("mm_br_dx", d_br[n], p["wbr"][n], "nt", BF16, 512, 1024, 2048) for n in range(N_BRANCH)]
    g["wbr"] = jnp.stack([_matmul("mm_br_dw", sv["ys"][n], d_br[n], "tn", F32, 1024, 1024, 512)
                          for n in range(N_BRANCH)])
    dq, dzc = _attn_dq("attn_dq", dys[2], sv["o_c"], sv["lse"], proj, sv["c"], sv["ct3"])
    dk, dv, dck, got = _attn_dkv("attn_dkv", dys[2], sv["o_c"], sv["lse"], proj, sv["c"], sv["ct3"], comm)
    dsm_c, g["fb"] = _fox_gate_bwd("fox_gate_bwd", dck.reshape(HEADS, seq), proj, p["fb"])
    du2, dzb, g["ln_w"], g["ln_b"] = _rowwise_bwd(
        "ln_gate_bwd", _ln_gate_fn, [(sv["u2"], BRANCH, 0), (proj, BRANCH, OFF_ZB // BRANCH)], [p["ln_w"], p["ln_b"]],
        [dys[1]], [F32, BF16], 256)
    dval, dgate, g["w31"], g["cb"] = _conv_bwd("conv_b_bwd", du2, proj, OFF_VAL, BRANCH, p["w31"], CONF_CONV,
                                               gate_off=OFF_GLUG)
    dca, dza, dsm_a, g["alog"], g["dtb"], g["onw"] = _gdr_bwd("gdr_bwd", dys[0], sv["states"], sv["ca"], proj,
                                                              p["alog"], p["dtb"], p["onw"])
    dqkva, g["w4"], _ = _conv_bwd("conv_a_bwd", dca, proj, OFF_QKVA, 3 * BRANCH, p["w4"], SHORT_CONV)
    d_small = jnp.pad((dsm_a + dsm_c).astype(BF16), ((0, 0), (0, N_PAD - OFF_SMALL - LANE)))
    d_proj = jnp.concatenate([d_gl0, d_gl1, d_gl2, dqkva, dza, dval, dgate, dzb, dq, dk, dv, dzc, d_small], axis=1)
    dh = _matmul("mm_in_dx", d_proj, p["w_in"], "nt", F32, 512, 2048, 2560)
    g["w_in"] = _matmul("mm_in_dw", sv["ht"], d_proj, "nn", F32, 2048, 640, 2048)
    dx, g["pre_w"] = _rowwise_bwd("rms_pre_bwd", _rms_pre_res_fn, [(x, D_MODEL, 0)], [p["pre_w"]], [dh, dxn], [F32], 256)
    return dx, g, got


N_CHIPS = 4
N_DEV = 8
HBM_SPEC = pl.BlockSpec(memory_space=pltpu.HBM)


def _mesh_pos():
    return lax.axis_index("x"), lax.axis_index("y"), lax.axis_index("c")


def _other_chips(x, y):
    return [(1 - x, y), (x, 1 - y), (1 - x, 1 - y)]


def _comm_plan(kind, arrs):
    n = len(arrs)
    if kind == "allgather":
        return ([jax.ShapeDtypeStruct((3,) + a.shape, a.dtype) for a in arrs], [pltpu.SemaphoreType.DMA((3 * n,))] * 4)
    return ([jax.ShapeDtypeStruct((3,) + a.shape[1:], a.dtype) for a in arrs], [pltpu.SemaphoreType.DMA((3 * n,))] * 2)


def _comm_ops(kind, ins, outs, sems):
    return (_allgather_ops if kind == "allgather" else _reduce_scatter_ops)(ins, outs, sems)


def _allgather_ops(ins, outs, sems):
    send_sems, recv_sems, pass_send, pass_recv = sems
    n = len(ins)
    x, y, c = _mesh_pos()

    def part(a, core):
        half = ins[a].shape[0] // 2
        return pl.ds(half * core, half)

    def ici(a, j):
        px, py = _other_chips(x, y)[j]
        return pltpu.make_async_remote_copy(
            src_ref=ins[a].at[part(a, c)], dst_ref=outs[a].at[j, part(a, c)], send_sem=send_sems.at[3 * a + j],
            recv_sem=recv_sems.at[3 * a + j], device_id=(px, py, c), device_id_type=MESH)

    def d2d(a, j, core):
        blk = outs[a].at[j, part(a, core)]
        return pltpu.make_async_remote_copy(
            src_ref=blk, dst_ref=blk, send_sem=pass_send.at[3 * a + j], recv_sem=pass_recv.at[3 * a + j],
            device_id=(x, y, 1 - c), device_id_type=MESH)

    def start():
        for a in range(n):
            for j in range(3):
                ici(a, j).start()

    def finish():
        for a in range(n):
            for j in range(3):
                ici(a, j).wait_recv()
                d2d(a, j, c).start()
        for a in range(n):
            for j in range(3):
                d2d(a, j, 1 - c).wait_recv()
        for a in range(n):
            for j in range(3):
                ici(a, j).wait_send()
                d2d(a, j, c).wait_send()

    return start, finish


def _reduce_scatter_ops(ins, outs, sems):
    send_sems, recv_sems = sems
    n = len(ins)
    x, y, c = _mesh_pos()

    def remote(a, j):
        px, py = _other_chips(x, y)[j]
        return pltpu.make_async_remote_copy(
            src_ref=ins[a].at[2 * px + py], dst_ref=outs[a].at[j], send_sem=send_sems.at[3 * a + j],
            recv_sem=recv_sems.at[3 * a + j], device_id=(px, py, c), device_id_type=MESH)

    def start():
        for a in range(n):
            for j in range(3):
                remote(a, j).start()

    def finish():
        for a in range(n):
            for j in range(3):
                remote(a, j).wait_recv()
        for a in range(n):
            for j in range(3):
                remote(a, j).wait_send()

    return start, finish


def _exchange_chips(name, kind, arrs):
    n = len(arrs)
    shapes, sems = _comm_plan(kind, arrs)

    def body(*refs):
        start, finish = _comm_ops(kind, refs[:n], refs[n:2 * n], refs[2 * n:])
        start()
        finish()

    return pl.pallas_call(body, name=name, in_specs=[HBM_SPEC] * n, out_specs=[HBM_SPEC] * n, out_shape=shapes,
                          scratch_shapes=sems)(*arrs)


def _halves_to_sibling(name, arrs):
    n = len(arrs)

    def body(*refs):
        ins, outs = refs[:n], refs[n:2 * n]
        send_sems, recv_sems = refs[2 * n:]
        x, y, c = _mesh_pos()
        copies = []
        for a in range(n):
            rows = ins[a].shape[1] // 2
            for s in range(N_CHIPS):
                copies.append(pltpu.make_async_remote_copy(
                    src_ref=ins[a].at[s, pl.ds((1 - c) * rows, rows)], dst_ref=outs[a].at[s],
                    send_sem=send_sems.at[N_CHIPS * a + s], recv_sem=recv_sems.at[N_CHIPS * a + s],
                    device_id=(x, y, 1 - c), device_id_type=MESH))
        for cp in copies:
            cp.start()
        for cp in copies:
            cp.wait_recv()
        for cp in copies:
            cp.wait_send()

    return pl.pallas_call(
        body, name=name, in_specs=[HBM_SPEC] * n, out_specs=[HBM_SPEC] * n,
        out_shape=[jax.ShapeDtypeStruct((a.shape[0], a.shape[1] // 2, a.shape[2]), a.dtype) for a in arrs],
        scratch_shapes=[pltpu.SemaphoreType.DMA((N_CHIPS * n,)), pltpu.SemaphoreType.DMA((N_CHIPS * n,))],
    )(*arrs)


def _add_own_half(name, full, other, core, tile):
    n, rows, cols = other.shape
    tile = min(tile, rows)
    assert rows % tile == 0, (name, other.shape)
    nb = rows // tile

    def body(c_ref, f_ref, o_ref, out_ref):
        out_ref[...] = f_ref[...] + o_ref[...]

    return pl.pallas_call(
        body, name=name,
        grid_spec=pltpu.PrefetchScalarGridSpec(
            num_scalar_prefetch=1, grid=(n, nb),
            in_specs=[pl.BlockSpec((1, tile, cols), lambda s, i, c_ref: (s, c_ref[0] * nb + i, 0)),
                      pl.BlockSpec((1, tile, cols), lambda s, i, c_ref: (s, i, 0))],
            out_specs=pl.BlockSpec((1, tile, cols), lambda s, i, c_ref: (s, i, 0))),
        out_shape=jax.ShapeDtypeStruct(other.shape, F32),
        compiler_params=_params(("parallel", "parallel")),
    )(core, full, other)


def _join_cores(name, arrs):
    n = len(arrs)

    def body(*refs):
        bufs = refs[n:2 * n]
        send_sems, recv_sems = refs[2 * n:]
        x, y, c = _mesh_pos()

        def copy(a, slot):
            return pltpu.make_async_remote_copy(
                src_ref=bufs[a].at[slot], dst_ref=bufs[a].at[slot], send_sem=send_sems.at[a], recv_sem=recv_sems.at[a],
                device_id=(x, y, 1 - c), device_id_type=MESH)

        for a in range(n):
            copy(a, c).start()
        for a in range(n):
            copy(a, 1 - c).wait_recv()
        for a in range(n):
            copy(a, c).wait_send()

    return pl.pallas_call(
        body, name=name, in_specs=[HBM_SPEC] * n, out_specs=[HBM_SPEC] * n,
        out_shape=[jax.ShapeDtypeStruct(a.shape, a.dtype) for a in arrs],
        input_output_aliases={a: a for a in range(n)},
        scratch_shapes=[pltpu.SemaphoreType.DMA((n,)), pltpu.SemaphoreType.DMA((n,))],
    )(*arrs)


def _sum_chips(name, own, recv, chip, core, tile):
    _, rows, cols = recv.shape
    tile = min(tile, rows)
    assert rows % tile == 0, (name, recv.shape)

    def body(chip_ref, core_ref, own_ref, recv_ref, o_ref):
        o_ref[0] = ((own_ref[0] + recv_ref[0]) + recv_ref[1]) + recv_ref[2]

    return pl.pallas_call(
        body, name=name,
        grid_spec=pltpu.PrefetchScalarGridSpec(
            num_scalar_prefetch=2, grid=(rows // tile,),
            in_specs=[pl.BlockSpec((1, tile, cols), lambda i, chip_ref, core_ref: (chip_ref[0], i, 0)),
                      pl.BlockSpec((3, tile, cols), lambda i, chip_ref, core_ref: (0, i, 0))],
            out_specs=pl.BlockSpec((1, tile, cols), lambda i, chip_ref, core_ref: (core_ref[0], i, 0))),
        out_shape=jax.ShapeDtypeStruct((2, rows, cols), F32),
        compiler_params=_params(("parallel",)),
    )(chip, core, own, recv)


def _allgather_devices(name, buf):
    def body(in_ref, out_ref, send_sems, recv_sems):
        x, y, c = _mesh_pos()
        me = 4 * x + 2 * y + c
        out_ref[me] = in_ref[...]

        def remote(k, slot):
            peer = (x ^ (k >> 2), y ^ ((k >> 1) & 1), c ^ (k & 1))
            return pltpu.make_async_remote_copy(
                src_ref=in_ref, dst_ref=out_ref.at[slot], send_sem=send_sems.at[k - 1], recv_sem=recv_sems.at[k - 1],
                device_id=peer, device_id_type=MESH)

        for k in range(1, N_DEV):
            remote(k, me).start()
        for k in range(1, N_DEV):
            remote(k, me ^ k).wait_recv()
        for k in range(1, N_DEV):
            remote(k, me).wait_send()

    vmem = pl.BlockSpec(memory_space=pltpu.VMEM)
    return pl.pallas_call(
        body, name=name, in_specs=[vmem], out_specs=vmem,
        out_shape=jax.ShapeDtypeStruct((N_DEV,) + buf.shape, buf.dtype),
        scratch_shapes=[pltpu.SemaphoreType.DMA((N_DEV - 1,)), pltpu.SemaphoreType.DMA((N_DEV - 1,))],
    )(buf)


def _sum_slots(name, a, tile):
    n, rows, cols = a.shape
    tile = min(tile, rows)
    assert rows % tile == 0, (name, a.shape)

    def body(a_ref, o_ref):
        acc = a_ref[0].astype(F32)
        for i in range(1, n):
            acc = acc + a_ref[i].astype(F32)
        o_ref[...] = acc

    return pl.pallas_call(
        body, name=name, grid=(rows // tile,),
        in_specs=[pl.BlockSpec((n, tile, cols), lambda i: (0, i, 0))],
        out_specs=pl.BlockSpec((tile, cols), lambda i: (i, 0)),
        out_shape=jax.ShapeDtypeStruct((rows, cols), F32),
        compiler_params=_params(("parallel",)),
    )(a)


_SMALL = (
    ("pre_norm_w", (DEPTH, D_MODEL)), ("post_norm_w", (DEPTH, D_MODEL)), ("a_log", (DEPTH, HEADS)),
    ("dt_bias", (DEPTH, HEADS)), ("o_norm_w", (DEPTH, HEAD_DIM)), ("conv_b", (DEPTH, BRANCH)), ("ln_w", (DEPTH, BRANCH)),
    ("ln_b", (DEPTH, BRANCH)), ("f_bias", (DEPTH, HEADS)), ("conv_qkv_w", (DEPTH, SHORT_CONV, 3 * BRANCH)),
    ("conv_w", (DEPTH, CONF_CONV, BRANCH)), ("loss", (1,)))
_SHARDED_SMALL = {"conv_qkv_w": 3 * BRANCH // N_CHIPS, "conv_w": BRANCH // N_CHIPS}
_WEIGHTS = ("pre_norm_w", "post_norm_w", "w_in", "conv_qkv_w", "a_log", "dt_bias", "o_norm_w", "conv_w", "conv_b",
            "ln_w", "ln_b", "f_bias", "w_branch", "w_out")


def _pack(parts):
    flat = jnp.concatenate([p.reshape(-1).astype(F32) for p in parts])
    rows = -(-flat.shape[0] // (8 * LANE)) * 8
    return jnp.pad(flat, (0, rows * LANE - flat.shape[0])).reshape(rows, LANE)


def _unpack(buf, shapes):
    flat, out, off = buf.reshape(-1), [], 0
    for shp in shapes:
        size = 1
        for s in shp:
            size *= s
        out.append(flat[off:off + size].reshape(shp))
        off += size
    return out


def kernel(x, pre_norm_w, post_norm_w, w_in, conv_qkv_w, a_log, dt_bias, o_norm_w, conv_w, conv_b, ln_w, ln_b, f_bias, w_branch, w_out, loss_target, m_pre_norm_w, m_post_norm_w, m_w_in, m_conv_qkv_w, m_a_log, m_dt_bias, m_o_norm_w, m_conv_w, m_conv_b, m_ln_w, m_ln_b, m_f_bias, m_w_branch, m_w_out, v_pre_norm_w, v_post_norm_w, v_w_in, v_conv_qkv_w, v_a_log, v_dt_bias, v_o_norm_w, v_conv_w, v_conv_b, v_ln_w, v_ln_b, v_f_bias, v_w_branch, v_w_out):
    weights = dict(pre_norm_w=pre_norm_w, post_norm_w=post_norm_w, w_in=w_in, conv_qkv_w=conv_qkv_w, a_log=a_log,
                   dt_bias=dt_bias, o_norm_w=o_norm_w, conv_w=conv_w, conv_b=conv_b, ln_w=ln_w, ln_b=ln_b, f_bias=f_bias,
                   w_branch=w_branch, w_out=w_out)
    mom1 = dict(pre_norm_w=m_pre_norm_w, post_norm_w=m_post_norm_w, w_in=m_w_in, conv_qkv_w=m_conv_qkv_w, a_log=m_a_log,
                dt_bias=m_dt_bias, o_norm_w=m_o_norm_w, conv_w=m_conv_w, conv_b=m_conv_b, ln_w=m_ln_w, ln_b=m_ln_b,
                f_bias=m_f_bias, w_branch=m_w_branch, w_out=m_w_out)
    mom2 = dict(pre_norm_w=v_pre_norm_w, post_norm_w=v_post_norm_w, w_in=v_w_in, conv_qkv_w=v_conv_qkv_w, a_log=v_a_log,
                dt_bias=v_dt_bias, o_norm_w=v_o_norm_w, conv_w=v_conv_w, conv_b=v_conv_b, ln_w=v_ln_w, ln_b=v_ln_b,
                f_bias=v_f_bias, w_branch=v_w_branch, w_out=v_w_out)
    chip = 2 * lax.axis_index("x") + lax.axis_index("y")
    core = lax.axis_index("c").astype(jnp.int32).reshape(1)
    chip_id = chip.astype(jnp.int32).reshape(1)

    w_in_b, w_out_b = w_in.astype(BF16), w_out.astype(BF16)
    w_br_b = w_branch.astype(BF16).reshape(DEPTH, N_BRANCH * BRANCH, D_MODEL // N_CHIPS)
    shards = lambda l: [w_in_b[l], w_br_b[l], w_out_b[l]]

    def whole(own, got, axis=-1):
        parts = []
        for s in range(N_CHIPS):
            d = chip ^ s
            parts.append(jnp.where(d == 0, own, jnp.where(d == 2, got[0], jnp.where(d == 1, got[1], got[2]))))
        return jnp.concatenate(parts, axis=axis)

    first = _exchange_chips("allgather_weights", "allgather", shards(0) + [conv_qkv_w, conv_w])
    c4_full = jnp.pad(whole(conv_qkv_w, first[3]), ((0, 0), (0, 8 - SHORT_CONV), (0, 0)))
    c31_full = jnp.pad(whole(conv_w, first[4]), ((0, 0), (0, 32 - CONF_CONV), (0, 0)))

    def layer_params(l, got):
        return dict(
            pre_w=pre_norm_w[l][None], post_w=post_norm_w[l][None], w_in=_pad_cols(whole(w_in_b[l], got[0])),
            w4=c4_full[l], alog=_lane_row(a_log[l], ALPHA_LANE), dtb=_lane_row(dt_bias[l], ALPHA_LANE),
            onw=o_norm_w[l][None], w31=c31_full[l], cb=conv_b[l][None], ln_w=ln_w[l][None], ln_b=ln_b[l][None],
            fb=_lane_row(f_bias[l], FORGET_LANE),
            wbr=whole(w_br_b[l], got[1]).reshape(N_BRANCH, BRANCH, D_MODEL), wout=whole(w_out_b[l], got[2], axis=0))

    act = x[0]
    got = first[:3]
    layers, saved = [], []
    for l in range(DEPTH):
        layers.append(layer_params(l, got))
        act, sv, got = _layer_fwd(act, layers[l], ("allgather", shards(l + 1)) if l + 1 < DEPTH else None)
        saved.append(sv)
    d_act, loss_blk = _loss_head("loss_head", act, loss_target[0])

    parts_in, parts_br, parts_out, small_g = [], [], [], []

    def finish_reduce(own, recv):
        mine = [_sum_chips("sum_chips", o, r, chip_id, core, t) for o, r, t in zip(own, recv, (64, 512, 128))]
        joined = _join_cores("join_cores", mine)
        for lst, j in zip((parts_in, parts_br, parts_out), joined):
            lst.append(j.reshape(2 * j.shape[1], j.shape[2]))

    pending = None
    for l in reversed(range(DEPTH)):
        d_act, g, recv = _layer_bwd(d_act, layers[l], saved[l], ("reduce_scatter", pending) if pending else None)
        if pending:
            finish_reduce(pending, recv)
        g_in = _unpad_cols(g["w_in"]).reshape(D_MODEL, N_CHIPS, N_IN // N_CHIPS).transpose(1, 0, 2)
        g_br = g["wbr"].reshape(N_BRANCH * BRANCH, N_CHIPS, D_MODEL // N_CHIPS).transpose(1, 0, 2)
        g_out = g["wout"].reshape(N_CHIPS, D_MODEL // N_CHIPS, D_MODEL)
        parts = [g_in, g_br, g_out]
        other = _halves_to_sibling("halves_to_sibling", parts)
        pending = [_add_own_half("add_own_half", f, o, core, t) for f, o, t in zip(parts, other, (64, 512, 128))]
        small_g.append(g)
    finish_reduce(pending, _exchange_chips("reduce_scatter_grads", "reduce_scatter", pending))
    small_g = small_g[::-1]
    parts_in, parts_br, parts_out = parts_in[::-1], parts_br[::-1], parts_out[::-1]

    stack = lambda key, f=lambda a: a: jnp.stack([f(g[key]) for g in small_g])
    small = dict(
        pre_norm_w=stack("pre_w", lambda a: a[0]), post_norm_w=stack("post_w", lambda a: a[0]),
        a_log=stack("alog", lambda a: a[0, ALPHA_LANE:ALPHA_LANE + HEADS]),
        dt_bias=stack("dtb", lambda a: a[0, ALPHA_LANE:ALPHA_LANE + HEADS]), o_norm_w=stack("onw", lambda a: a[0]),
        conv_b=stack("cb", lambda a: a[0]), ln_w=stack("ln_w", lambda a: a[0]), ln_b=stack("ln_b", lambda a: a[0]),
        f_bias=stack("fb", lambda a: a[0, FORGET_LANE:FORGET_LANE + HEADS]),
        conv_qkv_w=stack("w4", lambda a: a[:SHORT_CONV]), conv_w=stack("w31", lambda a: a[:CONF_CONV]),
        loss=loss_blk[0, 0:1])
    gathered = _allgather_devices("allgather_small", _pack([small[n] for n, _ in _SMALL]))
    total = _unpack(_sum_slots("sum_devices", gathered, gathered.shape[1]), [s for _, s in _SMALL])
    total = {n: t for (n, _), t in zip(_SMALL, total)}
    loss = total.pop("loss")[0]
    for n, width in _SHARDED_SMALL.items():
        total[n] = lax.dynamic_slice_in_dim(total[n], chip * width, width, axis=2)

    names = list(total)
    packed = [_pack([d[n] for n in names]) for d in (weights, mom1, mom2)]
    res = _adamw("adamw_small", packed[0], packed[1], packed[2], [_pack([total[n] for n in names])], packed[0].shape[0])
    shapes = [weights[n].shape for n in names]
    grads, delta, new_m, new_v = [dict(zip(names, _unpack(r, shapes))) for r in res]
    big = (("w_in", parts_in, (DEPTH * D_MODEL, N_IN // N_CHIPS), 64),
           ("w_branch", parts_br, (DEPTH * N_BRANCH * BRANCH, D_MODEL // N_CHIPS), 512),
           ("w_out", parts_out, (DEPTH * D_MODEL // N_CHIPS, D_MODEL), 128))
    for n, parts, shape2, tile in big:
        res = _adamw("adamw_" + n, weights[n].reshape(shape2), mom1[n].reshape(shape2), mom2[n].reshape(shape2),
                     [jnp.concatenate(parts, axis=0)], tile)
        grads[n], delta[n], new_m[n], new_v[n] = [r.reshape(weights[n].shape) for r in res]

    outs = [loss, d_act[None]]
    for d in (grads, delta, new_m, new_v):
        outs += [d[n] for n in _WEIGHTS]
    return tuple(outs)
```

```python
import functools

import jax
import jax.numpy as jnp
from jax import lax
from jax.experimental import pallas as pl
from jax.experimental.pallas import tpu as pltpu

F32 = jnp.float32
BF16 = jnp.bfloat16
MXU_DTYPE = jnp.bfloat16

D_MODEL = 2048
DEPTH = 4
BRANCH = 1024
HEAD_DIM = 128
HEADS = 8
CHUNK = 64
SHORT_CONV = 4
CONF_CONV = 31
N_BRANCH = 3
NORM_EPS = 1e-6
N_IN = 17432

OFF_GATE = 0
OFF_QKVA = 6144
OFF_ZA = 9216
OFF_VAL = 10240
OFF_GLUG = 11264
OFF_ZB = 12288
OFF_QKVC = 13312
OFF_ZC = 16384
OFF_SMALL = 17408
N_PAD = 17920
LANE = 128
BETA_LANE, ALPHA_LANE, FORGET_LANE = 0, 8, 16

ADAM_LR = 0.001
ADAM_B1 = 0.9
ADAM_B2 = 0.999
ADAM_EPS = 1e-08
ADAM_WD = 0.01
ADAM_STEP = 10

VMEM_LIMIT = 56 * 1024 * 1024

NN = (((1,), (0,)), ((), ()))
NT = (((1,), (1,)), ((), ()))
TN = (((0,), (0,)), ((), ()))
MESH = pl.DeviceIdType.MESH


def _params(sem=None):
    return pltpu.CompilerParams(dimension_semantics=sem, vmem_limit_bytes=VMEM_LIMIT)


def _dot(a, b, dims=NN):
    return lax.dot_general(a.astype(MXU_DTYPE), b.astype(MXU_DTYPE), dims, preferred_element_type=F32)


def _dot_hi(a, b, dims=NN):
    return lax.dot_general(a, b, dims, precision=lax.Precision.HIGHEST, preferred_element_type=F32)


def _dot_3x(a, b, dims=NN):
    a_hi, b_hi = a.astype(BF16), b.astype(BF16)
    a_lo, b_lo = (a - a_hi.astype(F32)).astype(BF16), (b - b_hi.astype(F32)).astype(BF16)
    dot = lambda u, v: lax.dot_general(u, v, dims, preferred_element_type=F32)
    return dot(a_hi, b_hi) + (dot(a_hi, b_lo) + dot(a_lo, b_hi))


def _sigmoid(x):
    return 1.0 / (1.0 + jnp.exp(-x))


def _silu(x):
    return x * _sigmoid(x)


def _softplus(x):
    return jnp.maximum(x, 0.0) + jnp.log(1.0 + jnp.exp(-jnp.abs(x)))


def _colsel(m, j):
    lane = lax.broadcasted_iota(jnp.int32, m.shape, 1)
    return jnp.sum(jnp.where(lane == j, m, 0.0), axis=1, keepdims=True)


def _rowsel(m, j):
    sub = lax.broadcasted_iota(jnp.int32, m.shape, 0)
    return jnp.sum(jnp.where(sub == j, m, 0.0), axis=0, keepdims=True)


def _row_specs(rows, tile):
    return [pl.BlockSpec((tile, w), functools.partial(lambda i, cb: (i, cb), cb=cb)) for (_, w, cb) in rows]


def _rowwise(name, fn, rows, params, outs, tile):
    seq = rows[0][0].shape[0]
    tile = min(tile, seq)
    n_in = len(rows) + len(params)

    def body(*refs):
        res = fn(*[r[...] for r in refs[:n_in]])
        for o_ref, r in zip(refs[n_in:], res):
            o_ref[...] = r.astype(o_ref.dtype)

    return pl.pallas_call(
        body, name=name, grid=(seq // tile,),
        in_specs=_row_specs(rows, tile) + [pl.BlockSpec(p.shape, lambda i: (0, 0)) for p in params],
        out_specs=[pl.BlockSpec((tile, w), lambda i: (i, 0)) for (w, _) in outs],
        out_shape=[jax.ShapeDtypeStruct((seq, w), dt) for (w, dt) in outs],
        compiler_params=_params(("parallel",)),
    )(*[r[0] for r in rows], *params)


def _rowwise_bwd(name, fn, rows, params, cts, row_grads, tile):
    seq = rows[0][0].shape[0]
    tile = min(tile, seq)
    nr, npar, nct = len(rows), len(params), len(cts)
    n_in = nr + npar

    def body(*refs):
        vals = [r[...] for r in refs[:n_in]]
        res, vjp = jax.vjp(fn, *vals)
        grads = vjp(tuple(c[...].astype(r.dtype) for c, r in zip(refs[n_in:n_in + nct], res)))
        outs = refs[n_in + nct:]
        k = 0
        for idx, dt in enumerate(row_grads):
            if dt is not None:
                outs[k][...] = grads[idx].astype(dt)
                k += 1
        first = pl.program_id(0) == 0
        for j in range(npar):
            g = grads[nr + j].astype(F32)
            o_ref = outs[k + j]

            @pl.when(first)
            def _(o_ref=o_ref, g=g):
                o_ref[...] = g

            @pl.when(jnp.logical_not(first))
            def _(o_ref=o_ref, g=g):
                o_ref[...] += g

    want = [(rows[i][1], dt) for i, dt in enumerate(row_grads) if dt is not None]
    return pl.pallas_call(
        body, name=name, grid=(seq // tile,),
        in_specs=(_row_specs(rows, tile) + [pl.BlockSpec(p.shape, lambda i: (0, 0)) for p in params]
                  + [pl.BlockSpec((tile, c.shape[1]), lambda i: (i, 0)) for c in cts]),
        out_specs=([pl.BlockSpec((tile, w), lambda i: (i, 0)) for (w, _) in want]
                   + [pl.BlockSpec(p.shape, lambda i: (0, 0)) for p in params]),
        out_shape=([jax.ShapeDtypeStruct((seq, w), dt) for (w, dt) in want]
                   + [jax.ShapeDtypeStruct(p.shape, F32) for p in params]),
        compiler_params=_params(("arbitrary",)),
    )(*[r[0] for r in rows], *params, *cts)


def _rms(x, w):
    x = x.astype(F32)
    return x * lax.rsqrt(jnp.mean(x * x, axis=-1, keepdims=True) + NORM_EPS) * w


def _rms_pre_fn(x, w):
    return (_rms(x, w),)


def _rms_pre_res_fn(x, w):
    return (_rms(x, w), x)


def _rms_post_fn(out, x, w):
    return (x + _rms(out, w),)


def _rms_only_fn(out, w):
    return (_rms(out, w),)


def _ln_gate_fn(u, z, w, b):
    u = u.astype(F32)
    uc = u - jnp.mean(u, axis=-1, keepdims=True)
    y = uc * lax.rsqrt(jnp.mean(uc * uc, axis=-1, keepdims=True) + NORM_EPS) * w + b
    return (_silu(y) * _silu(z.astype(F32)),)


def _merge_fn(g0, g1, g2, b0, b1, b2):
    return (_sigmoid(g0) * b0 + _sigmoid(g1) * b1 + _sigmoid(g2) * b2,)


def _matmul(name, a, b, mode, out_dtype, tm, tn, tk):
    if mode == "nn":
        (m, kc), n = a.shape, b.shape[1]
    elif mode == "nt":
        (m, kc), n = a.shape, b.shape[0]
    else:
        (kc, m), n = a.shape, b.shape[1]
    tm, tn, tk = min(tm, m), min(tn, n), min(tk, kc)
    nk = kc // tk
    assert m % tm == 0 and n % tn == 0 and kc % tk == 0, (name, a.shape, b.shape)
    dims = {"nn": NN, "nt": NT, "tn": TN}[mode]
    a_spec = (pl.BlockSpec((tk, tm), lambda j, i, k: (k, i)) if mode == "tn"
              else pl.BlockSpec((tm, tk), lambda j, i, k: (i, k)))
    b_spec = (pl.BlockSpec((tn, tk), lambda j, i, k: (j, k)) if mode == "nt"
              else pl.BlockSpec((tk, tn), lambda j, i, k: (k, j)))
    use_acc = nk > 1 and out_dtype != F32

    def body(a_ref, b_ref, o_ref, *acc):
        p = _dot(a_ref[...], b_ref[...], dims)
        if nk == 1:
            o_ref[...] = p.astype(out_dtype)
            return
        k = pl.program_id(2)
        dst = acc[0] if use_acc else o_ref

        @pl.when(k == 0)
        def _():
            dst[...] = p

        @pl.when(k > 0)
        def _():
            dst[...] += p

        if use_acc:
            @pl.when(k == nk - 1)
            def _():
                o_ref[...] = dst[...].astype(out_dtype)

    return pl.pallas_call(
        body, name=name, grid=(n // tn, m // tm, nk),
        in_specs=[a_spec, b_spec],
        out_specs=pl.BlockSpec((tm, tn), lambda j, i, k: (i, j)),
        out_shape=jax.ShapeDtypeStruct((m, n), out_dtype),
        scratch_shapes=[pltpu.VMEM((tm, tn), F32)] if use_acc else [],
        compiler_params=_params(("parallel", "parallel", "arbitrary")),
    )(a, b)


HALO = 32
CONV_TC = 256
CONV_T = 1024


def _conv_fwd(name, x, x_off, ch, w, b, k_width, gate_off=None):
    seq = x.shape[0]
    t_blk = min(CONV_T, seq)
    tc = CONV_TC
    hb = t_blk // HALO
    xcb = x_off // tc
    has_gate = gate_off is not None

    def body(*refs):
        if has_gate:
            xm_ref, xh_ref, gm_ref, gh_ref, w_ref, b_ref, y_ref, win = refs
        else:
            xm_ref, xh_ref, w_ref, b_ref, y_ref, win = refs
        t = pl.program_id(1)
        xm, xh = xm_ref[...], xh_ref[...]
        if has_gate:
            xm = xm * _sigmoid(gm_ref[...])
            xh = xh * _sigmoid(gh_ref[...])
        win[0:HALO, :] = jnp.where(t == 0, 0.0, xh)
        win[HALO:HALO + t_blk, :] = xm
        acc = jnp.broadcast_to(b_ref[...], (t_blk, tc))
        for k in range(k_width):
            acc = acc + w_ref[k:k + 1, :] * win[HALO - (k_width - 1) + k:HALO - (k_width - 1) + k + t_blk, :]
        y_ref[...] = acc

    main = lambda off: pl.BlockSpec((t_blk, tc), lambda c, t: (t, off + c))
    halo = lambda off: pl.BlockSpec((HALO, tc), lambda c, t: (jnp.maximum(t * hb - 1, 0), off + c))
    ins, specs = [x, x], [main(xcb), halo(xcb)]
    if has_gate:
        gcb = gate_off // tc
        ins += [x, x]
        specs += [main(gcb), halo(gcb)]
    ins += [w, b]
    specs += [pl.BlockSpec((w.shape[0], tc), lambda c, t: (0, c)), pl.BlockSpec((1, tc), lambda c, t: (0, c))]
    return pl.pallas_call(
        body, name=name, grid=(ch // tc, seq // t_blk), in_specs=specs,
        out_specs=pl.BlockSpec((t_blk, tc), lambda c, t: (t, c)),
        out_shape=jax.ShapeDtypeStruct((seq, ch), F32),
        scratch_shapes=[pltpu.VMEM((HALO + t_blk, tc), F32)],
        compiler_params=_params(("parallel", "arbitrary")),
    )(*ins)


def _conv_bwd(name, dy, x, x_off, ch, w, k_width, gate_off=None):
    seq = x.shape[0]
    t_blk = min(CONV_T, seq)
    tc = CONV_TC
    hb = t_blk // HALO
    nt = seq // t_blk
    xcb = x_off // tc
    has_gate = gate_off is not None
    kp = w.shape[0]

    def body(*refs):
        if has_gate:
            dm_ref, dh_ref, xm_ref, xh_ref, gm_ref, gh_ref, w_ref, dv_ref, dg_ref, dw_ref, db_ref, winx, wind = refs
        else:
            dm_ref, dh_ref, xm_ref, xh_ref, w_ref, dx_ref, dw_ref, db_ref, winx, wind = refs
        t = pl.program_id(1)
        xm, xh = xm_ref[...], xh_ref[...]
        if has_gate:
            sg = _sigmoid(gm_ref[...])
            um = xm * sg
            uh = xh * _sigmoid(gh_ref[...])
        else:
            um, uh = xm, xh
        winx[0:HALO, :] = jnp.where(t == nt - 1, 0.0, uh)
        winx[HALO:HALO + t_blk, :] = um
        dm = dm_ref[...]
        wind[0:t_blk, :] = dm
        wind[t_blk:t_blk + HALO, :] = jnp.where(t == 0, 0.0, dh_ref[...])
        du = jnp.zeros((t_blk, tc), F32)
        for k in range(k_width):
            du = du + w_ref[k:k + 1, :] * wind[k_width - 1 - k:k_width - 1 - k + t_blk, :]
        if has_gate:
            dv_ref[...] = (du * sg).astype(dv_ref.dtype)
            dg_ref[...] = (du * xm * sg * (1.0 - sg)).astype(dg_ref.dtype)
        else:
            dx_ref[...] = du.astype(dx_ref.dtype)

        @pl.when(t == 0)
        def _():
            dw_ref[...] = jnp.zeros_like(dw_ref)
            db_ref[...] = jnp.zeros_like(db_ref)

        for k in range(k_width):
            s0 = HALO - (k_width - 1) + k
            dw_ref[k:k + 1, :] += jnp.sum(dm * winx[s0:s0 + t_blk, :], axis=0, keepdims=True)
        db_ref[...] += jnp.sum(dm, axis=0, keepdims=True)

    rt = lambda t: nt - 1 - t
    main = lambda off: pl.BlockSpec((t_blk, tc), lambda c, t: (rt(t), off + c))
    past = lambda off: pl.BlockSpec((HALO, tc), lambda c, t: (jnp.maximum(rt(t) * hb - 1, 0), off + c))
    future = pl.BlockSpec((HALO, tc), lambda c, t: (jnp.minimum((rt(t) + 1) * hb, seq // HALO - 1), c))
    ins, specs = [dy, dy, x, x], [main(0), future, main(xcb), past(xcb)]
    if has_gate:
        gcb = gate_off // tc
        ins += [x, x]
        specs += [main(gcb), past(gcb)]
    ins += [w]
    specs += [pl.BlockSpec((kp, tc), lambda c, t: (0, c))]
    blk = pl.BlockSpec((t_blk, tc), lambda c, t: (rt(t), c))
    n_dx = 2 if has_gate else 1
    return pl.pallas_call(
        body, name=name, grid=(ch // tc, nt), in_specs=specs,
        out_specs=[blk] * n_dx + [pl.BlockSpec((kp, tc), lambda c, t: (0, c)), pl.BlockSpec((1, tc), lambda c, t: (0, c))],
        out_shape=[jax.ShapeDtypeStruct((seq, ch), BF16)] * n_dx + [jax.ShapeDtypeStruct((kp, ch), F32),
                                                                    jax.ShapeDtypeStruct((1, ch), F32)],
        scratch_shapes=[pltpu.VMEM((HALO + t_blk, tc), F32), pltpu.VMEM((HALO + t_blk, tc), F32)],
        compiler_params=_params(("parallel", "arbitrary")),
    )(*ins)


@jax.custom_vjp
def _inv_unit_lower(lows):
    n = lows[0].shape[0]
    eye = (lax.broadcasted_iota(jnp.int32, (n, n), 0) == lax.broadcasted_iota(jnp.int32, (n, n), 1)).astype(F32)
    accs = [eye - low for low in lows]
    pws = list(lows)
    steps = 1
    while steps * 2 < n:
        pws = [_dot_3x(pw, pw) for pw in pws]
        accs = [acc + _dot_3x(acc, pw) for acc, pw in zip(accs, pws)]
        steps *= 2
    return tuple(accs)


def _inv_fwd(lows):
    ts = _inv_unit_lower(lows)
    return ts, ts


def _inv_bwd(ts, dts):
    left = [_dot_3x(t, dt, TN) for t, dt in zip(ts, dts)]
    return (tuple(-_dot_3x(l, t, NT) for l, t in zip(left, ts)),)


_inv_unit_lower.defvjp(_inv_fwd, _inv_bwd)


def _gdr_chunk(cqkv, z, sm, alog, dtb, onw, state):
    c = cqkv.shape[0]
    hs = range(HEADS)
    ri = lax.broadcasted_iota(jnp.int32, (c, c), 0)
    ci = lax.broadcasted_iota(jnp.int32, (c, c), 1)
    incl, strict = ri >= ci, ri > ci
    beta_all = _sigmoid(sm)
    la_all = -jnp.exp(alog) * _softplus(sm + dtb)
    g_cols = _dot_hi(incl.astype(F32), la_all)
    g_rows = _dot_hi(la_all, (ri <= ci).astype(F32), TN)
    g_end = jnp.sum(la_all, axis=0, keepdims=True)
    act = _silu(cqkv)
    sl = lambda base, h: slice(base + h * HEAD_DIM, base + (h + 1) * HEAD_DIM)
    q = [act[:, sl(0, h)] for h in hs]
    k = [act[:, sl(BRANCH, h)] for h in hs]
    v = [act[:, sl(2 * BRANCH, h)] for h in hs]
    q = [x * lax.rsqrt(jnp.sum(x * x, axis=-1, keepdims=True) + NORM_EPS) * (HEAD_DIM ** -0.5) for x in q]
    k = [x * lax.rsqrt(jnp.sum(x * x, axis=-1, keepdims=True) + NORM_EPS) for x in k]
    beta = [_colsel(beta_all, BETA_LANE + h) for h in hs]
    g = [_colsel(g_cols, ALPHA_LANE + h) for h in hs]
    g_row = [_rowsel(g_rows, ALPHA_LANE + h) for h in hs]
    g_last = [_colsel(g_end, ALPHA_LANE + h) for h in hs]
    decay = [jnp.where(incl, jnp.exp(jnp.where(incl, g[h] - g_row[h], 0.0)), 0.0) for h in hs]
    kk = [_dot(k[h], k[h], NT) for h in hs]
    qk = [_dot(q[h], k[h], NT) * decay[h] for h in hs]
    t_inv = _inv_unit_lower(tuple(jnp.where(strict, beta[h] * kk[h] * decay[h], 0.0) for h in hs))
    eg = [jnp.exp(g[h]) for h in hs]
    u0 = [_dot(t_inv[h], v[h] * beta[h]) for h in hs]
    w_cum = [_dot(t_inv[h], k[h] * (beta[h] * eg[h])) for h in hs]
    s_in = [state[h] for h in hs]
    u = [u0[h] - _dot(w_cum[h], s_in[h]) for h in hs]
    o = [_dot(q[h] * eg[h], s_in[h]) + _dot(qk[h], u[h]) for h in hs]
    s_out = [s_in[h] * jnp.exp(g_last[h]) + _dot(k[h] * jnp.exp(g_last[h] - g[h]), u[h], TN) for h in hs]
    o = [x * lax.rsqrt(jnp.mean(x * x, axis=-1, keepdims=True) + NORM_EPS) * onw for x in o]
    y = [o[h] * _silu(z[:, sl(0, h)]) for h in hs]
    return jnp.concatenate(y, axis=1), jnp.concatenate([s[None] for s in s_out], axis=0)


def _gdr_specs(nc, order):
    return [
        pl.BlockSpec((CHUNK, 3 * BRANCH), lambda n: (order(n), 0)),
        pl.BlockSpec((CHUNK, BRANCH), lambda n: (order(n), OFF_ZA // BRANCH)),
        pl.BlockSpec((CHUNK, LANE), lambda n: (order(n), OFF_SMALL // LANE)),
        pl.BlockSpec((1, LANE), lambda n: (0, 0)),
        pl.BlockSpec((1, LANE), lambda n: (0, 0)),
        pl.BlockSpec((1, LANE), lambda n: (0, 0)),
    ]


def _gdr_fwd(name, cqkv, proj, alog, dtb, onw):
    seq = cqkv.shape[0]
    nc = seq // CHUNK

    def body(c_ref, z_ref, sm_ref, al_ref, dt_ref, on_ref, y_ref, st_ref, state):
        @pl.when(pl.program_id(0) == 0)
        def _():
            state[...] = jnp.zeros_like(state)

        s_in = state[...]
        st_ref[0] = s_in
        y, s_out = _gdr_chunk(c_ref[...], z_ref[...], sm_ref[...], al_ref[...], dt_ref[...], on_ref[...], s_in)
        y_ref[...] = y.astype(y_ref.dtype)
        state[...] = s_out

    return pl.pallas_call(
        body, name=name, grid=(nc,), in_specs=_gdr_specs(nc, lambda n: n),
        out_specs=[pl.BlockSpec((CHUNK, BRANCH), lambda n: (n, 0)),
                   pl.BlockSpec((1, HEADS, HEAD_DIM, HEAD_DIM), lambda n: (n, 0, 0, 0))],
        out_shape=[jax.ShapeDtypeStruct((seq, BRANCH), BF16),
                   jax.ShapeDtypeStruct((nc, HEADS, HEAD_DIM, HEAD_DIM), F32)],
        scratch_shapes=[pltpu.VMEM((HEADS, HEAD_DIM, HEAD_DIM), F32)],
        compiler_params=_params(("arbitrary",)),
    )(cqkv, proj, proj, alog, dtb, onw)


def _gdr_bwd(name, dy, states, cqkv, proj, alog, dtb, onw):
    seq = cqkv.shape[0]
    nc = seq // CHUNK
    rev = lambda n: nc - 1 - n

    def body(c_ref, z_ref, sm_ref, al_ref, dt_ref, on_ref, dy_ref, st_ref,
             dc_ref, dz_ref, dsm_ref, dal_ref, ddt_ref, don_ref, dstate):
        first = pl.program_id(0) == 0

        @pl.when(first)
        def _():
            dstate[...] = jnp.zeros_like(dstate)
            dal_ref[...] = jnp.zeros_like(dal_ref)
            ddt_ref[...] = jnp.zeros_like(ddt_ref)
            don_ref[...] = jnp.zeros_like(don_ref)

        _, vjp = jax.vjp(_gdr_chunk, c_ref[...], z_ref[...], sm_ref[...], al_ref[...], dt_ref[...], on_ref[...],
                         st_ref[0])
        dc, dz, dsm, dal, ddt, don, ds = vjp((dy_ref[...].astype(F32), dstate[...]))
        dc_ref[...] = dc
        dz_ref[...] = dz.astype(dz_ref.dtype)
        dsm_ref[...] = dsm
        dal_ref[...] += dal
        ddt_ref[...] += ddt
        don_ref[...] += don
        dstate[...] = ds

    small = pl.BlockSpec((1, LANE), lambda n: (0, 0))
    return pl.pallas_call(
        body, name=name, grid=(nc,),
        in_specs=_gdr_specs(nc, rev) + [pl.BlockSpec((CHUNK, BRANCH), lambda n: (rev(n), 0)),
                                        pl.BlockSpec((1, HEADS, HEAD_DIM, HEAD_DIM), lambda n: (rev(n), 0, 0, 0))],
        out_specs=[pl.BlockSpec((CHUNK, 3 * BRANCH), lambda n: (rev(n), 0)),
                   pl.BlockSpec((CHUNK, BRANCH), lambda n: (rev(n), 0)),
                   pl.BlockSpec((CHUNK, LANE), lambda n: (rev(n), 0)), small, small, small],
        out_shape=[jax.ShapeDtypeStruct((seq, 3 * BRANCH), F32), jax.ShapeDtypeStruct((seq, BRANCH), BF16),
                   jax.ShapeDtypeStruct((seq, LANE), F32)] + [jax.ShapeDtypeStruct((1, LANE), F32)] * 3,
        scratch_shapes=[pltpu.VMEM((HEADS, HEAD_DIM, HEAD_DIM), F32)],
        compiler_params=_params(("arbitrary",)),
    )(cqkv, proj, proj, alog, dtb, onw, dy, states)


GATE_T = 512
ATT_T = 1024


def _fox_gate_fwd(name, proj, fb):
    seq = proj.shape[0]
    tb = min(GATE_T, seq)

    def body(sm_ref, fb_ref, c_ref, ct_ref):
        tri = (lax.broadcasted_iota(jnp.int32, (tb, tb), 0) >= lax.broadcasted_iota(jnp.int32, (tb, tb), 1)).astype(F32)
        carry = jnp.zeros((1, LANE), F32)
        for i in range(seq // tb):
            lf = -_softplus(-(sm_ref[i * tb:(i + 1) * tb, :] + fb_ref[...]))
            cb = _dot_hi(tri, lf) + carry
            c_ref[i * tb:(i + 1) * tb, :] = cb
            ct_ref[:, i * tb:(i + 1) * tb] = cb.T
            carry = carry + jnp.sum(lf, axis=0, keepdims=True)

    return pl.pallas_call(
        body, name=name, grid=(1,),
        in_specs=[pl.BlockSpec((seq, LANE), lambda i: (0, OFF_SMALL // LANE)), pl.BlockSpec((1, LANE), lambda i: (0, 0))],
        out_specs=[pl.BlockSpec((seq, LANE), lambda i: (0, 0)), pl.BlockSpec((LANE, seq), lambda i: (0, 0))],
        out_shape=[jax.ShapeDtypeStruct((seq, LANE), F32), jax.ShapeDtypeStruct((LANE, seq), F32)],
        compiler_params=_params(("arbitrary",)),
    )(proj, fb)


def _fox_gate_bwd(name, dck, proj, fb):
    seq = proj.shape[0]
    tb = min(GATE_T, seq)
    nb = seq // tb

    def body(d_ref, sm_ref, fb_ref, o_ref, dfb_ref, pad):
        tri = (lax.broadcasted_iota(jnp.int32, (tb, tb), 0) >= lax.broadcasted_iota(jnp.int32, (tb, tb), 1)).astype(F32)
        pad[...] = jnp.zeros_like(pad)
        carry = jnp.zeros((HEADS, 1), F32)
        dfb = jnp.zeros((1, LANE), F32)
        for i in reversed(range(nb)):
            blk = d_ref[:, i * tb:(i + 1) * tb]
            pad[FORGET_LANE:FORGET_LANE + HEADS, :] = _dot_hi(blk, tri) + carry
            carry = carry + jnp.sum(blk, axis=1, keepdims=True)
            x = sm_ref[i * tb:(i + 1) * tb, :] + fb_ref[...]
            dsm = pad[...].T * _sigmoid(-x)
            o_ref[i * tb:(i + 1) * tb, :] = dsm
            dfb = dfb + jnp.sum(dsm, axis=0, keepdims=True)
        dfb_ref[...] = dfb

    return pl.pallas_call(
        body, name=name, grid=(1,),
        in_specs=[pl.BlockSpec((HEADS, seq), lambda i: (0, 0)), pl.BlockSpec((seq, LANE), lambda i: (0, OFF_SMALL // LANE)),
                  pl.BlockSpec((1, LANE), lambda i: (0, 0))],
        out_specs=[pl.BlockSpec((seq, LANE), lambda i: (0, 0)), pl.BlockSpec((1, LANE), lambda i: (0, 0))],
        out_shape=[jax.ShapeDtypeStruct((seq, LANE), F32), jax.ShapeDtypeStruct((1, LANE), F32)],
        scratch_shapes=[pltpu.VMEM((LANE, tb), F32)],
        compiler_params=_params(("arbitrary",)),
    )(dck, proj, fb)


def _att_scores(q, k, cq, ck_row, diagonal):
    s = _dot(q, k, NT) * (HEAD_DIM ** -0.5) + (cq - ck_row)
    if diagonal:
        keep = lax.broadcasted_iota(jnp.int32, s.shape, 0) >= lax.broadcasted_iota(jnp.int32, s.shape, 1)
        s = jnp.where(keep, s, -jnp.inf)
    return s


def _causal_pairs(nq, key_major):
    pairs = ([(i, j) for j in range(nq) for i in range(j, nq)] if key_major
             else [(i, j) for i in range(nq) for j in range(i + 1)])
    return jnp.asarray([p[0] for p in pairs], jnp.int32), jnp.asarray([p[1] for p in pairs], jnp.int32)


def _attn_specs(tq):
    qs = lambda off: pl.BlockSpec((tq, HEAD_DIM), lambda h, t, it, jt: (it[t], off + h))
    kv = lambda off: pl.BlockSpec((tq, HEAD_DIM), lambda h, t, it, jt: (jt[t], off + h))
    c_spec = pl.BlockSpec((tq, LANE), lambda h, t, it, jt: (it[t], 0))
    ct_spec = pl.BlockSpec((1, 1, tq), lambda h, t, it, jt: (FORGET_LANE + h, 0, jt[t]))
    lse_spec = pl.BlockSpec((1, tq, LANE), lambda h, t, it, jt: (h, it[t], 0))
    return qs, kv, c_spec, ct_spec, lse_spec


def _ride_along(comm, refs, n_in, n_out, n_scratch, first, last):
    n_c = len(comm[1]) if comm else 0
    ins, c_in = refs[:n_in], refs[n_in:n_in + n_c]
    outs, c_out = refs[n_in + n_c:n_in + n_c + n_out], refs[n_in + n_c + n_out:n_in + 2 * n_c + n_out]
    scratch = refs[n_in + 2 * n_c + n_out:n_in + 2 * n_c + n_out + n_scratch]
    sems = refs[n_in + 2 * n_c + n_out + n_scratch:]
    if not comm:
        return ins, outs, scratch, lambda: None
    start, finish = _comm_ops(comm[0], c_in, c_out, sems)
    pl.when(first)(start)
    return ins, outs, scratch, lambda: pl.when(last)(finish)


def _attn_fwd(name, proj, c, ct3, comm=None):
    seq = proj.shape[0]
    tq = min(ATT_T, seq)
    nq = seq // tq
    qb, zb = OFF_QKVC // HEAD_DIM, OFF_ZC // HEAD_DIM
    i_tab, j_tab = _causal_pairs(nq, False)
    n_pairs = i_tab.shape[0]
    c_arrs = list(comm[1]) if comm else []
    c_shapes, c_sems = _comm_plan(comm[0], c_arrs) if comm else ([], [])

    def body(it, jt, *refs):
        h, t = pl.program_id(0), pl.program_id(1)
        i, j = it[t], jt[t]
        ins, outs, scratch, finish = _ride_along(comm, refs, 6, 3, 3, (h == 0) & (t == 0),
                                                 (h == HEADS - 1) & (t == n_pairs - 1))
        q_ref, k_ref, v_ref, z_ref, c_ref, ct_ref = ins
        y_ref, o_ref, lse_ref = outs
        m_s, l_s, acc_s = scratch

        @pl.when(j == 0)
        def _():
            m_s[...] = jnp.full_like(m_s, -jnp.inf)
            l_s[...] = jnp.zeros_like(l_s)
            acc_s[...] = jnp.zeros_like(acc_s)

        def step(diagonal):
            cq = _colsel(c_ref[...], FORGET_LANE + h)
            s = _att_scores(q_ref[...], k_ref[...], cq, ct_ref[0], diagonal)
            m_old = m_s[...]
            m_new = jnp.maximum(m_old, jnp.max(s, axis=1, keepdims=True))
            p = jnp.exp(s - m_new)
            alpha = jnp.exp(m_old - m_new)
            l_s[...] = alpha * l_s[...] + jnp.sum(p, axis=1, keepdims=True)
            p_hi = p.astype(MXU_DTYPE).astype(F32)
            acc_s[...] = alpha * acc_s[...] + _dot(p_hi, v_ref[...]) + _dot(p - p_hi, v_ref[...])
            m_s[...] = m_new

        @pl.when(j < i)
        def _():
            step(False)

        @pl.when(j == i)
        def _():
            step(True)
            o = acc_s[...] / l_s[...]
            o_ref[...] = o
            y_ref[...] = (o * _silu(z_ref[...])).astype(y_ref.dtype)
            lse_ref[0] = jnp.broadcast_to(m_s[...] + jnp.log(l_s[...]), (tq, LANE))

        finish()

    qs, kv, c_spec, ct_spec, lse_spec = _attn_specs(tq)
    res = pl.pallas_call(
        body, name=name,
        grid_spec=pltpu.PrefetchScalarGridSpec(
            num_scalar_prefetch=2, grid=(HEADS, n_pairs),
            in_specs=[qs(qb), kv(qb + HEADS), kv(qb + 2 * HEADS), qs(zb), c_spec, ct_spec] + [HBM_SPEC] * len(c_arrs),
            out_specs=[qs(0), qs(0), lse_spec] + [HBM_SPEC] * len(c_arrs),
            scratch_shapes=[pltpu.VMEM((tq, 1), F32), pltpu.VMEM((tq, 1), F32), pltpu.VMEM((tq, HEAD_DIM), F32)]
            + c_sems),
        out_shape=[jax.ShapeDtypeStruct((seq, BRANCH), BF16), jax.ShapeDtypeStruct((seq, BRANCH), F32),
                   jax.ShapeDtypeStruct((HEADS, seq, LANE), F32)] + c_shapes,
        compiler_params=_params(("arbitrary", "arbitrary")),
    )(i_tab, j_tab, proj, proj, proj, proj, c, ct3, *c_arrs)
    return res[0], res[1], res[2], list(res[3:])


def _attn_dq(name, dy, o, lse, proj, c, ct3):
    seq = proj.shape[0]
    tq = min(ATT_T, seq)
    nq = seq // tq
    qb, zb = OFF_QKVC // HEAD_DIM, OFF_ZC // HEAD_DIM
    i_tab, j_tab = _causal_pairs(nq, False)

    def body(it, jt, q_ref, k_ref, v_ref, z_ref, c_ref, ct_ref, dy_ref, o_ref, lse_ref, dq_ref, dz_ref, do_s, dl_s, acc_s):
        h, t = pl.program_id(0), pl.program_id(1)
        i, j = it[t], jt[t]

        @pl.when(j == 0)
        def _():
            z = z_ref[...]
            sg = _sigmoid(z)
            dyv = dy_ref[...].astype(F32)
            do = dyv * z * sg
            do_s[...] = do
            dl_s[...] = jnp.sum(do.astype(MXU_DTYPE).astype(F32) * o_ref[...], axis=1, keepdims=True)
            dz_ref[...] = (dyv * o_ref[...] * sg * (1.0 + z * (1.0 - sg))).astype(dz_ref.dtype)
            acc_s[...] = jnp.zeros_like(acc_s)

        def step(diagonal):
            cq = _colsel(c_ref[...], FORGET_LANE + h)
            s = _att_scores(q_ref[...], k_ref[...], cq, ct_ref[0], diagonal)
            p = jnp.exp(s - jnp.max(lse_ref[0], axis=1, keepdims=True))
            dp = _dot(do_s[...], v_ref[...], NT)
            ds = p * (dp - dl_s[...])
            acc_s[...] += _dot(ds, k_ref[...])

        @pl.when(j < i)
        def _():
            step(False)

        @pl.when(j == i)
        def _():
            step(True)
            dq_ref[...] = (acc_s[...] * (HEAD_DIM ** -0.5)).astype(dq_ref.dtype)

    qs, kv, c_spec, ct_spec, lse_spec = _attn_specs(tq)
    return pl.pallas_call(
        body, name=name,
        grid_spec=pltpu.PrefetchScalarGridSpec(
            num_scalar_prefetch=2, grid=(HEADS, i_tab.shape[0]),
            in_specs=[qs(qb), kv(qb + HEADS), kv(qb + 2 * HEADS), qs(zb), c_spec, ct_spec, qs(0), qs(0), lse_spec],
            out_specs=[qs(0), qs(0)],
            scratch_shapes=[pltpu.VMEM((tq, HEAD_DIM), F32), pltpu.VMEM((tq, 1), F32), pltpu.VMEM((tq, HEAD_DIM), F32)]),
        out_shape=[jax.ShapeDtypeStruct((seq, BRANCH), BF16)] * 2,
        compiler_params=_params(("parallel", "arbitrary")),
    )(i_tab, j_tab, proj, proj, proj, proj, c, ct3, dy, o, lse)


def _attn_dkv(name, dy, o, lse, proj, c, ct3, comm=None):
    seq = proj.shape[0]
    tq = min(ATT_T, seq)
    nq = seq // tq
    qb, zb = OFF_QKVC // HEAD_DIM, OFF_ZC // HEAD_DIM
    i_tab, j_tab = _causal_pairs(nq, True)
    n_pairs = i_tab.shape[0]
    c_arrs = list(comm[1]) if comm else []
    c_shapes, c_sems = _comm_plan(comm[0], c_arrs) if comm else ([], [])

    def body(it, jt, *refs):
        h, t = pl.program_id(0), pl.program_id(1)
        i, j = it[t], jt[t]
        ins, outs, scratch, finish = _ride_along(comm, refs, 9, 3, 3, (h == 0) & (t == 0),
                                                 (h == HEADS - 1) & (t == n_pairs - 1))
        q_ref, k_ref, v_ref, z_ref, c_ref, ct_ref, dy_ref, o_ref, lse_ref = ins
        dk_ref, dv_ref, dc_ref = outs
        dk_s, dv_s, dc_s = scratch

        @pl.when(i == j)
        def _():
            dk_s[...] = jnp.zeros_like(dk_s)
            dv_s[...] = jnp.zeros_like(dv_s)
            dc_s[...] = jnp.zeros_like(dc_s)

        def step(diagonal):
            z = z_ref[...]
            do = dy_ref[...].astype(F32) * _silu(z)
            delta = jnp.sum(do.astype(MXU_DTYPE).astype(F32) * o_ref[...], axis=1, keepdims=True)
            cq = _colsel(c_ref[...], FORGET_LANE + h)
            s = _att_scores(q_ref[...], k_ref[...], cq, ct_ref[0], diagonal)
            p = jnp.exp(s - jnp.max(lse_ref[0], axis=1, keepdims=True))
            dv_s[...] += _dot(p, do, TN)
            ds = p * (_dot(do, v_ref[...], NT) - delta)
            dk_s[...] += _dot(ds, q_ref[...], TN)
            dc_s[...] -= jnp.sum(ds, axis=0, keepdims=True)

        @pl.when(i == j)
        def _():
            step(True)

        @pl.when(i > j)
        def _():
            step(False)

        @pl.when(i == nq - 1)
        def _():
            dk_ref[...] = (dk_s[...] * (HEAD_DIM ** -0.5)).astype(dk_ref.dtype)
            dv_ref[...] = dv_s[...].astype(dv_ref.dtype)
            dc_ref[0] = dc_s[...]

        finish()

    qs, kv, c_spec, ct_spec, lse_spec = _attn_specs(tq)
    res = pl.pallas_call(
        body, name=name,
        grid_spec=pltpu.PrefetchScalarGridSpec(
            num_scalar_prefetch=2, grid=(HEADS, n_pairs),
            in_specs=[qs(qb), kv(qb + HEADS), kv(qb + 2 * HEADS), qs(zb), c_spec, ct_spec, qs(0), qs(0), lse_spec]
            + [HBM_SPEC] * len(c_arrs),
            out_specs=[kv(0), kv(0), pl.BlockSpec((1, 1, tq), lambda h, t, it, jt: (h, 0, jt[t]))]
            + [HBM_SPEC] * len(c_arrs),
            scratch_shapes=[pltpu.VMEM((tq, HEAD_DIM), F32), pltpu.VMEM((tq, HEAD_DIM), F32), pltpu.VMEM((1, tq), F32)]
            + c_sems),
        out_shape=[jax.ShapeDtypeStruct((seq, BRANCH), BF16)] * 2 + [jax.ShapeDtypeStruct((HEADS, 1, seq), F32)]
        + c_shapes,
        compiler_params=_params(("arbitrary", "arbitrary")),
    )(i_tab, j_tab, proj, proj, proj, proj, c, ct3, dy, o, lse, *c_arrs)
    return res[0], res[1], res[2], list(res[3:])


def _loss_head(name, y, target):
    seq, d = y.shape
    tile = min(256, seq)

    def body(y_ref, t_ref, dy_ref, l_ref):
        err = y_ref[...] - t_ref[...]
        dy_ref[...] = err / d

        @pl.when(pl.program_id(0) == 0)
        def _():
            l_ref[...] = jnp.zeros_like(l_ref)

        l_ref[...] += 0.5 * jnp.sum(jnp.mean(err * err, axis=-1, keepdims=True), axis=0, keepdims=True)

    return pl.pallas_call(
        body, name=name, grid=(seq // tile,),
        in_specs=[pl.BlockSpec((tile, d), lambda i: (i, 0))] * 2,
        out_specs=[pl.BlockSpec((tile, d), lambda i: (i, 0)), pl.BlockSpec((8, LANE), lambda i: (0, 0))],
        out_shape=[jax.ShapeDtypeStruct((seq, d), F32), jax.ShapeDtypeStruct((8, LANE), F32)],
        compiler_params=_params(("arbitrary",)),
    )(y, target)


def _adamw(name, w, m, v, g_parts, tile):
    rows, cols = w.shape
    n_g = len(g_parts)
    part_rows = rows // n_g
    tile = min(tile, part_rows)
    assert part_rows % tile == 0 and all(p.shape == (part_rows, cols) for p in g_parts), (name, w.shape)
    nb = part_rows // tile

    def body(*refs):
        w_ref, m_ref, v_ref = refs[:3]
        g_refs = refs[3:3 + n_g]
        g_ref, d_ref, nm_ref, nv_ref = refs[3 + n_g:]
        part = pl.program_id(0)
        g = g_refs[0][...]
        for k in range(1, n_g):
            g = jnp.where(part == k, g_refs[k][...], g)
        m_new = ADAM_B1 * m_ref[...] + (1.0 - ADAM_B1) * g
        v_new = ADAM_B2 * v_ref[...] + (1.0 - ADAM_B2) * (g * g)
        m_hat = m_new / (1.0 - ADAM_B1 ** ADAM_STEP)
        v_hat = v_new / (1.0 - ADAM_B2 ** ADAM_STEP)
        g_ref[...] = g
        d_ref[...] = -ADAM_LR * (m_hat / (jnp.sqrt(v_hat) + ADAM_EPS) + ADAM_WD * w_ref[...])
        nm_ref[...] = m_new
        nv_ref[...] = v_new

    blk = pl.BlockSpec((tile, cols), lambda p, i: (p * nb + i, 0))
    g_specs = [pl.BlockSpec((tile, cols), functools.partial(lambda p, i, k: (jnp.where(p == k, i, 0), 0), k=k))
               for k in range(n_g)]
    return pl.pallas_call(
        body, name=name, grid=(n_g, nb), in_specs=[blk] * 3 + g_specs, out_specs=[blk] * 4,
        out_shape=[jax.ShapeDtypeStruct((rows, cols), F32)] * 4,
        compiler_params=_params(("arbitrary", "arbitrary")),
    )(w, m, v, *g_parts)


_ORIG_SEGMENTS = (
    ("qkv_a", 0, 3072), ("z_a", 3072, 1024), ("beta", 4096, 8), ("alpha", 4104, 8), ("glu", 4112, 2048),
    ("z_b", 6160, 1024), ("qkv_c", 7184, 3072), ("z_c", 10256, 1024), ("forget", 11280, 8), ("gate", 11288, 6144))
_PAD_ORDER = ("gate", "qkv_a", "z_a", "glu", "z_b", "qkv_c", "z_c", "beta", "alpha", "forget")


def _pad_cols(w):
    seg = {n: w[..., s:s + k] for n, s, k in _ORIG_SEGMENTS}
    fill = jnp.zeros(w.shape[:-1] + (N_PAD - N_IN,), w.dtype)
    return jnp.concatenate([seg[n] for n in _PAD_ORDER] + [fill], axis=-1)


def _unpad_cols(g):
    off, seg = 0, {}
    widths = {n: k for n, _, k in _ORIG_SEGMENTS}
    for n in _PAD_ORDER:
        seg[n] = g[..., off:off + widths[n]]
        off += widths[n]
    return jnp.concatenate([seg[n] for n, _, _ in _ORIG_SEGMENTS], axis=-1)


def _lane_row(vals, lane0):
    return jnp.pad(vals.astype(F32), (lane0, LANE - HEADS - lane0))[None]


def _layer_fwd(x, p, comm=None):
    seq = x.shape[0]
    h = _rowwise("rms_pre", _rms_pre_fn, [(x, D_MODEL, 0)], [p["pre_w"]], [(D_MODEL, BF16)], 256)[0]
    proj = _matmul("mm_in", h, p["w_in"], "nn", F32, 512, 1280, 2048)
    ca = _conv_fwd("conv_a", proj, OFF_QKVA, 3 * BRANCH, p["w4"], jnp.zeros((1, 3 * BRANCH), F32), SHORT_CONV)
    y_a, states = _gdr_fwd("gdr_fwd", ca, proj, p["alog"], p["dtb"], p["onw"])
    u2 = _conv_fwd("conv_b", proj, OFF_VAL, BRANCH, p["w31"], p["cb"], CONF_CONV, gate_off=OFF_GLUG)
    y_b = _rowwise("ln_gate", _ln_gate_fn, [(u2, BRANCH, 0), (proj, BRANCH, OFF_ZB // BRANCH)],
                   [p["ln_w"], p["ln_b"]], [(BRANCH, BF16)], 256)[0]
    c, ct = _fox_gate_fwd("fox_gate", proj, p["fb"])
    ct3 = ct.reshape(LANE, 1, seq)
    y_c, o_c, lse, got = _attn_fwd("attn_fwd", proj, c, ct3, comm)
    ys = (y_a, y_b, y_c)
    br = [_matmul("mm_br", ys[n], p["wbr"][n], "nn", F32, 512, 2048, 1024) for n in range(N_BRANCH)]
    merged = _rowwise("merge", _merge_fn, [(proj, D_MODEL, n) for n in range(N_BRANCH)] + [(b, D_MODEL, 0) for b in br],
                      [], [(D_MODEL, BF16)], 256)[0]
    out = _matmul("mm_out", merged, p["wout"], "nn", F32, 512, 2048, 2048)
    x_new = _rowwise("rms_post", _rms_post_fn, [(out, D_MODEL, 0), (x, D_MODEL, 0)], [p["post_w"]],
                     [(D_MODEL, F32)], 256)[0]
    saved = dict(x=x, ht=h.T, proj=proj, ca=ca, states=states, u2=u2, c=c, ct3=ct3, o_c=o_c, lse=lse, ys=ys, br=br,
                 merged=merged, out=out)
    return x_new, saved, got


def _layer_bwd(dxn, p, sv, comm=None):
    x, proj = sv["x"], sv["proj"]
    seq = x.shape[0]
    g = {}
    d_out, g["post_w"] = _rowwise_bwd("rms_post_bwd", _rms_only_fn, [(sv["out"], D_MODEL, 0)], [p["post_w"]], [dxn],
                                      [BF16], 256)
    d_merged = _matmul("mm_out_dx", d_out, p["wout"], "nt", F32, 512, 2048, 2048)
    g["wout"] = _matmul("mm_out_dw", sv["merged"], d_out, "tn", F32, 1024, 1024, 512)
    rows = [(proj, D_MODEL, n) for n in range(N_BRANCH)] + [(b, D_MODEL, 0) for b in sv["br"]]
    d_gl0, d_gl1, d_gl2, d_b0, d_b1, d_b2 = _rowwise_bwd("merge_bwd", _merge_fn, rows, [], [d_merged], [BF16] * 6, 128)
    d_br = (d_b0, d_b1, d_b2)
    dys = [_matmul("mm_br_dx", d_br[n], p["wbr"][n], "nt", BF16, 512, 1024, 2048) for n in range(N_BRANCH)]
    g["wbr"] = jnp.stack([_matmul("mm_br_dw", sv["ys"][n], d_br[n], "tn", F32, 1024, 1024, 512)
                          for n in range(N_BRANCH)])
    dq, dzc = _attn_dq("attn_dq", dys[2], sv["o_c"], sv["lse"], proj, sv["c"], sv["ct3"])
    dk, dv, dck, got = _attn_dkv("attn_dkv", dys[2], sv["o_c"], sv["lse"], proj, sv["c"], sv["ct3"], comm)
    dsm_c, g["fb"] = _fox_gate_bwd("fox_gate_bwd", dck.reshape(HEADS, seq), proj, p["fb"])
    du2, dzb, g["ln_w"], g["ln_b"] = _rowwise_bwd(
        "ln_gate_bwd", _ln_gate_fn, [(sv["u2"], BRANCH, 0), (proj, BRANCH, OFF_ZB // BRANCH)], [p["ln_w"], p["ln_b"]],
        [dys[1]], [F32, BF16], 256)
    dval, dgate, g["w31"], g["cb"] = _conv_bwd("conv_b_bwd", du2, proj, OFF_VAL, BRANCH, p["w31"], CONF_CONV,
                                               gate_off=OFF_GLUG)
    dca, dza, dsm_a, g["alog"], g["dtb"], g["onw"] = _gdr_bwd("gdr_bwd", dys[0], sv["states"], sv["ca"], proj,
                                                              p["alog"], p["dtb"], p["onw"])
    dqkva, g["w4"], _ = _conv_bwd("conv_a_bwd", dca, proj, OFF_QKVA, 3 * BRANCH, p["w4"], SHORT_CONV)
    d_small = jnp.pad((dsm_a + dsm_c).astype(BF16), ((0, 0), (0, N_PAD - OFF_SMALL - LANE)))
    d_proj = jnp.concatenate([d_gl0, d_gl1, d_gl2, dqkva, dza, dval, dgate, dzb, dq, dk, dv, dzc, d_small], axis=1)
    dh = _matmul("mm_in_dx", d_proj, p["w_in"], "nt", F32, 512, 2048, 2560)
    g["w_in"] = _matmul("mm_in_dw", sv["ht"], d_proj, "nn", F32, 2048, 640, 2048)
    dx, g["pre_w"] = _rowwise_bwd("rms_pre_bwd", _rms_pre_res_fn, [(x, D_MODEL, 0)], [p["pre_w"]], [dh, dxn], [F32], 256)
    return dx, g, got


N_CHIPS = 4
N_DEV = 8
HBM_SPEC = pl.BlockSpec(memory_space=pltpu.HBM)


def _mesh_pos():
    return lax.axis_index("x"), lax.axis_index("y"), lax.axis_index("c")


def _other_chips(x, y):
    return [(1 - x, y), (x, 1 - y), (1 - x, 1 - y)]


def _comm_plan(kind, arrs):
    n = len(arrs)
    if kind == "allgather":
        return ([jax.ShapeDtypeStruct((3,) + a.shape, a.dtype) for a in arrs], [pltpu.SemaphoreType.DMA((3 * n,))] * 4)
    return ([jax.ShapeDtypeStruct((3,) + a.shape[1:], a.dtype) for a in arrs], [pltpu.SemaphoreType.DMA((3 * n,))] * 2)


def _comm_ops(kind, ins, outs, sems):
    return (_allgather_ops if kind == "allgather" else _reduce_scatter_ops)(ins, outs, sems)


def _allgather_ops(ins, outs, sems):
    send_sems, recv_sems, pass_send, pass_recv = sems
    n = len(ins)
    x, y, c = _mesh_pos()

    def part(a, core):
        half = ins[a].shape[0] // 2
        return pl.ds(half * core, half)

    def ici(a, j):
        px, py = _other_chips(x, y)[j]
        return pltpu.make_async_remote_copy(
            src_ref=ins[a].at[part(a, c)], dst_ref=outs[a].at[j, part(a, c)], send_sem=send_sems.at[3 * a + j],
            recv_sem=recv_sems.at[3 * a + j], device_id=(px, py, c), device_id_type=MESH)

    def d2d(a, j, core):
        blk = outs[a].at[j, part(a, core)]
        return pltpu.make_async_remote_copy(
            src_ref=blk, dst_ref=blk, send_sem=pass_send.at[3 * a + j], recv_sem=pass_recv.at[3 * a + j],
            device_id=(x, y, 1 - c), device_id_type=MESH)

    def start():
        for a in range(n):
            for j in range(3):
                ici(a, j).start()

    def finish():
        for a in range(n):
            for j in range(3):
                ici(a, j).wait_recv()
                d2d(a, j, c).start()
        for a in range(n):
            for j in range(3):
                d2d(a, j, 1 - c).wait_recv()
        for a in range(n):
            for j in range(3):
                ici(a, j).wait_send()
                d2d(a, j, c).wait_send()

    return start, finish


def _reduce_scatter_ops(ins, outs, sems):
    send_sems, recv_sems = sems
    n = len(ins)
    x, y, c = _mesh_pos()

    def remote(a, j):
        px, py = _other_chips(x, y)[j]
        return pltpu.make_async_remote_copy(
            src_ref=ins[a].at[2 * px + py], dst_ref=outs[a].at[j], send_sem=send_sems.at[3 * a + j],
            recv_sem=recv_sems.at[3 * a + j], device_id=(px, py, c), device_id_type=MESH)

    def start():
        for a in range(n):
            for j in range(3):
                remote(a, j).start()

    def finish():
        for a in range(n):
            for j in range(3):
                remote(a, j).wait_recv()
        for a in range(n):
            for j in range(3):
                remote(a, j).wait_send()

    return start, finish


def _exchange_chips(name, kind, arrs):
    n = len(arrs)
    shapes, sems = _comm_plan(kind, arrs)

    def body(*refs):
        start, finish = _comm_ops(kind, refs[:n], refs[n:2 * n], refs[2 * n:])
        start()
        finish()

    return pl.pallas_call(body, name=name, in_specs=[HBM_SPEC] * n, out_specs=[HBM_SPEC] * n, out_shape=shapes,
                          scratch_shapes=sems)(*arrs)


def _halves_to_sibling(name, arrs):
    n = len(arrs)

    def body(*refs):
        ins, outs = refs[:n], refs[n:2 * n]
        send_sems, recv_sems = refs[2 * n:]
        x, y, c = _mesh_pos()
        copies = []
        for a in range(n):
            rows = ins[a].shape[1] // 2
            for s in range(N_CHIPS):
                copies.append(pltpu.make_async_remote_copy(
                    src_ref=ins[a].at[s, pl.ds((1 - c) * rows, rows)], dst_ref=outs[a].at[s],
                    send_sem=send_sems.at[N_CHIPS * a + s], recv_sem=recv_sems.at[N_CHIPS * a + s],
                    device_id=(x, y, 1 - c), device_id_type=MESH))
        for cp in copies:
            cp.start()
        for cp in copies:
            cp.wait_recv()
        for cp in copies:
            cp.wait_send()

    return pl.pallas_call(
        body, name=name, in_specs=[HBM_SPEC] * n, out_specs=[HBM_SPEC] * n,
        out_shape=[jax.ShapeDtypeStruct((a.shape[0], a.shape[1] // 2, a.shape[2]), a.dtype) for a in arrs],
        scratch_shapes=[pltpu.SemaphoreType.DMA((N_CHIPS * n,)), pltpu.SemaphoreType.DMA((N_CHIPS * n,))],
    )(*arrs)


def _add_own_half(name, full, other, core, tile):
    n, rows, cols = other.shape
    tile = min(tile, rows)
    assert rows % tile == 0, (name, other.shape)
    nb = rows // tile

    def body(c_ref, f_ref, o_ref, out_ref):
        out_ref[...] = f_ref[...] + o_ref[...]

    return pl.pallas_call(
        body, name=name,
        grid_spec=pltpu.PrefetchScalarGridSpec(
            num_scalar_prefetch=1, grid=(n, nb),
            in_specs=[pl.BlockSpec((1, tile, cols), lambda s, i, c_ref: (s, c_ref[0] * nb + i, 0)),
                      pl.BlockSpec((1, tile, cols), lambda s, i, c_ref: (s, i, 0))],
            out_specs=pl.BlockSpec((1, tile, cols), lambda s, i, c_ref: (s, i, 0))),
        out_shape=jax.ShapeDtypeStruct(other.shape, F32),
        compiler_params=_params(("parallel", "parallel")),
    )(core, full, other)


def _join_cores(name, arrs):
    n = len(arrs)

    def body(*refs):
        bufs = refs[n:2 * n]
        send_sems, recv_sems = refs[2 * n:]
        x, y, c = _mesh_pos()

        def copy(a, slot):
            return pltpu.make_async_remote_copy(
                src_ref=bufs[a].at[slot], dst_ref=bufs[a].at[slot], send_sem=send_sems.at[a], recv_sem=recv_sems.at[a],
                device_id=(x, y, 1 - c), device_id_type=MESH)

        for a in range(n):
            copy(a, c).start()
        for a in range(n):
            copy(a, 1 - c).wait_recv()
        for a in range(n):
            copy(a, c).wait_send()

    return pl.pallas_call(
        body, name=name, in_specs=[HBM_SPEC] * n, out_specs=[HBM_SPEC] * n,
        out_shape=[jax.ShapeDtypeStruct(a.shape, a.dtype) for a in arrs],
        input_output_aliases={a: a for a in range(n)},
        scratch_shapes=[pltpu.SemaphoreType.DMA((n,)), pltpu.SemaphoreType.DMA((n,))],
    )(*arrs)


def _sum_chips(name, own, recv, chip, core, tile):
    _, rows, cols = recv.shape
    tile = min(tile, rows)
    assert rows % tile == 0, (name, recv.shape)

    def body(chip_ref, core_ref, own_ref, recv_ref, o_ref):
        o_ref[0] = ((own_ref[0] + recv_ref[0]) + recv_ref[1]) + recv_ref[2]

    return pl.pallas_call(
        body, name=name,
        grid_spec=pltpu.PrefetchScalarGridSpec(
            num_scalar_prefetch=2, grid=(rows // tile,),
            in_specs=[pl.BlockSpec((1, tile, cols), lambda i, chip_ref, core_ref: (chip_ref[0], i, 0)),
                      pl.BlockSpec((3, tile, cols), lambda i, chip_ref, core_ref: (0, i, 0))],
            out_specs=pl.BlockSpec((1, tile, cols), lambda i, chip_ref, core_ref: (core_ref[0], i, 0))),
        out_shape=jax.ShapeDtypeStruct((2, rows, cols), F32),
        compiler_params=_params(("parallel",)),
    )(chip, core, own, recv)


def _allgather_devices(name, buf):
    def body(in_ref, out_ref, send_sems, recv_sems):
        x, y, c = _mesh_pos()
        me = 4 * x + 2 * y + c
        out_ref[me] = in_ref[...]

        def remote(k, slot):
            peer = (x ^ (k >> 2), y ^ ((k >> 1) & 1), c ^ (k & 1))
            return pltpu.make_async_remote_copy(
                src_ref=in_ref, dst_ref=out_ref.at[slot], send_sem=send_sems.at[k - 1], recv_sem=recv_sems.at[k - 1],
                device_id=peer, device_id_type=MESH)

        for k in range(1, N_DEV):
            remote(k, me).start()
        for k in range(1, N_DEV):
            remote(k, me ^ k).wait_recv()
        for k in range(1, N_DEV):
            remote(k, me).wait_send()

    vmem = pl.BlockSpec(memory_space=pltpu.VMEM)
    return pl.pallas_call(
        body, name=name, in_specs=[vmem], out_specs=vmem,
        out_shape=jax.ShapeDtypeStruct((N_DEV,) + buf.shape, buf.dtype),
        scratch_shapes=[pltpu.SemaphoreType.DMA((N_DEV - 1,)), pltpu.SemaphoreType.DMA((N_DEV - 1,))],
    )(buf)


def _sum_slots(name, a, tile):
    n, rows, cols = a.shape
    tile = min(tile, rows)
    assert rows % tile == 0, (name, a.shape)

    def body(a_ref, o_ref):
        acc = a_ref[0].astype(F32)
        for i in range(1, n):
            acc = acc + a_ref[i].astype(F32)
        o_ref[...] = acc

    return pl.pallas_call(
        body, name=name, grid=(rows // tile,),
        in_specs=[pl.BlockSpec((n, tile, cols), lambda i: (0, i, 0))],
        out_specs=pl.BlockSpec((tile, cols), lambda i: (i, 0)),
        out_shape=jax.ShapeDtypeStruct((rows, cols), F32),
        compiler_params=_params(("parallel",)),
    )(a)


_SMALL = (
    ("pre_norm_w", (DEPTH, D_MODEL)), ("post_norm_w", (DEPTH, D_MODEL)), ("a_log", (DEPTH, HEADS)),
    ("dt_bias", (DEPTH, HEADS)), ("o_norm_w", (DEPTH, HEAD_DIM)), ("conv_b", (DEPTH, BRANCH)), ("ln_w", (DEPTH, BRANCH)),
    ("ln_b", (DEPTH, BRANCH)), ("f_bias", (DEPTH, HEADS)), ("conv_qkv_w", (DEPTH, SHORT_CONV, 3 * BRANCH)),
    ("conv_w", (DEPTH, CONF_CONV, BRANCH)), ("loss", (1,)))
_SHARDED_SMALL = {"conv_qkv_w": 3 * BRANCH // N_CHIPS, "conv_w": BRANCH // N_CHIPS}
_WEIGHTS = ("pre_norm_w", "post_norm_w", "w_in", "conv_qkv_w", "a_log", "dt_bias", "o_norm_w", "conv_w", "conv_b",
            "ln_w", "ln_b", "f_bias", "w_branch", "w_out")


def _pack(parts):
    flat = jnp.concatenate([p.reshape(-1).astype(F32) for p in parts])
    rows = -(-flat.shape[0] // (8 * LANE)) * 8
    return jnp.pad(flat, (0, rows * LANE - flat.shape[0])).reshape(rows, LANE)


def _unpack(buf, shapes):
    flat, out, off = buf.reshape(-1), [], 0
    for shp in shapes:
        size = 1
        for s in shp:
            size *= s
        out.append(flat[off:off + size].reshape(shp))
        off += size
    return out


def kernel(x, pre_norm_w, post_norm_w, w_in, conv_qkv_w, a_log, dt_bias, o_norm_w, conv_w, conv_b, ln_w, ln_b, f_bias, w_branch, w_out, loss_target, m_pre_norm_w, m_post_norm_w, m_w_in, m_conv_qkv_w, m_a_log, m_dt_bias, m_o_norm_w, m_conv_w, m_conv_b, m_ln_w, m_ln_b, m_f_bias, m_w_branch, m_w_out, v_pre_norm_w, v_post_norm_w, v_w_in, v_conv_qkv_w, v_a_log, v_dt_bias, v_o_norm_w, v_conv_w, v_conv_b, v_ln_w, v_ln_b, v_f_bias, v_w_branch, v_w_out):
    weights = dict(pre_norm_w=pre_norm_w, post_norm_w=post_norm_w, w_in=w_in, conv_qkv_w=conv_qkv_w, a_log=a_log,
                   dt_bias=dt_bias, o_norm_w=o_norm_w, conv_w=conv_w, conv_b=conv_b, ln_w=ln_w, ln_b=ln_b, f_bias=f_bias,
                   w_branch=w_branch, w_out=w_out)
    mom1 = dict(pre_norm_w=m_pre_norm_w, post_norm_w=m_post_norm_w, w_in=m_w_in, conv_qkv_w=m_conv_qkv_w, a_log=m_a_log,
                dt_bias=m_dt_bias, o_norm_w=m_o_norm_w, conv_w=m_conv_w, conv_b=m_conv_b, ln_w=m_ln_w, ln_b=m_ln_b,
                f_bias=m_f_bias, w_branch=m_w_branch, w_out=m_w_out)
    mom2 = dict(pre_norm_w=v_pre_norm_w, post_norm_w=v_post_norm_w, w_in=v_w_in, conv_qkv_w=v_conv_qkv_w, a_log=v_a_log,
                dt_bias=v_dt_bias, o_norm_w=v_o_norm_w, conv_w=v_conv_w, conv_b=v_conv_b, ln_w=v_ln_w, ln_b=v_ln_b,
                f_bias=v_f_bias, w_branch=v_w_branch, w_out=v_w_out)
    chip = 2 * lax.axis_index("x") + lax.axis_index("y")
    core = lax.axis_index("c").astype(jnp.int32).reshape(1)
    chip_id = chip.astype(jnp.int32).reshape(1)

    w_in_b, w_out_b = w_in.astype(BF16), w_out.astype(BF16)
    w_br_b = w_branch.astype(BF16).reshape(DEPTH, N_BRANCH * BRANCH, D_MODEL // N_CHIPS)
    shards = lambda l: [w_in_b[l], w_br_b[l], w_out_b[l]]

    def whole(own, got, axis=-1):
        parts = []
        for s in range(N_CHIPS):
            d = chip ^ s
            parts.append(jnp.where(d == 0, own, jnp.where(d == 2, got[0], jnp.where(d == 1, got[1], got[2]))))
        return jnp.concatenate(parts, axis=axis)

    first = _exchange_chips("allgather_weights", "allgather", shards(0) + [conv_qkv_w, conv_w])
    c4_full = jnp.pad(whole(conv_qkv_w, first[3]), ((0, 0), (0, 8 - SHORT_CONV), (0, 0)))
    c31_full = jnp.pad(whole(conv_w, first[4]), ((0, 0), (0, 32 - CONF_CONV), (0, 0)))

    def layer_params(l, got):
        return dict(
            pre_w=pre_norm_w[l][None], post_w=post_norm_w[l][None], w_in=_pad_cols(whole(w_in_b[l], got[0])),
            w4=c4_full[l], alog=_lane_row(a_log[l], ALPHA_LANE), dtb=_lane_row(dt_bias[l], ALPHA_LANE),
            onw=o_norm_w[l][None], w31=c31_full[l], cb=conv_b[l][None], ln_w=ln_w[l][None], ln_b=ln_b[l][None],
            fb=_lane_row(f_bias[l], FORGET_LANE),
            wbr=whole(w_br_b[l], got[1]).reshape(N_BRANCH, BRANCH, D_MODEL), wout=whole(w_out_b[l], got[2], axis=0))

    act = x[0]
    got = first[:3]
    layers, saved = [], []
    for l in range(DEPTH):
        layers.append(layer_params(l, got))
        act, sv, got = _layer_fwd(act, layers[l], ("allgather", shards(l + 1)) if l + 1 < DEPTH else None)
        saved.append(sv)
    d_act, loss_blk = _loss_head("loss_head", act, loss_target[0])

    parts_in, parts_br, parts_out, small_g = [], [], [], []

    def finish_reduce(own, recv):
        mine = [_sum_chips("sum_chips", o, r, chip_id, core, t) for o, r, t in zip(own, recv, (64, 512, 128))]
        joined = _join_cores("join_cores", mine)
        for lst, j in zip((parts_in, parts_br, parts_out), joined):
            lst.append(j.reshape(2 * j.shape[1], j.shape[2]))

    pending = None
    for l in reversed(range(DEPTH)):
        d_act, g, recv = _layer_bwd(d_act, layers[l], saved[l], ("reduce_scatter", pending) if pending else None)
        if pending:
            finish_reduce(pending, recv)
        g_in = _unpad_cols(g["w_in"]).reshape(D_MODEL, N_CHIPS, N_IN // N_CHIPS).transpose(1, 0, 2)
        g_br = g["wbr"].reshape(N_BRANCH * BRANCH, N_CHIPS, D_MODEL // N_CHIPS).transpose(1, 0, 2)
        g_out = g["wout"].reshape(N_CHIPS, D_MODEL // N_CHIPS, D_MODEL)
        parts = [g_in, g_br, g_out]
        other = _halves_to_sibling("halves_to_sibling", parts)
        pending = [_add_own_half("add_own_half", f, o, core, t) for f, o, t in zip(parts, other, (64, 512, 128))]
        small_g.append(g)
    finish_reduce(pending, _exchange_chips("reduce_scatter_grads", "reduce_scatter", pending))
    small_g = small_g[::-1]
    parts_in, parts_br, parts_out = parts_in[::-1], parts_br[::-1], parts_out[::-1]

    stack = lambda key, f=lambda a: a: jnp.stack([f(g[key]) for g in small_g])
    small = dict(
        pre_norm_w=stack("pre_w", lambda a: a[0]), post_norm_w=stack("post_w", lambda a: a[0]),
        a_log=stack("alog", lambda a: a[0, ALPHA_LANE:ALPHA_LANE + HEADS]),
        dt_bias=stack("dtb", lambda a: a[0, ALPHA_LANE:ALPHA_LANE + HEADS]), o_norm_w=stack("onw", lambda a: a[0]),
        conv_b=stack("cb", lambda a: a[0]), ln_w=stack("ln_w", lambda a: a[0]), ln_b=stack("ln_b", lambda a: a[0]),
        f_bias=stack("fb", lambda a: a[0, FORGET_LANE:FORGET_LANE + HEADS]),
        conv_qkv_w=stack("w4", lambda a: a[:SHORT_CONV]), conv_w=stack("w31", lambda a: a[:CONF_CONV]),
        loss=loss_blk[0, 0:1])
    gathered = _allgather_devices("allgather_small", _pack([small[n] for n, _ in _SMALL]))
    total = _unpack(_sum_slots("sum_devices", gathered, gathered.shape[1]), [s for _, s in _SMALL])
    total = {n: t for (n, _), t in zip(_SMALL, total)}
    loss = total.pop("loss")[0]
    for n, width in _SHARDED_SMALL.items():
        total[n] = lax.dynamic_slice_in_dim(total[n], chip * width, width, axis=2)

    names = list(total)
    packed = [_pack([d[n] for n in names]) for d in (weights, mom1, mom2)]
    res = _adamw("adamw_small", packed[0], packed[1], packed[2], [_pack([total[n] for n in names])], packed[0].shape[0])
    shapes = [weights[n].shape for n in names]
    grads, delta, new_m, new_v = [dict(zip(names, _unpack(r, shapes))) for r in res]
    big = (("w_in", parts_in, (DEPTH * D_MODEL, N_IN // N_CHIPS), 64),
           ("w_branch", parts_br, (DEPTH * N_BRANCH * BRANCH, D_MODEL // N_CHIPS), 512),
           ("w_out", parts_out, (DEPTH * D_MODEL // N_CHIPS, D_MODEL), 128))
    for n, parts, shape2, tile in big:
        res = _adamw("adamw_" + n, weights[n].reshape(shape2), mom1[n].reshape(shape2), mom2[n].reshape(shape2),
                     parts, tile)
        grads[n], delta[n], new_m[n], new_v[n] = [r.reshape(weights[n].shape) for r in res]

    outs = [loss, d_act[None]]
    for d in (grads, delta, new_m, new_v):
        outs += [d[n] for n in _WEIGHTS]
    return tuple(outs)
```

```python
import functools

import jax
import jax.numpy as jnp
from jax import lax
from jax.experimental import pallas as pl
from jax.experimental.pallas import tpu as pltpu

F32 = jnp.float32
BF16 = jnp.bfloat16
MXU_DTYPE = jnp.bfloat16

D_MODEL = 2048
DEPTH = 4
BRANCH = 1024
HEAD_DIM = 128
HEADS = 8
CHUNK = 64
SHORT_CONV = 4
CONF_CONV = 31
N_BRANCH = 3
NORM_EPS = 1e-6
N_IN = 17432

OFF_GATE = 0
OFF_QKVA = 6144
OFF_ZA = 9216
OFF_VAL = 10240
OFF_GLUG = 11264
OFF_ZB = 12288
OFF_QKVC = 13312
OFF_ZC = 16384
OFF_SMALL = 17408
N_PAD = 17920
LANE = 128
BETA_LANE, ALPHA_LANE, FORGET_LANE = 0, 8, 16

ADAM_LR = 0.001
ADAM_B1 = 0.9
ADAM_B2 = 0.999
ADAM_EPS = 1e-08
ADAM_WD = 0.01
ADAM_STEP = 10

VMEM_LIMIT = 56 * 1024 * 1024

NN = (((1,), (0,)), ((), ()))
NT = (((1,), (1,)), ((), ()))
TN = (((0,), (0,)), ((), ()))
MESH = pl.DeviceIdType.MESH


def _params(sem=None):
    return pltpu.CompilerParams(dimension_semantics=sem, vmem_limit_bytes=VMEM_LIMIT)


def _dot(a, b, dims=NN):
    return lax.dot_general(a.astype(MXU_DTYPE), b.astype(MXU_DTYPE), dims, preferred_element_type=F32)


def _dot_hi(a, b, dims=NN):
    return lax.dot_general(a, b, dims, precision=lax.Precision.HIGHEST, preferred_element_type=F32)


def _dot_3x(a, b, dims=NN):
    a_hi, b_hi = a.astype(BF16), b.astype(BF16)
    a_lo, b_lo = (a - a_hi.astype(F32)).astype(BF16), (b - b_hi.astype(F32)).astype(BF16)
    dot = lambda u, v: lax.dot_general(u, v, dims, preferred_element_type=F32)
    return dot(a_hi, b_hi) + (dot(a_hi, b_lo) + dot(a_lo, b_hi))


def _sigmoid(x):
    return 1.0 / (1.0 + jnp.exp(-x))


def _silu(x):
    return x * _sigmoid(x)


def _softplus(x):
    return jnp.maximum(x, 0.0) + jnp.log(1.0 + jnp.exp(-jnp.abs(x)))


def _colsel(m, j):
    lane = lax.broadcasted_iota(jnp.int32, m.shape, 1)
    return jnp.sum(jnp.where(lane == j, m, 0.0), axis=1, keepdims=True)


def _rowsel(m, j):
    sub = lax.broadcasted_iota(jnp.int32, m.shape, 0)
    return jnp.sum(jnp.where(sub == j, m, 0.0), axis=0, keepdims=True)


def _row_specs(rows, tile):
    return [pl.BlockSpec((tile, w), functools.partial(lambda i, cb: (i, cb), cb=cb)) for (_, w, cb) in rows]


def _rowwise(name, fn, rows, params, outs, tile):
    seq = rows[0][0].shape[0]
    tile = min(tile, seq)
    n_in = len(rows) + len(params)

    def body(*refs):
        res = fn(*[r[...] for r in refs[:n_in]])
        for o_ref, r in zip(refs[n_in:], res):
            o_ref[...] = r.astype(o_ref.dtype)

    return pl.pallas_call(
        body, name=name, grid=(seq // tile,),
        in_specs=_row_specs(rows, tile) + [pl.BlockSpec(p.shape, lambda i: (0, 0)) for p in params],
        out_specs=[pl.BlockSpec((tile, w), lambda i: (i, 0)) for (w, _) in outs],
        out_shape=[jax.ShapeDtypeStruct((seq, w), dt) for (w, dt) in outs],
        compiler_params=_params(("parallel",)),
    )(*[r[0] for r in rows], *params)


def _rowwise_bwd(name, fn, rows, params, cts, row_grads, tile):
    seq = rows[0][0].shape[0]
    tile = min(tile, seq)
    nr, npar, nct = len(rows), len(params), len(cts)
    n_in = nr + npar

    def body(*refs):
        vals = [r[...] for r in refs[:n_in]]
        res, vjp = jax.vjp(fn, *vals)
        grads = vjp(tuple(c[...].astype(r.dtype) for c, r in zip(refs[n_in:n_in + nct], res)))
        outs = refs[n_in + nct:]
        k = 0
        for idx, dt in enumerate(row_grads):
            if dt is not None:
                outs[k][...] = grads[idx].astype(dt)
                k += 1
        first = pl.program_id(0) == 0
        for j in range(npar):
            g = grads[nr + j].astype(F32)
            o_ref = outs[k + j]

            @pl.when(first)
            def _(o_ref=o_ref, g=g):
                o_ref[...] = g

            @pl.when(jnp.logical_not(first))
            def _(o_ref=o_ref, g=g):
                o_ref[...] += g

    want = [(rows[i][1], dt) for i, dt in enumerate(row_grads) if dt is not None]
    return pl.pallas_call(
        body, name=name, grid=(seq // tile,),
        in_specs=(_row_specs(rows, tile) + [pl.BlockSpec(p.shape, lambda i: (0, 0)) for p in params]
                  + [pl.BlockSpec((tile, c.shape[1]), lambda i: (i, 0)) for c in cts]),
        out_specs=([pl.BlockSpec((tile, w), lambda i: (i, 0)) for (w, _) in want]
                   + [pl.BlockSpec(p.shape, lambda i: (0, 0)) for p in params]),
        out_shape=([jax.ShapeDtypeStruct((seq, w), dt) for (w, dt) in want]
                   + [jax.ShapeDtypeStruct(p.shape, F32) for p in params]),
        compiler_params=_params(("arbitrary",)),
    )(*[r[0] for r in rows], *params, *cts)


def _rms(x, w):
    x = x.astype(F32)
    return x * lax.rsqrt(jnp.mean(x * x, axis=-1, keepdims=True) + NORM_EPS) * w


def _rms_pre_fn(x, w):
    return (_rms(x, w),)


def _rms_pre_res_fn(x, w):
    return (_rms(x, w), x)


def _rms_post_fn(out, x, w):
    return (x + _rms(out, w),)


def _rms_only_fn(out, w):
    return (_rms(out, w),)


def _ln_gate_fn(u, z, w, b):
    u = u.astype(F32)
    uc = u - jnp.mean(u, axis=-1, keepdims=True)
    y = uc * lax.rsqrt(jnp.mean(uc * uc, axis=-1, keepdims=True) + NORM_EPS) * w + b
    return (_silu(y) * _silu(z.astype(F32)),)


def _merge_fn(g0, g1, g2, b0, b1, b2):
    return (_sigmoid(g0) * b0 + _sigmoid(g1) * b1 + _sigmoid(g2) * b2,)


def _matmul(name, a, b, mode, out_dtype, tm, tn, tk):
    if mode == "nn":
        (m, kc), n = a.shape, b.shape[1]
    elif mode == "nt":
        (m, kc), n = a.shape, b.shape[0]
    else:
        (kc, m), n = a.shape, b.shape[1]
    tm, tn, tk = min(tm, m), min(tn, n), min(tk, kc)
    nk = kc // tk
    assert m % tm == 0 and n % tn == 0 and kc % tk == 0, (name, a.shape, b.shape)
    dims = {"nn": NN, "nt": NT, "tn": TN}[mode]
    a_spec = (pl.BlockSpec((tk, tm), lambda j, i, k: (k, i)) if mode == "tn"
              else pl.BlockSpec((tm, tk), lambda j, i, k: (i, k)))
    b_spec = (pl.BlockSpec((tn, tk), lambda j, i, k: (j, k)) if mode == "nt"
              else pl.BlockSpec((tk, tn), lambda j, i, k: (k, j)))
    use_acc = nk > 1 and out_dtype != F32

    def body(a_ref, b_ref, o_ref, *acc):
        p = _dot(a_ref[...], b_ref[...], dims)
        if nk == 1:
            o_ref[...] = p.astype(out_dtype)
            return
        k = pl.program_id(2)
        dst = acc[0] if use_acc else o_ref

        @pl.when(k == 0)
        def _():
            dst[...] = p

        @pl.when(k > 0)
        def _():
            dst[...] += p

        if use_acc:
            @pl.when(k == nk - 1)
            def _():
                o_ref[...] = dst[...].astype(out_dtype)

    return pl.pallas_call(
        body, name=name, grid=(n // tn, m // tm, nk),
        in_specs=[a_spec, b_spec],
        out_specs=pl.BlockSpec((tm, tn), lambda j, i, k: (i, j)),
        out_shape=jax.ShapeDtypeStruct((m, n), out_dtype),
        scratch_shapes=[pltpu.VMEM((tm, tn), F32)] if use_acc else [],
        compiler_params=_params(("parallel", "parallel", "arbitrary")),
    )(a, b)


HALO = 32
CONV_TC = 256
CONV_T = 1024


def _conv_fwd(name, x, x_off, ch, w, b, k_width, gate_off=None):
    seq = x.shape[0]
    t_blk = min(CONV_T, seq)
    tc = CONV_TC
    hb = t_blk // HALO
    xcb = x_off // tc
    has_gate = gate_off is not None

    def body(*refs):
        if has_gate:
            xm_ref, xh_ref, gm_ref, gh_ref, w_ref, b_ref, y_ref, win = refs
        else:
            xm_ref, xh_ref, w_ref, b_ref, y_ref, win = refs
        t = pl.program_id(1)
        xm, xh = xm_ref[...], xh_ref[...]
        if has_gate:
            xm = xm * _sigmoid(gm_ref[...])
            xh = xh * _sigmoid(gh_ref[...])
        win[0:HALO, :] = jnp.where(t == 0, 0.0, xh)
        win[HALO:HALO + t_blk, :] = xm
        acc = jnp.broadcast_to(b_ref[...], (t_blk, tc))
        for k in range(k_width):
            acc = acc + w_ref[k:k + 1, :] * win[HALO - (k_width - 1) + k:HALO - (k_width - 1) + k + t_blk, :]
        y_ref[...] = acc

    main = lambda off: pl.BlockSpec((t_blk, tc), lambda c, t: (t, off + c))
    halo = lambda off: pl.BlockSpec((HALO, tc), lambda c, t: (jnp.maximum(t * hb - 1, 0), off + c))
    ins, specs = [x, x], [main(xcb), halo(xcb)]
    if has_gate:
        gcb = gate_off // tc
        ins += [x, x]
        specs += [main(gcb), halo(gcb)]
    ins += [w, b]
    specs += [pl.BlockSpec((w.shape[0], tc), lambda c, t: (0, c)), pl.BlockSpec((1, tc), lambda c, t: (0, c))]
    return pl.pallas_call(
        body, name=name, grid=(ch // tc, seq // t_blk), in_specs=specs,
        out_specs=pl.BlockSpec((t_blk, tc), lambda c, t: (t, c)),
        out_shape=jax.ShapeDtypeStruct((seq, ch), F32),
        scratch_shapes=[pltpu.VMEM((HALO + t_blk, tc), F32)],
        compiler_params=_params(("parallel", "arbitrary")),
    )(*ins)


def _conv_bwd(name, dy, x, x_off, ch, w, k_width, gate_off=None):
    seq = x.shape[0]
    t_blk = min(CONV_T, seq)
    tc = CONV_TC
    hb = t_blk // HALO
    nt = seq // t_blk
    xcb = x_off // tc
    has_gate = gate_off is not None
    kp = w.shape[0]

    def body(*refs):
        if has_gate:
            dm_ref, dh_ref, xm_ref, xh_ref, gm_ref, gh_ref, w_ref, dv_ref, dg_ref, dw_ref, db_ref, winx, wind = refs
        else:
            dm_ref, dh_ref, xm_ref, xh_ref, w_ref, dx_ref, dw_ref, db_ref, winx, wind = refs
        t = pl.program_id(1)
        xm, xh = xm_ref[...], xh_ref[...]
        if has_gate:
            sg = _sigmoid(gm_ref[...])
            um = xm * sg
            uh = xh * _sigmoid(gh_ref[...])
        else:
            um, uh = xm, xh
        winx[0:HALO, :] = jnp.where(t == nt - 1, 0.0, uh)
        winx[HALO:HALO + t_blk, :] = um
        dm = dm_ref[...]
        wind[0:t_blk, :] = dm
        wind[t_blk:t_blk + HALO, :] = jnp.where(t == 0, 0.0, dh_ref[...])
        du = jnp.zeros((t_blk, tc), F32)
        for k in range(k_width):
            du = du + w_ref[k:k + 1, :] * wind[k_width - 1 - k:k_width - 1 - k + t_blk, :]
        if has_gate:
            dv_ref[...] = (du * sg).astype(dv_ref.dtype)
            dg_ref[...] = (du * xm * sg * (1.0 - sg)).astype(dg_ref.dtype)
        else:
            dx_ref[...] = du.astype(dx_ref.dtype)

        @pl.when(t == 0)
        def _():
            dw_ref[...] = jnp.zeros_like(dw_ref)
            db_ref[...] = jnp.zeros_like(db_ref)

        for k in range(k_width):
            s0 = HALO - (k_width - 1) + k
            dw_ref[k:k + 1, :] += jnp.sum(dm * winx[s0:s0 + t_blk, :], axis=0, keepdims=True)
        db_ref[...] += jnp.sum(dm, axis=0, keepdims=True)

    rt = lambda t: nt - 1 - t
    main = lambda off: pl.BlockSpec((t_blk, tc), lambda c, t: (rt(t), off + c))
    past = lambda off: pl.BlockSpec((HALO, tc), lambda c, t: (jnp.maximum(rt(t) * hb - 1, 0), off + c))
    future = pl.BlockSpec((HALO, tc), lambda c, t: (jnp.minimum((rt(t) + 1) * hb, seq // HALO - 1), c))
    ins, specs = [dy, dy, x, x], [main(0), future, main(xcb), past(xcb)]
    if has_gate:
        gcb = gate_off // tc
        ins += [x, x]
        specs += [main(gcb), past(gcb)]
    ins += [w]
    specs += [pl.BlockSpec((kp, tc), lambda c, t: (0, c))]
    blk = pl.BlockSpec((t_blk, tc), lambda c, t: (rt(t), c))
    n_dx = 2 if has_gate else 1
    return pl.pallas_call(
        body, name=name, grid=(ch // tc, nt), in_specs=specs,
        out_specs=[blk] * n_dx + [pl.BlockSpec((kp, tc), lambda c, t: (0, c)), pl.BlockSpec((1, tc), lambda c, t: (0, c))],
        out_shape=[jax.ShapeDtypeStruct((seq, ch), BF16)] * n_dx + [jax.ShapeDtypeStruct((kp, ch), F32),
                                                                    jax.ShapeDtypeStruct((1, ch), F32)],
        scratch_shapes=[pltpu.VMEM((HALO + t_blk, tc), F32), pltpu.VMEM((HALO + t_blk, tc), F32)],
        compiler_params=_params(("parallel", "arbitrary")),
    )(*ins)


@jax.custom_vjp
def _inv_unit_lower(lows):
    n = lows[0].shape[0]
    eye = (lax.broadcasted_iota(jnp.int32, (n, n), 0) == lax.broadcasted_iota(jnp.int32, (n, n), 1)).astype(F32)
    accs = [eye - low for low in lows]
    pws = list(lows)
    steps = 1
    while steps * 2 < n:
        pws = [_dot_3x(pw, pw) for pw in pws]
        accs = [acc + _dot_3x(acc, pw) for acc, pw in zip(accs, pws)]
        steps *= 2
    return tuple(accs)


def _inv_fwd(lows):
    ts = _inv_unit_lower(lows)
    return ts, ts


def _inv_bwd(ts, dts):
    left = [_dot_3x(t, dt, TN) for t, dt in zip(ts, dts)]
    return (tuple(-_dot_3x(l, t, NT) for l, t in zip(left, ts)),)


_inv_unit_lower.defvjp(_inv_fwd, _inv_bwd)


def _gdr_chunk(cqkv, z, sm, alog, dtb, onw, state):
    c = cqkv.shape[0]
    hs = range(HEADS)
    ri = lax.broadcasted_iota(jnp.int32, (c, c), 0)
    ci = lax.broadcasted_iota(jnp.int32, (c, c), 1)
    incl, strict = ri >= ci, ri > ci
    beta_all = _sigmoid(sm)
    la_all = -jnp.exp(alog) * _softplus(sm + dtb)
    g_cols = _dot_hi(incl.astype(F32), la_all)
    g_rows = _dot_hi(la_all, (ri <= ci).astype(F32), TN)
    g_end = jnp.sum(la_all, axis=0, keepdims=True)
    act = _silu(cqkv)
    sl = lambda base, h: slice(base + h * HEAD_DIM, base + (h + 1) * HEAD_DIM)
    q = [act[:, sl(0, h)] for h in hs]
    k = [act[:, sl(BRANCH, h)] for h in hs]
    v = [act[:, sl(2 * BRANCH, h)] for h in hs]
    q = [x * lax.rsqrt(jnp.sum(x * x, axis=-1, keepdims=True) + NORM_EPS) * (HEAD_DIM ** -0.5) for x in q]
    k = [x * lax.rsqrt(jnp.sum(x * x, axis=-1, keepdims=True) + NORM_EPS) for x in k]
    beta = [_colsel(beta_all, BETA_LANE + h) for h in hs]
    g = [_colsel(g_cols, ALPHA_LANE + h) for h in hs]
    g_row = [_rowsel(g_rows, ALPHA_LANE + h) for h in hs]
    g_last = [_colsel(g_end, ALPHA_LANE + h) for h in hs]
    decay = [jnp.where(incl, jnp.exp(jnp.where(incl, g[h] - g_row[h], 0.0)), 0.0) for h in hs]
    kk = [_dot(k[h], k[h], NT) for h in hs]
    qk = [_dot(q[h], k[h], NT) * decay[h] for h in hs]
    t_inv = _inv_unit_lower(tuple(jnp.where(strict, beta[h] * kk[h] * decay[h], 0.0) for h in hs))
    eg = [jnp.exp(g[h]) for h in hs]
    u0 = [_dot(t_inv[h], v[h] * beta[h]) for h in hs]
    w_cum = [_dot(t_inv[h], k[h] * (beta[h] * eg[h])) for h in hs]
    s_in = [state[h] for h in hs]
    u = [u0[h] - _dot(w_cum[h], s_in[h]) for h in hs]
    o = [_dot(q[h] * eg[h], s_in[h]) + _dot(qk[h], u[h]) for h in hs]
    s_out = [s_in[h] * jnp.exp(g_last[h]) + _dot(k[h] * jnp.exp(g_last[h] - g[h]), u[h], TN) for h in hs]
    o = [x * lax.rsqrt(jnp.mean(x * x, axis=-1, keepdims=True) + NORM_EPS) * onw for x in o]
    y = [o[h] * _silu(z[:, sl(0, h)]) for h in hs]
    return jnp.concatenate(y, axis=1), jnp.concatenate([s[None] for s in s_out], axis=0)


def _gdr_specs(nc, order):
    return [
        pl.BlockSpec((CHUNK, 3 * BRANCH), lambda n: (order(n), 0)),
        pl.BlockSpec((CHUNK, BRANCH), lambda n: (order(n), OFF_ZA // BRANCH)),
        pl.BlockSpec((CHUNK, LANE), lambda n: (order(n), OFF_SMALL // LANE)),
        pl.BlockSpec((1, LANE), lambda n: (0, 0)),
        pl.BlockSpec((1, LANE), lambda n: (0, 0)),
        pl.BlockSpec((1, LANE), lambda n: (0, 0)),
    ]


def _gdr_fwd(name, cqkv, proj, alog, dtb, onw):
    seq = cqkv.shape[0]
    nc = seq // CHUNK

    def body(c_ref, z_ref, sm_ref, al_ref, dt_ref, on_ref, y_ref, st_ref, state):
        @pl.when(pl.program_id(0) == 0)
        def _():
            state[...] = jnp.zeros_like(state)

        s_in = state[...]
        st_ref[0] = s_in
        y, s_out = _gdr_chunk(c_ref[...], z_ref[...], sm_ref[...], al_ref[...], dt_ref[...], on_ref[...], s_in)
        y_ref[...] = y.astype(y_ref.dtype)
        state[...] = s_out

    return pl.pallas_call(
        body, name=name, grid=(nc,), in_specs=_gdr_specs(nc, lambda n: n),
        out_specs=[pl.BlockSpec((CHUNK, BRANCH), lambda n: (n, 0)),
                   pl.BlockSpec((1, HEADS, HEAD_DIM, HEAD_DIM), lambda n: (n, 0, 0, 0))],
        out_shape=[jax.ShapeDtypeStruct((seq, BRANCH), BF16),
                   jax.ShapeDtypeStruct((nc, HEADS, HEAD_DIM, HEAD_DIM), F32)],
        scratch_shapes=[pltpu.VMEM((HEADS, HEAD_DIM, HEAD_DIM), F32)],
        compiler_params=_params(("arbitrary",)),
    )(cqkv, proj, proj, alog, dtb, onw)


def _gdr_bwd(name, dy, states, cqkv, proj, alog, dtb, onw):
    seq = cqkv.shape[0]
    nc = seq // CHUNK
    rev = lambda n: nc - 1 - n

    def body(c_ref, z_ref, sm_ref, al_ref, dt_ref, on_ref, dy_ref, st_ref,
             dc_ref, dz_ref, dsm_ref, dal_ref, ddt_ref, don_ref, dstate):
        first = pl.program_id(0) == 0

        @pl.when(first)
        def _():
            dstate[...] = jnp.zeros_like(dstate)
            dal_ref[...] = jnp.zeros_like(dal_ref)
            ddt_ref[...] = jnp.zeros_like(ddt_ref)
            don_ref[...] = jnp.zeros_like(don_ref)

        _, vjp = jax.vjp(_gdr_chunk, c_ref[...], z_ref[...], sm_ref[...], al_ref[...], dt_ref[...], on_ref[...],
                         st_ref[0])
        dc, dz, dsm, dal, ddt, don, ds = vjp((dy_ref[...].astype(F32), dstate[...]))
        dc_ref[...] = dc
        dz_ref[...] = dz.astype(dz_ref.dtype)
        dsm_ref[...] = dsm
        dal_ref[...] += dal
        ddt_ref[...] += ddt
        don_ref[...] += don
        dstate[...] = ds

    small = pl.BlockSpec((1, LANE), lambda n: (0, 0))
    return pl.pallas_call(
        body, name=name, grid=(nc,),
        in_specs=_gdr_specs(nc, rev) + [pl.BlockSpec((CHUNK, BRANCH), lambda n: (rev(n), 0)),
                                        pl.BlockSpec((1, HEADS, HEAD_DIM, HEAD_DIM), lambda n: (rev(n), 0, 0, 0))],
        out_specs=[pl.BlockSpec((CHUNK, 3 * BRANCH), lambda n: (rev(n), 0)),
                   pl.BlockSpec((CHUNK, BRANCH), lambda n: (rev(n), 0)),
                   pl.BlockSpec((CHUNK, LANE), lambda n: (rev(n), 0)), small, small, small],
        out_shape=[jax.ShapeDtypeStruct((seq, 3 * BRANCH), F32), jax.ShapeDtypeStruct((seq, BRANCH), BF16),
                   jax.ShapeDtypeStruct((seq, LANE), F32)] + [jax.ShapeDtypeStruct((1, LANE), F32)] * 3,
        scratch_shapes=[pltpu.VMEM((HEADS, HEAD_DIM, HEAD_DIM), F32)],
        compiler_params=_params(("arbitrary",)),
    )(cqkv, proj, proj, alog, dtb, onw, dy, states)


GATE_T = 512
ATT_T = 1024


def _fox_gate_fwd(name, proj, fb):
    seq = proj.shape[0]
    tb = min(GATE_T, seq)

    def body(sm_ref, fb_ref, c_ref, ct_ref):
        tri = (lax.broadcasted_iota(jnp.int32, (tb, tb), 0) >= lax.broadcasted_iota(jnp.int32, (tb, tb), 1)).astype(F32)
        carry = jnp.zeros((1, LANE), F32)
        for i in range(seq // tb):
            lf = -_softplus(-(sm_ref[i * tb:(i + 1) * tb, :] + fb_ref[...]))
            cb = _dot_hi(tri, lf) + carry
            c_ref[i * tb:(i + 1) * tb, :] = cb
            ct_ref[:, i * tb:(i + 1) * tb] = cb.T
            carry = carry + jnp.sum(lf, axis=0, keepdims=True)

    return pl.pallas_call(
        body, name=name, grid=(1,),
        in_specs=[pl.BlockSpec((seq, LANE), lambda i: (0, OFF_SMALL // LANE)), pl.BlockSpec((1, LANE), lambda i: (0, 0))],
        out_specs=[pl.BlockSpec((seq, LANE), lambda i: (0, 0)), pl.BlockSpec((LANE, seq), lambda i: (0, 0))],
        out_shape=[jax.ShapeDtypeStruct((seq, LANE), F32), jax.ShapeDtypeStruct((LANE, seq), F32)],
        compiler_params=_params(("arbitrary",)),
    )(proj, fb)


def _fox_gate_bwd(name, dck, proj, fb):
    seq = proj.shape[0]
    tb = min(GATE_T, seq)
    nb = seq // tb

    def body(d_ref, sm_ref, fb_ref, o_ref, dfb_ref, pad):
        tri = (lax.broadcasted_iota(jnp.int32, (tb, tb), 0) >= lax.broadcasted_iota(jnp.int32, (tb, tb), 1)).astype(F32)
        pad[...] = jnp.zeros_like(pad)
        carry = jnp.zeros((HEADS, 1), F32)
        dfb = jnp.zeros((1, LANE), F32)
        for i in reversed(range(nb)):
            blk = d_ref[:, i * tb:(i + 1) * tb]
            pad[FORGET_LANE:FORGET_LANE + HEADS, :] = _dot_hi(blk, tri) + carry
            carry = carry + jnp.sum(blk, axis=1, keepdims=True)
            x = sm_ref[i * tb:(i + 1) * tb, :] + fb_ref[...]
            dsm = pad[...].T * _sigmoid(-x)
            o_ref[i * tb:(i + 1) * tb, :] = dsm
            dfb = dfb + jnp.sum(dsm, axis=0, keepdims=True)
        dfb_ref[...] = dfb

    return pl.pallas_call(
        body, name=name, grid=(1,),
        in_specs=[pl.BlockSpec((HEADS, seq), lambda i: (0, 0)), pl.BlockSpec((seq, LANE), lambda i: (0, OFF_SMALL // LANE)),
                  pl.BlockSpec((1, LANE), lambda i: (0, 0))],
        out_specs=[pl.BlockSpec((seq, LANE), lambda i: (0, 0)), pl.BlockSpec((1, LANE), lambda i: (0, 0))],
        out_shape=[jax.ShapeDtypeStruct((seq, LANE), F32), jax.ShapeDtypeStruct((1, LANE), F32)],
        scratch_shapes=[pltpu.VMEM((LANE, tb), F32)],
        compiler_params=_params(("arbitrary",)),
    )(dck, proj, fb)


def _att_scores(q, k, cq, ck_row, diagonal):
    s = _dot(q, k, NT) * (HEAD_DIM ** -0.5) + (cq - ck_row)
    if diagonal:
        keep = lax.broadcasted_iota(jnp.int32, s.shape, 0) >= lax.broadcasted_iota(jnp.int32, s.shape, 1)
        s = jnp.where(keep, s, -jnp.inf)
    return s


def _causal_pairs(nq, key_major):
    pairs = ([(i, j) for j in range(nq) for i in range(j, nq)] if key_major
             else [(i, j) for i in range(nq) for j in range(i + 1)])
    return jnp.asarray([p[0] for p in pairs], jnp.int32), jnp.asarray([p[1] for p in pairs], jnp.int32)


def _attn_specs(tq):
    qs = lambda off: pl.BlockSpec((tq, HEAD_DIM), lambda h, t, it, jt: (it[t], off + h))
    kv = lambda off: pl.BlockSpec((tq, HEAD_DIM), lambda h, t, it, jt: (jt[t], off + h))
    c_spec = pl.BlockSpec((tq, LANE), lambda h, t, it, jt: (it[t], 0))
    ct_spec = pl.BlockSpec((1, 1, tq), lambda h, t, it, jt: (FORGET_LANE + h, 0, jt[t]))
    lse_spec = pl.BlockSpec((1, tq, LANE), lambda h, t, it, jt: (h, it[t], 0))
    return qs, kv, c_spec, ct_spec, lse_spec


def _ride_along(comm, refs, n_in, n_out, n_scratch, first, last):
    n_c = len(comm[1]) if comm else 0
    ins, c_in = refs[:n_in], refs[n_in:n_in + n_c]
    outs, c_out = refs[n_in + n_c:n_in + n_c + n_out], refs[n_in + n_c + n_out:n_in + 2 * n_c + n_out]
    scratch = refs[n_in + 2 * n_c + n_out:n_in + 2 * n_c + n_out + n_scratch]
    sems = refs[n_in + 2 * n_c + n_out + n_scratch:]
    if not comm:
        return ins, outs, scratch, lambda: None
    start, finish = _comm_ops(comm[0], c_in, c_out, sems)
    pl.when(first)(start)
    return ins, outs, scratch, lambda: pl.when(last)(finish)


def _attn_fwd(name, proj, c, ct3, comm=None):
    seq = proj.shape[0]
    tq = min(ATT_T, seq)
    nq = seq // tq
    qb, zb = OFF_QKVC // HEAD_DIM, OFF_ZC // HEAD_DIM
    i_tab, j_tab = _causal_pairs(nq, False)
    n_pairs = i_tab.shape[0]
    c_arrs = list(comm[1]) if comm else []
    c_shapes, c_sems = _comm_plan(comm[0], c_arrs) if comm else ([], [])

    def body(it, jt, *refs):
        h, t = pl.program_id(0), pl.program_id(1)
        i, j = it[t], jt[t]
        ins, outs, scratch, finish = _ride_along(comm, refs, 6, 3, 3, (h == 0) & (t == 0),
                                                 (h == HEADS - 1) & (t == n_pairs - 1))
        q_ref, k_ref, v_ref, z_ref, c_ref, ct_ref = ins
        y_ref, o_ref, lse_ref = outs
        m_s, l_s, acc_s = scratch

        @pl.when(j == 0)
        def _():
            m_s[...] = jnp.full_like(m_s, -jnp.inf)
            l_s[...] = jnp.zeros_like(l_s)
            acc_s[...] = jnp.zeros_like(acc_s)

        def step(diagonal):
            cq = _colsel(c_ref[...], FORGET_LANE + h)
            s = _att_scores(q_ref[...], k_ref[...], cq, ct_ref[0], diagonal)
            m_old = m_s[...]
            m_new = jnp.maximum(m_old, jnp.max(s, axis=1, keepdims=True))
            p = jnp.exp(s - m_new)
            alpha = jnp.exp(m_old - m_new)
            l_s[...] = alpha * l_s[...] + jnp.sum(p, axis=1, keepdims=True)
            p_hi = p.astype(MXU_DTYPE).astype(F32)
            acc_s[...] = alpha * acc_s[...] + _dot(p_hi, v_ref[...]) + _dot(p - p_hi, v_ref[...])
            m_s[...] = m_new

        @pl.when(j < i)
        def _():
            step(False)

        @pl.when(j == i)
        def _():
            step(True)
            o = acc_s[...] / l_s[...]
            o_ref[...] = o
            y_ref[...] = (o * _silu(z_ref[...])).astype(y_ref.dtype)
            lse_ref[0] = jnp.broadcast_to(m_s[...] + jnp.log(l_s[...]), (tq, LANE))

        finish()

    qs, kv, c_spec, ct_spec, lse_spec = _attn_specs(tq)
    res = pl.pallas_call(
        body, name=name,
        grid_spec=pltpu.PrefetchScalarGridSpec(
            num_scalar_prefetch=2, grid=(HEADS, n_pairs),
            in_specs=[qs(qb), kv(qb + HEADS), kv(qb + 2 * HEADS), qs(zb), c_spec, ct_spec] + [HBM_SPEC] * len(c_arrs),
            out_specs=[qs(0), qs(0), lse_spec] + [HBM_SPEC] * len(c_arrs),
            scratch_shapes=[pltpu.VMEM((tq, 1), F32), pltpu.VMEM((tq, 1), F32), pltpu.VMEM((tq, HEAD_DIM), F32)]
            + c_sems),
        out_shape=[jax.ShapeDtypeStruct((seq, BRANCH), BF16), jax.ShapeDtypeStruct((seq, BRANCH), F32),
                   jax.ShapeDtypeStruct((HEADS, seq, LANE), F32)] + c_shapes,
        compiler_params=_params(("arbitrary", "arbitrary")),
    )(i_tab, j_tab, proj, proj, proj, proj, c, ct3, *c_arrs)
    return res[0], res[1], res[2], list(res[3:])


def _attn_dq(name, dy, o, lse, proj, c, ct3):
    seq = proj.shape[0]
    tq = min(ATT_T, seq)
    nq = seq // tq
    qb, zb = OFF_QKVC // HEAD_DIM, OFF_ZC // HEAD_DIM
    i_tab, j_tab = _causal_pairs(nq, False)

    def body(it, jt, q_ref, k_ref, v_ref, z_ref, c_ref, ct_ref, dy_ref, o_ref, lse_ref, dq_ref, dz_ref, do_s, dl_s, acc_s):
        h, t = pl.program_id(0), pl.program_id(1)
        i, j = it[t], jt[t]

        @pl.when(j == 0)
        def _():
            z = z_ref[...]
            sg = _sigmoid(z)
            dyv = dy_ref[...].astype(F32)
            do = dyv * z * sg
            do_s[...] = do
            dl_s[...] = jnp.sum(do.astype(MXU_DTYPE).astype(F32) * o_ref[...], axis=1, keepdims=True)
            dz_ref[...] = (dyv * o_ref[...] * sg * (1.0 + z * (1.0 - sg))).astype(dz_ref.dtype)
            acc_s[...] = jnp.zeros_like(acc_s)

        def step(diagonal):
            cq = _colsel(c_ref[...], FORGET_LANE + h)
            s = _att_scores(q_ref[...], k_ref[...], cq, ct_ref[0], diagonal)
            p = jnp.exp(s - jnp.max(lse_ref[0], axis=1, keepdims=True))
            dp = _dot(do_s[...], v_ref[...], NT)
            ds = p * (dp - dl_s[...])
            acc_s[...] += _dot(ds, k_ref[...])

        @pl.when(j < i)
        def _():
            step(False)

        @pl.when(j == i)
        def _():
            step(True)
            dq_ref[...] = (acc_s[...] * (HEAD_DIM ** -0.5)).astype(dq_ref.dtype)

    qs, kv, c_spec, ct_spec, lse_spec = _attn_specs(tq)
    return pl.pallas_call(
        body, name=name,
        grid_spec=pltpu.PrefetchScalarGridSpec(
            num_scalar_prefetch=2, grid=(HEADS, i_tab.shape[0]),
            in_specs=[qs(qb), kv(qb + HEADS), kv(qb + 2 * HEADS), qs(zb), c_spec, ct_spec, qs(0), qs(0), lse_spec],
            out_specs=[qs(0), qs(0)],
            scratch_shapes=[pltpu.VMEM((tq, HEAD_DIM), F32), pltpu.VMEM((tq, 1), F32), pltpu.VMEM((tq, HEAD_DIM), F32)]),
        out_shape=[jax.ShapeDtypeStruct((seq, BRANCH), BF16)] * 2,
        compiler_params=_params(("parallel", "arbitrary")),
    )(i_tab, j_tab, proj, proj, proj, proj, c, ct3, dy, o, lse)


def _attn_dkv(name, dy, o, lse, proj, c, ct3, comm=None):
    seq = proj.shape[0]
    tq = min(ATT_T, seq)
    nq = seq // tq
    qb, zb = OFF_QKVC // HEAD_DIM, OFF_ZC // HEAD_DIM
    i_tab, j_tab = _causal_pairs(nq, True)
    n_pairs = i_tab.shape[0]
    c_arrs = list(comm[1]) if comm else []
    c_shapes, c_sems = _comm_plan(comm[0], c_arrs) if comm else ([], [])

    def body(it, jt, *refs):
        h, t = pl.program_id(0), pl.program_id(1)
        i, j = it[t], jt[t]
        ins, outs, scratch, finish = _ride_along(comm, refs, 9, 3, 3, (h == 0) & (t == 0),
                                                 (h == HEADS - 1) & (t == n_pairs - 1))
        q_ref, k_ref, v_ref, z_ref, c_ref, ct_ref, dy_ref, o_ref, lse_ref = ins
        dk_ref, dv_ref, dc_ref = outs
        dk_s, dv_s, dc_s = scratch

        @pl.when(i == j)
        def _():
            dk_s[...] = jnp.zeros_like(dk_s)
            dv_s[...] = jnp.zeros_like(dv_s)
            dc_s[...] = jnp.zeros_like(dc_s)

        def step(diagonal):
            z = z_ref[...]
            do = dy_ref[...].astype(F32) * _silu(z)
            delta = jnp.sum(do.astype(MXU_DTYPE).astype(F32) * o_ref[...], axis=1, keepdims=True)
            cq = _colsel(c_ref[...], FORGET_LANE + h)
            s = _att_scores(q_ref[...], k_ref[...], cq, ct_ref[0], diagonal)
            p = jnp.exp(s - jnp.max(lse_ref[0], axis=1, keepdims=True))
            dv_s[...] += _dot(p, do, TN)
            ds = p * (_dot(do, v_ref[...], NT) - delta)
            dk_s[...] += _dot(ds, q_ref[...], TN)
            dc_s[...] -= jnp.sum(ds, axis=0, keepdims=True)

        @pl.when(i == j)
        def _():
            step(True)

        @pl.when(i > j)
        def _():
            step(False)

        @pl.when(i == nq - 1)
        def _():
            dk_ref[...] = (dk_s[...] * (HEAD_DIM ** -0.5)).astype(dk_ref.dtype)
            dv_ref[...] = dv_s[...].astype(dv_ref.dtype)
            dc_ref[0] = dc_s[...]

        finish()

    qs, kv, c_spec, ct_spec, lse_spec = _attn_specs(tq)
    res = pl.pallas_call(
        body, name=name,
        grid_spec=pltpu.PrefetchScalarGridSpec(
            num_scalar_prefetch=2, grid=(HEADS, n_pairs),
            in_specs=[qs(qb), kv(qb + HEADS), kv(qb + 2 * HEADS), qs(zb), c_spec, ct_spec, qs(0), qs(0), lse_spec]
            + [HBM_SPEC] * len(c_arrs),
            out_specs=[kv(0), kv(0), pl.BlockSpec((1, 1, tq), lambda h, t, it, jt: (h, 0, jt[t]))]
            + [HBM_SPEC] * len(c_arrs),
            scratch_shapes=[pltpu.VMEM((tq, HEAD_DIM), F32), pltpu.VMEM((tq, HEAD_DIM), F32), pltpu.VMEM((1, tq), F32)]
            + c_sems),
        out_shape=[jax.ShapeDtypeStruct((seq, BRANCH), BF16)] * 2 + [jax.ShapeDtypeStruct((HEADS, 1, seq), F32)]
        + c_shapes,
        compiler_params=_params(("arbitrary", "arbitrary")),
    )(i_tab, j_tab, proj, proj, proj, proj, c, ct3, dy, o, lse, *c_arrs)
    return res[0], res[1], res[2], list(res[3:])


def _loss_head(name, y, target):
    seq, d = y.shape
    tile = min(256, seq)

    def body(y_ref, t_ref, dy_ref, l_ref):
        err = y_ref[...] - t_ref[...]
        dy_ref[...] = err / d

        @pl.when(pl.program_id(0) == 0)
        def _():
            l_ref[...] = jnp.zeros_like(l_ref)

        l_ref[...] += 0.5 * jnp.sum(jnp.mean(err * err, axis=-1, keepdims=True), axis=0, keepdims=True)

    return pl.pallas_call(
        body, name=name, grid=(seq // tile,),
        in_specs=[pl.BlockSpec((tile, d), lambda i: (i, 0))] * 2,
        out_specs=[pl.BlockSpec((tile, d), lambda i: (i, 0)), pl.BlockSpec((8, LANE), lambda i: (0, 0))],
        out_shape=[jax.ShapeDtypeStruct((seq, d), F32), jax.ShapeDtypeStruct((8, LANE), F32)],
        compiler_params=_params(("arbitrary",)),
    )(y, target)


def _adamw(name, w, m, v, g_parts, tile):
    rows, cols = w.shape
    n_g = len(g_parts)
    part_rows = rows // n_g
    tile = min(tile, part_rows)
    assert part_rows % tile == 0 and all(p.shape == (part_rows, cols) for p in g_parts), (name, w.shape)
    nb = part_rows // tile

    def body(*refs):
        w_ref, m_ref, v_ref = refs[:3]
        g_refs = refs[3:3 + n_g]
        g_ref, d_ref, nm_ref, nv_ref = refs[3 + n_g:]
        part = pl.program_id(0)
        g = g_refs[0][...]
        for k in range(1, n_g):
            g = jnp.where(part == k, g_refs[k][...], g)
        m_new = ADAM_B1 * m_ref[...] + (1.0 - ADAM_B1) * g
        v_new = ADAM_B2 * v_ref[...] + (1.0 - ADAM_B2) * (g * g)
        m_hat = m_new / (1.0 - ADAM_B1 ** ADAM_STEP)
        v_hat = v_new / (1.0 - ADAM_B2 ** ADAM_STEP)
        g_ref[...] = g
        d_ref[...] = -ADAM_LR * (m_hat / (jnp.sqrt(v_hat) + ADAM_EPS) + ADAM_WD * w_ref[...])
        nm_ref[...] = m_new
        nv_ref[...] = v_new

    blk = pl.BlockSpec((tile, cols), lambda p, i: (p * nb + i, 0))
    g_specs = [pl.BlockSpec((tile, cols), functools.partial(lambda p, i, k: (jnp.where(p == k, i, 0), 0), k=k))
               for k in range(n_g)]
    return pl.pallas_call(
        body, name=name, grid=(n_g, nb), in_specs=[blk] * 3 + g_specs, out_specs=[blk] * 4,
        out_shape=[jax.ShapeDtypeStruct((rows, cols), F32)] * 4,
        compiler_params=_params(("arbitrary", "arbitrary")),
    )(w, m, v, *g_parts)


_ORIG_SEGMENTS = (
    ("qkv_a", 0, 3072), ("z_a", 3072, 1024), ("beta", 4096, 8), ("alpha", 4104, 8), ("glu", 4112, 2048),
    ("z_b", 6160, 1024), ("qkv_c", 7184, 3072), ("z_c", 10256, 1024), ("forget", 11280, 8), ("gate", 11288, 6144))
_PAD_ORDER = ("gate", "qkv_a", "z_a", "glu", "z_b", "qkv_c", "z_c", "beta", "alpha", "forget")


def _pad_cols(w):
    seg = {n: w[..., s:s + k] for n, s, k in _ORIG_SEGMENTS}
    fill = jnp.zeros(w.shape[:-1] + (N_PAD - N_IN,), w.dtype)
    return jnp.concatenate([seg[n] for n in _PAD_ORDER] + [fill], axis=-1)


def _unpad_cols(g):
    off, seg = 0, {}
    widths = {n: k for n, _, k in _ORIG_SEGMENTS}
    for n in _PAD_ORDER:
        seg[n] = g[..., off:off + widths[n]]
        off += widths[n]
    return jnp.concatenate([seg[n] for n, _, _ in _ORIG_SEGMENTS], axis=-1)


def _lane_row(vals, lane0):
    return jnp.pad(vals.astype(F32), (lane0, LANE - HEADS - lane0))[None]


def _layer_fwd(x, p, comm=None):
    seq = x.shape[0]
    h = _rowwise("rms_pre", _rms_pre_fn, [(x, D_MODEL, 0)], [p["pre_w"]], [(D_MODEL, BF16)], 256)[0]
    proj = _matmul("mm_in", h, p["w_in"], "nn", F32, 512, 1280, 2048)
    ca = _conv_fwd("conv_a", proj, OFF_QKVA, 3 * BRANCH, p["w4"], jnp.zeros((1, 3 * BRANCH), F32), SHORT_CONV)
    y_a, states = _gdr_fwd("gdr_fwd", ca, proj, p["alog"], p["dtb"], p["onw"])
    u2 = _conv_fwd("conv_b", proj, OFF_VAL, BRANCH, p["w31"], p["cb"], CONF_CONV, gate_off=OFF_GLUG)
    y_b = _rowwise("ln_gate", _ln_gate_fn, [(u2, BRANCH, 0), (proj, BRANCH, OFF_ZB // BRANCH)],
                   [p["ln_w"], p["ln_b"]], [(BRANCH, BF16)], 256)[0]
    c, ct = _fox_gate_fwd("fox_gate", proj, p["fb"])
    ct3 = ct.reshape(LANE, 1, seq)
    y_c, o_c, lse, got = _attn_fwd("attn_fwd", proj, c, ct3, comm)
    ys = (y_a, y_b, y_c)
    br = [_matmul("mm_br", ys[n], p["wbr"][n], "nn", F32, 512, 2048, 1024) for n in range(N_BRANCH)]
    merged = _rowwise("merge", _merge_fn, [(proj, D_MODEL, n) for n in range(N_BRANCH)] + [(b, D_MODEL, 0) for b in br],
                      [], [(D_MODEL, BF16)], 256)[0]
    out = _matmul("mm_out", merged, p["wout"], "nn", F32, 512, 2048, 2048)
    x_new = _rowwise("rms_post", _rms_post_fn, [(out, D_MODEL, 0), (x, D_MODEL, 0)], [p["post_w"]],
                     [(D_MODEL, F32)], 256)[0]
    saved = dict(x=x, ht=h.T, proj=proj, ca=ca, states=states, u2=u2, c=c, ct3=ct3, o_c=o_c, lse=lse, ys=ys, br=br,
                 merged=merged, out=out)
    return x_new, saved, got


def _layer_bwd(dxn, p, sv, comm=None):
    x, proj = sv["x"], sv["proj"]
    seq = x.shape[0]
    g = {}
    d_out, g["post_w"] = _rowwise_bwd("rms_post_bwd", _rms_only_fn, [(sv["out"], D_MODEL, 0)], [p["post_w"]], [dxn],
                                      [BF16], 256)
    d_merged = _matmul("mm_out_dx", d_out, p["wout"], "nt", F32, 512, 2048, 2048)
    g["wout"] = _matmul("mm_out_dw", sv["merged"], d_out, "tn", F32, 1024, 1024, 512)
    rows = [(proj, D_MODEL, n) for n in range(N_BRANCH)] + [(b, D_MODEL, 0) for b in sv["br"]]
    d_gl0, d_gl1, d_gl2, d_b0, d_b1, d_b2 = _rowwise_bwd("merge_bwd", _merge_fn, rows, [], [d_merged], [BF16] * 6, 128)
    d_br = (d_b0, d_b1, d_b2)
    dys = [_matmul("mm_br_dx", d_br[n], p["wbr"][n], "nt", BF16, 512, 1024, 2048) for n in range(N_BRANCH)]
    g["wbr"] = jnp.stack([_matmul("mm_br_dw", sv["ys"][n], d_br[n], "tn", F32, 1024, 1024, 512)
                          for n in range(N_BRANCH)])
    dq, dzc = _attn_dq("attn_dq", dys[2], sv["o_c"], sv["lse"], proj, sv["c"], sv["ct3"])
    dk, dv, dck, got = _attn_dkv("attn_dkv", dys[2], sv["o_c"], sv["lse"], proj, sv["c"], sv["ct3"], comm)
    dsm_c, g["fb"] = _fox_gate_bwd("fox_gate_bwd", dck.reshape(HEADS, seq), proj, p["fb"])
    du2, dzb, g["ln_w"], g["ln_b"] = _rowwise_bwd(
        "ln_gate_bwd", _ln_gate_fn, [(sv["u2"], BRANCH, 0), (proj, BRANCH, OFF_ZB // BRANCH)], [p["ln_w"], p["ln_b"]],
        [dys[1]], [F32, BF16], 256)
    dval, dgate, g["w31"], g["cb"] = _conv_bwd("conv_b_bwd", du2, proj, OFF_VAL, BRANCH, p["w31"], CONF_CONV,
                                               gate_off=OFF_GLUG)
    dca, dza, dsm_a, g["alog"], g["dtb"], g["onw"] = _gdr_bwd("gdr_bwd", dys[0], sv["states"], sv["ca"], proj,
                                                              p["alog"], p["dtb"], p["onw"])
    dqkva, g["w4"], _ = _conv_bwd("conv_a_bwd", dca, proj, OFF_QKVA, 3 * BRANCH, p["w4"], SHORT_CONV)
    d_small = jnp.pad((dsm_a + dsm_c).astype(BF16), ((0, 0), (0, N_PAD - OFF_SMALL - LANE)))
    d_proj = jnp.concatenate([d_gl0, d_gl1, d_gl2, dqkva, dza, dval, dgate, dzb, dq, dk, dv, dzc, d_small], axis=1)
    dh = _matmul("mm_in_dx", d_proj, p["w_in"], "nt", F32, 512, 2048, 2560)
    g["w_in"] = _matmul("mm_in_dw", sv["ht"], d_proj, "nn", F32, 2048, 640, 2048)
    dx, g["pre_w"] = _rowwise_bwd("rms_pre_bwd", _rms_pre_res_fn, [(x, D_MODEL, 0)], [p["pre_w"]], [dh, dxn], [F32], 256)
    return dx, g, got


N_CHIPS = 4
N_DEV = 8
HBM_SPEC = pl.BlockSpec(memory_space=pltpu.HBM)


def _mesh_pos():
    return lax.axis_index("x"), lax.axis_index("y"), lax.axis_index("c")


def _other_chips(x, y):
    return [(1 - x, y), (x, 1 - y), (1 - x, 1 - y)]


def _comm_plan(kind, arrs):
    n = len(arrs)
    if kind == "allgather":
        return ([jax.ShapeDtypeStruct((3,) + a.shape, a.dtype) for a in arrs], [pltpu.SemaphoreType.DMA((3 * n,))] * 4)
    return ([jax.ShapeDtypeStruct((3,) + a.shape[1:], a.dtype) for a in arrs], [pltpu.SemaphoreType.DMA((3 * n,))] * 2)


def _comm_ops(kind, ins, outs, sems):
    return (_allgather_ops if kind == "allgather" else _reduce_scatter_ops)(ins, outs, sems)


def _allgather_ops(ins, outs, sems):
    send_sems, recv_sems, pass_send, pass_recv = sems
    n = len(ins)
    x, y, c = _mesh_pos()

    def part(a, core):
        half = ins[a].shape[0] // 2
        return pl.ds(half * core, half)

    def ici(a, j):
        px, py = _other_chips(x, y)[j]
        return pltpu.make_async_remote_copy(
            src_ref=ins[a].at[part(a, c)], dst_ref=outs[a].at[j, part(a, c)], send_sem=send_sems.at[3 * a + j],
            recv_sem=recv_sems.at[3 * a + j], device_id=(px, py, c), device_id_type=MESH)

    def d2d(a, j, core):
        blk = outs[a].at[j, part(a, core)]
        return pltpu.make_async_remote_copy(
            src_ref=blk, dst_ref=blk, send_sem=pass_send.at[3 * a + j], recv_sem=pass_recv.at[3 * a + j],
            device_id=(x, y, 1 - c), device_id_type=MESH)

    def start():
        for a in range(n):
            for j in range(3):
                ici(a, j).start()

    def finish():
        for a in range(n):
            for j in range(3):
                ici(a, j).wait_recv()
                d2d(a, j, c).start()
        for a in range(n):
            for j in range(3):
                d2d(a, j, 1 - c).wait_recv()
        for a in range(n):
            for j in range(3):
                ici(a, j).wait_send()
                d2d(a, j, c).wait_send()

    return start, finish


def _reduce_scatter_ops(ins, outs, sems):
    send_sems, recv_sems = sems
    n = len(ins)
    x, y, c = _mesh_pos()

    def remote(a, j):
        px, py = _other_chips(x, y)[j]
        return pltpu.make_async_remote_copy(
            src_ref=ins[a].at[2 * px + py], dst_ref=outs[a].at[j], send_sem=send_sems.at[3 * a + j],
            recv_sem=recv_sems.at[3 * a + j], device_id=(px, py, c), device_id_type=MESH)

    def start():
        for a in range(n):
            for j in range(3):
                remote(a, j).start()

    def finish():
        for a in range(n):
            for j in range(3):
                remote(a, j).wait_recv()
        for a in range(n):
            for j in range(3):
                remote(a, j).wait_send()

    return start, finish


def _exchange_chips(name, kind, arrs):
    n = len(arrs)
    shapes, sems = _comm_plan(kind, arrs)

    def body(*refs):
        start, finish = _comm_ops(kind, refs[:n], refs[n:2 * n], refs[2 * n:])
        start()
        finish()

    return pl.pallas_call(body, name=name, in_specs=[HBM_SPEC] * n, out_specs=[HBM_SPEC] * n, out_shape=shapes,
                          scratch_shapes=sems)(*arrs)


def _halves_to_sibling(name, arrs):
    n = len(arrs)

    def body(*refs):
        ins, outs = refs[:n], refs[n:2 * n]
        send_sems, recv_sems = refs[2 * n:]
        x, y, c = _mesh_pos()
        copies = []
        for a in range(n):
            rows = ins[a].shape[1] // 2
            for s in range(N_CHIPS):
                copies.append(pltpu.make_async_remote_copy(
                    src_ref=ins[a].at[s, pl.ds((1 - c) * rows, rows)], dst_ref=outs[a].at[s],
                    send_sem=send_sems.at[N_CHIPS * a + s], recv_sem=recv_sems.at[N_CHIPS * a + s],
                    device_id=(x, y, 1 - c), device_id_type=MESH))
        for cp in copies:
            cp.start()
        for cp in copies:
            cp.wait_recv()
        for cp in copies:
            cp.wait_send()

    return pl.pallas_call(
        body, name=name, in_specs=[HBM_SPEC] * n, out_specs=[HBM_SPEC] * n,
        out_shape=[jax.ShapeDtypeStruct((a.shape[0], a.shape[1] // 2, a.shape[2]), a.dtype) for a in arrs],
        scratch_shapes=[pltpu.SemaphoreType.DMA((N_CHIPS * n,)), pltpu.SemaphoreType.DMA((N_CHIPS * n,))],
    )(*arrs)


def _add_own_half(name, full, other, core, tile):
    n, rows, cols = other.shape
    tile = min(tile, rows)
    assert rows % tile == 0, (name, other.shape)
    nb = rows // tile

    def body(c_ref, f_ref, o_ref, out_ref):
        out_ref[...] = (f_ref[...] + o_ref[...]).astype(out_ref.dtype)

    return pl.pallas_call(
        body, name=name,
        grid_spec=pltpu.PrefetchScalarGridSpec(
            num_scalar_prefetch=1, grid=(n, nb),
            in_specs=[pl.BlockSpec((1, tile, cols), lambda s, i, c_ref: (s, c_ref[0] * nb + i, 0)),
                      pl.BlockSpec((1, tile, cols), lambda s, i, c_ref: (s, i, 0))],
            out_specs=pl.BlockSpec((1, tile, cols), lambda s, i, c_ref: (s, i, 0))),
        out_shape=jax.ShapeDtypeStruct(other.shape, BF16),
        compiler_params=_params(("parallel", "parallel")),
    )(core, full, other)


def _join_cores(name, arrs):
    n = len(arrs)

    def body(*refs):
        bufs = refs[n:2 * n]
        send_sems, recv_sems = refs[2 * n:]
        x, y, c = _mesh_pos()

        def copy(a, slot):
            return pltpu.make_async_remote_copy(
                src_ref=bufs[a].at[slot], dst_ref=bufs[a].at[slot], send_sem=send_sems.at[a], recv_sem=recv_sems.at[a],
                device_id=(x, y, 1 - c), device_id_type=MESH)

        for a in range(n):
            copy(a, c).start()
        for a in range(n):
            copy(a, 1 - c).wait_recv()
        for a in range(n):
            copy(a, c).wait_send()

    return pl.pallas_call(
        body, name=name, in_specs=[HBM_SPEC] * n, out_specs=[HBM_SPEC] * n,
        out_shape=[jax.ShapeDtypeStruct(a.shape, a.dtype) for a in arrs],
        input_output_aliases={a: a for a in range(n)},
        scratch_shapes=[pltpu.SemaphoreType.DMA((n,)), pltpu.SemaphoreType.DMA((n,))],
    )(*arrs)


def _sum_chips(name, own, recv, chip, core, tile):
    _, rows, cols = recv.shape
    tile = min(tile, rows)
    assert rows % tile == 0, (name, recv.shape)

    def body(chip_ref, core_ref, own_ref, recv_ref, o_ref):
        f32 = lambda a: a.astype(F32)
        o_ref[0] = ((f32(own_ref[0]) + f32(recv_ref[0])) + f32(recv_ref[1])) + f32(recv_ref[2])

    return pl.pallas_call(
        body, name=name,
        grid_spec=pltpu.PrefetchScalarGridSpec(
            num_scalar_prefetch=2, grid=(rows // tile,),
            in_specs=[pl.BlockSpec((1, tile, cols), lambda i, chip_ref, core_ref: (chip_ref[0], i, 0)),
                      pl.BlockSpec((3, tile, cols), lambda i, chip_ref, core_ref: (0, i, 0))],
            out_specs=pl.BlockSpec((1, tile, cols), lambda i, chip_ref, core_ref: (core_ref[0], i, 0))),
        out_shape=jax.ShapeDtypeStruct((2, rows, cols), F32),
        compiler_params=_params(("parallel",)),
    )(chip, core, own, recv)


def _allgather_devices(name, buf):
    def body(in_ref, out_ref, send_sems, recv_sems):
        x, y, c = _mesh_pos()
        me = 4 * x + 2 * y + c
        out_ref[me] = in_ref[...]

        def remote(k, slot):
            peer = (x ^ (k >> 2), y ^ ((k >> 1) & 1), c ^ (k & 1))
            return pltpu.make_async_remote_copy(
                src_ref=in_ref, dst_ref=out_ref.at[slot], send_sem=send_sems.at[k - 1], recv_sem=recv_sems.at[k - 1],
                device_id=peer, device_id_type=MESH)

        for k in range(1, N_DEV):
            remote(k, me).start()
        for k in range(1, N_DEV):
            remote(k, me ^ k).wait_recv()
        for k in range(1, N_DEV):
            remote(k, me).wait_send()

    vmem = pl.BlockSpec(memory_space=pltpu.VMEM)
    return pl.pallas_call(
        body, name=name, in_specs=[vmem], out_specs=vmem,
        out_shape=jax.ShapeDtypeStruct((N_DEV,) + buf.shape, buf.dtype),
        scratch_shapes=[pltpu.SemaphoreType.DMA((N_DEV - 1,)), pltpu.SemaphoreType.DMA((N_DEV - 1,))],
    )(buf)


def _sum_slots(name, a, tile):
    n, rows, cols = a.shape
    tile = min(tile, rows)
    assert rows % tile == 0, (name, a.shape)

    def body(a_ref, o_ref):
        acc = a_ref[0].astype(F32)
        for i in range(1, n):
            acc = acc + a_ref[i].astype(F32)
        o_ref[...] = acc

    return pl.pallas_call(
        body, name=name, grid=(rows // tile,),
        in_specs=[pl.BlockSpec((n, tile, cols), lambda i: (0, i, 0))],
        out_specs=pl.BlockSpec((tile, cols), lambda i: (i, 0)),
        out_shape=jax.ShapeDtypeStruct((rows, cols), F32),
        compiler_params=_params(("parallel",)),
    )(a)


_SMALL = (
    ("pre_norm_w", (D_MODEL,)), ("post_norm_w", (D_MODEL,)), ("a_log", (HEADS,)), ("dt_bias", (HEADS,)),
    ("o_norm_w", (HEAD_DIM,)), ("conv_b", (BRANCH,)), ("ln_w", (BRANCH,)), ("ln_b", (BRANCH,)), ("f_bias", (HEADS,)),
    ("conv_qkv_w", (SHORT_CONV, 3 * BRANCH)), ("conv_w", (CONF_CONV, BRANCH)))


def _small_grads(g):
    heads = lambda a, lane0: a[0, lane0:lane0 + HEADS]
    return jnp.concatenate([
        g["pre_w"][0], g["post_w"][0], heads(g["alog"], ALPHA_LANE), heads(g["dtb"], ALPHA_LANE), g["onw"][0],
        g["cb"][0], g["ln_w"][0], g["ln_b"][0], heads(g["fb"], FORGET_LANE), g["w4"][:SHORT_CONV].reshape(-1),
        g["w31"][:CONF_CONV].reshape(-1)])
_SHARDED_SMALL = {"conv_qkv_w": 3 * BRANCH // N_CHIPS, "conv_w": BRANCH // N_CHIPS}
_WEIGHTS = ("pre_norm_w", "post_norm_w", "w_in", "conv_qkv_w", "a_log", "dt_bias", "o_norm_w", "conv_w", "conv_b",
            "ln_w", "ln_b", "f_bias", "w_branch", "w_out")


def _pack(parts):
    flat = jnp.concatenate([p.reshape(-1).astype(F32) for p in parts])
    rows = -(-flat.shape[0] // (8 * LANE)) * 8
    return jnp.pad(flat, (0, rows * LANE - flat.shape[0])).reshape(rows, LANE)


def _unpack(buf, shapes):
    flat, out, off = buf.reshape(-1), [], 0
    for shp in shapes:
        size = 1
        for s in shp:
            size *= s
        out.append(flat[off:off + size].reshape(shp))
        off += size
    return out


def kernel(x, pre_norm_w, post_norm_w, w_in, conv_qkv_w, a_log, dt_bias, o_norm_w, conv_w, conv_b, ln_w, ln_b, f_bias, w_branch, w_out, loss_target, m_pre_norm_w, m_post_norm_w, m_w_in, m_conv_qkv_w, m_a_log, m_dt_bias, m_o_norm_w, m_conv_w, m_conv_b, m_ln_w, m_ln_b, m_f_bias, m_w_branch, m_w_out, v_pre_norm_w, v_post_norm_w, v_w_in, v_conv_qkv_w, v_a_log, v_dt_bias, v_o_norm_w, v_conv_w, v_conv_b, v_ln_w, v_ln_b, v_f_bias, v_w_branch, v_w_out):
    weights = dict(pre_norm_w=pre_norm_w, post_norm_w=post_norm_w, w_in=w_in, conv_qkv_w=conv_qkv_w, a_log=a_log,
                   dt_bias=dt_bias, o_norm_w=o_norm_w, conv_w=conv_w, conv_b=conv_b, ln_w=ln_w, ln_b=ln_b, f_bias=f_bias,
                   w_branch=w_branch, w_out=w_out)
    mom1 = dict(pre_norm_w=m_pre_norm_w, post_norm_w=m_post_norm_w, w_in=m_w_in, conv_qkv_w=m_conv_qkv_w, a_log=m_a_log,
                dt_bias=m_dt_bias, o_norm_w=m_o_norm_w, conv_w=m_conv_w, conv_b=m_conv_b, ln_w=m_ln_w, ln_b=m_ln_b,
                f_bias=m_f_bias, w_branch=m_w_branch, w_out=m_w_out)
    mom2 = dict(pre_norm_w=v_pre_norm_w, post_norm_w=v_post_norm_w, w_in=v_w_in, conv_qkv_w=v_conv_qkv_w, a_log=v_a_log,
                dt_bias=v_dt_bias, o_norm_w=v_o_norm_w, conv_w=v_conv_w, conv_b=v_conv_b, ln_w=v_ln_w, ln_b=v_ln_b,
                f_bias=v_f_bias, w_branch=v_w_branch, w_out=v_w_out)
    chip = 2 * lax.axis_index("x") + lax.axis_index("y")
    core = lax.axis_index("c").astype(jnp.int32).reshape(1)
    chip_id = chip.astype(jnp.int32).reshape(1)

    w_in_b, w_out_b = w_in.astype(BF16), w_out.astype(BF16)
    w_br_b = w_branch.astype(BF16).reshape(DEPTH, N_BRANCH * BRANCH, D_MODEL // N_CHIPS)
    shards = lambda l: [w_in_b[l], w_br_b[l], w_out_b[l]]

    def whole(own, got, axis=-1):
        parts = []
        for s in range(N_CHIPS):
            d = chip ^ s
            parts.append(jnp.where(d == 0, own, jnp.where(d == 2, got[0], jnp.where(d == 1, got[1], got[2]))))
        return jnp.concatenate(parts, axis=axis)

    first = _exchange_chips("allgather_weights", "allgather", shards(0) + [conv_qkv_w, conv_w])
    c4_full = jnp.pad(whole(conv_qkv_w, first[3]), ((0, 0), (0, 8 - SHORT_CONV), (0, 0)))
    c31_full = jnp.pad(whole(conv_w, first[4]), ((0, 0), (0, 32 - CONF_CONV), (0, 0)))

    def layer_params(l, got):
        return dict(
            pre_w=pre_norm_w[l][None], post_w=post_norm_w[l][None], w_in=_pad_cols(whole(w_in_b[l], got[0])),
            w4=c4_full[l], alog=_lane_row(a_log[l], ALPHA_LANE), dtb=_lane_row(dt_bias[l], ALPHA_LANE),
            onw=o_norm_w[l][None], w31=c31_full[l], cb=conv_b[l][None], ln_w=ln_w[l][None], ln_b=ln_b[l][None],
            fb=_lane_row(f_bias[l], FORGET_LANE),
            wbr=whole(w_br_b[l], got[1]).reshape(N_BRANCH, BRANCH, D_MODEL), wout=whole(w_out_b[l], got[2], axis=0))

    act = x[0]
    got = first[:3]
    layers, saved = [], []
    for l in range(DEPTH):
        layers.append(layer_params(l, got))
        act, sv, got = _layer_fwd(act, layers[l], ("allgather", shards(l + 1)) if l + 1 < DEPTH else None)
        saved.append(sv)
    d_act, loss_blk = _loss_head("loss_head", act, loss_target[0])

    parts_in, parts_br, parts_out, small_g = [], [], [], []

    def finish_reduce(own, recv):
        mine = [_sum_chips("sum_chips", o, r, chip_id, core, t) for o, r, t in zip(own, recv, (64, 512, 128))]
        joined = _join_cores("join_cores", mine)
        for lst, j in zip((parts_in, parts_br, parts_out), joined):
            lst.append(j.reshape(2 * j.shape[1], j.shape[2]))

    pending = None
    for l in reversed(range(DEPTH)):
        d_act, g, recv = _layer_bwd(d_act, layers[l], saved[l], ("reduce_scatter", pending) if pending else None)
        if pending:
            finish_reduce(pending, recv)
        g_in = _unpad_cols(g["w_in"]).reshape(D_MODEL, N_CHIPS, N_IN // N_CHIPS).transpose(1, 0, 2)
        g_br = g["wbr"].reshape(N_BRANCH * BRANCH, N_CHIPS, D_MODEL // N_CHIPS).transpose(1, 0, 2)
        g_out = g["wout"].reshape(N_CHIPS, D_MODEL // N_CHIPS, D_MODEL)
        parts = [g_in, g_br, g_out]
        other = _halves_to_sibling("halves_to_sibling", parts)
        pending = [_add_own_half("add_own_half", f, o, core, t) for f, o, t in zip(parts, other, (64, 512, 128))]
        small_g.append(_small_grads(g))
    finish_reduce(pending, _exchange_chips("reduce_scatter_grads", "reduce_scatter", pending))
    small_g = jnp.stack(small_g[::-1])
    parts_in, parts_br, parts_out = parts_in[::-1], parts_br[::-1], parts_out[::-1]

    gathered = _allgather_devices("allgather_small", _pack([small_g, loss_blk[0, 0:1]]))
    summed = _sum_slots("sum_devices", gathered, gathered.shape[1]).reshape(-1)
    loss = summed[small_g.size]
    summed = summed[:small_g.size].reshape(small_g.shape)
    total, off = {}, 0
    for n, shape in _SMALL:
        size = shape[0] * (shape[1] if len(shape) > 1 else 1)
        total[n] = summed[:, off:off + size].reshape((DEPTH,) + shape)
        off += size
    for n, width in _SHARDED_SMALL.items():
        total[n] = lax.dynamic_slice_in_dim(total[n], chip * width, width, axis=2)

    names = list(total)
    packed = [_pack([d[n] for n in names]) for d in (weights, mom1, mom2)]
    res = _adamw("adamw_small", packed[0], packed[1], packed[2], [_pack([total[n] for n in names])], packed[0].shape[0])
    shapes = [weights[n].shape for n in names]
    grads, delta, new_m, new_v = [dict(zip(names, _unpack(r, shapes))) for r in res]
    big = (("w_in", parts_in, (DEPTH * D_MODEL, N_IN // N_CHIPS), 64),
           ("w_branch", parts_br, (DEPTH * N_BRANCH * BRANCH, D_MODEL // N_CHIPS), 512),
           ("w_out", parts_out, (DEPTH * D_MODEL // N_CHIPS, D_MODEL), 128))
    for n, parts, shape2, tile in big:
        res = _adamw("adamw_" + n, weights[n].reshape(shape2), mom1[n].reshape(shape2), mom2[n].reshape(shape2),
                     parts, tile)
        grads[n], delta[n], new_m[n], new_v[n] = [r.reshape(weights[n].shape) for r in res]

    outs = [loss, d_act[None]]
    for d in (grads, delta, new_m, new_v):
        outs += [d[n] for n in _WEIGHTS]
    return tuple(outs)
```

```python
import functools

import jax
import jax.numpy as jnp
from jax import lax
from jax.experimental import pallas as pl
from jax.experimental.pallas import tpu as pltpu

F32 = jnp.float32
BF16 = jnp.bfloat16
MXU_DTYPE = jnp.bfloat16

D_MODEL = 2048
DEPTH = 4
BRANCH = 1024
HEAD_DIM = 128
HEADS = 8
CHUNK = 64
SHORT_CONV = 4
CONF_CONV = 31
N_BRANCH = 3
NORM_EPS = 1e-6
N_IN = 17432

OFF_GATE = 0
OFF_QKVA = 6144
OFF_ZA = 9216
OFF_VAL = 10240
OFF_GLUG = 11264
OFF_ZB = 12288
OFF_QKVC = 13312
OFF_ZC = 16384
OFF_SMALL = 17408
N_PAD = 17920
LANE = 128
BETA_LANE, ALPHA_LANE, FORGET_LANE = 0, 8, 16

ADAM_LR = 0.001
ADAM_B1 = 0.9
ADAM_B2 = 0.999
ADAM_EPS = 1e-08
ADAM_WD = 0.01
ADAM_STEP = 10

VMEM_LIMIT = 56 * 1024 * 1024

NN = (((1,), (0,)), ((), ()))
NT = (((1,), (1,)), ((), ()))
TN = (((0,), (0,)), ((), ()))
MESH = pl.DeviceIdType.MESH


def _params(sem=None):
    return pltpu.CompilerParams(dimension_semantics=sem, vmem_limit_bytes=VMEM_LIMIT)


def _dot(a, b, dims=NN):
    return lax.dot_general(a.astype(MXU_DTYPE), b.astype(MXU_DTYPE), dims, preferred_element_type=F32)


def _dot_hi(a, b, dims=NN):
    return lax.dot_general(a, b, dims, precision=lax.Precision.HIGHEST, preferred_element_type=F32)


def _dot_3x(a, b, dims=NN):
    a_hi, b_hi = a.astype(BF16), b.astype(BF16)
    a_lo, b_lo = (a - a_hi.astype(F32)).astype(BF16), (b - b_hi.astype(F32)).astype(BF16)
    dot = lambda u, v: lax.dot_general(u, v, dims, preferred_element_type=F32)
    return dot(a_hi, b_hi) + (dot(a_hi, b_lo) + dot(a_lo, b_hi))


def _sigmoid(x):
    return 1.0 / (1.0 + jnp.exp(-x))


def _silu(x):
    return x * _sigmoid(x)


def _softplus(x):
    return jnp.maximum(x, 0.0) + jnp.log(1.0 + jnp.exp(-jnp.abs(x)))


def _colsel(m, j):
    lane = lax.broadcasted_iota(jnp.int32, m.shape, 1)
    return jnp.sum(jnp.where(lane == j, m, 0.0), axis=1, keepdims=True)


def _rowsel(m, j):
    sub = lax.broadcasted_iota(jnp.int32, m.shape, 0)
    return jnp.sum(jnp.where(sub == j, m, 0.0), axis=0, keepdims=True)


def _row_specs(rows, tile):
    return [pl.BlockSpec((tile, w), functools.partial(lambda i, cb: (i, cb), cb=cb)) for (_, w, cb) in rows]


def _rowwise(name, fn, rows, params, outs, tile):
    seq = rows[0][0].shape[0]
    tile = min(tile, seq)
    n_in = len(rows) + len(params)

    def body(*refs):
        res = fn(*[r[...] for r in refs[:n_in]])
        for o_ref, r in zip(refs[n_in:], res):
            o_ref[...] = r.astype(o_ref.dtype)

    return pl.pallas_call(
        body, name=name, grid=(seq // tile,),
        in_specs=_row_specs(rows, tile) + [pl.BlockSpec(p.shape, lambda i: (0, 0)) for p in params],
        out_specs=[pl.BlockSpec((tile, w), lambda i: (i, 0)) for (w, _) in outs],
        out_shape=[jax.ShapeDtypeStruct((seq, w), dt) for (w, dt) in outs],
        compiler_params=_params(("parallel",)),
    )(*[r[0] for r in rows], *params)


def _rowwise_bwd(name, fn, rows, params, cts, row_grads, tile):
    seq = rows[0][0].shape[0]
    tile = min(tile, seq)
    nr, npar, nct = len(rows), len(params), len(cts)
    n_in = nr + npar

    def body(*refs):
        vals = [r[...] for r in refs[:n_in]]
        res, vjp = jax.vjp(fn, *vals)
        grads = vjp(tuple(c[...].astype(r.dtype) for c, r in zip(refs[n_in:n_in + nct], res)))
        outs = refs[n_in + nct:]
        k = 0
        for idx, dt in enumerate(row_grads):
            if dt is not None:
                outs[k][...] = grads[idx].astype(dt)
                k += 1
        first = pl.program_id(0) == 0
        for j in range(npar):
            g = grads[nr + j].astype(F32)
            o_ref = outs[k + j]

            @pl.when(first)
            def _(o_ref=o_ref, g=g):
                o_ref[...] = g

            @pl.when(jnp.logical_not(first))
            def _(o_ref=o_ref, g=g):
                o_ref[...] += g

    want = [(rows[i][1], dt) for i, dt in enumerate(row_grads) if dt is not None]
    return pl.pallas_call(
        body, name=name, grid=(seq // tile,),
        in_specs=(_row_specs(rows, tile) + [pl.BlockSpec(p.shape, lambda i: (0, 0)) for p in params]
                  + [pl.BlockSpec((tile, c.shape[1]), lambda i: (i, 0)) for c in cts]),
        out_specs=([pl.BlockSpec((tile, w), lambda i: (i, 0)) for (w, _) in want]
                   + [pl.BlockSpec(p.shape, lambda i: (0, 0)) for p in params]),
        out_shape=([jax.ShapeDtypeStruct((seq, w), dt) for (w, dt) in want]
                   + [jax.ShapeDtypeStruct(p.shape, F32) for p in params]),
        compiler_params=_params(("arbitrary",)),
    )(*[r[0] for r in rows], *params, *cts)


def _rms(x, w):
    x = x.astype(F32)
    return x * lax.rsqrt(jnp.mean(x * x, axis=-1, keepdims=True) + NORM_EPS) * w


def _rms_pre_fn(x, w):
    return (_rms(x, w),)


def _rms_pre_res_fn(x, w):
    return (_rms(x, w), x)


def _rms_post_fn(out, x, w):
    return (x + _rms(out, w),)


def _rms_only_fn(out, w):
    return (_rms(out, w),)


def _ln_gate_fn(u, z, w, b):
    u = u.astype(F32)
    uc = u - jnp.mean(u, axis=-1, keepdims=True)
    y = uc * lax.rsqrt(jnp.mean(uc * uc, axis=-1, keepdims=True) + NORM_EPS) * w + b
    return (_silu(y) * _silu(z.astype(F32)),)


def _merge_fn(g0, g1, g2, b0, b1, b2):
    return (_sigmoid(g0) * b0 + _sigmoid(g1) * b1 + _sigmoid(g2) * b2,)


def _matmul(name, a, b, mode, out_dtype, tm, tn, tk):
    if mode == "nn":
        (m, kc), n = a.shape, b.shape[1]
    elif mode == "nt":
        (m, kc), n = a.shape, b.shape[0]
    else:
        (kc, m), n = a.shape, b.shape[1]
    tm, tn, tk = min(tm, m), min(tn, n), min(tk, kc)
    nk = kc // tk
    assert m % tm == 0 and n % tn == 0 and kc % tk == 0, (name, a.shape, b.shape)
    dims = {"nn": NN, "nt": NT, "tn": TN}[mode]
    a_spec = (pl.BlockSpec((tk, tm), lambda j, i, k: (k, i)) if mode == "tn"
              else pl.BlockSpec((tm, tk), lambda j, i, k: (i, k)))
    b_spec = (pl.BlockSpec((tn, tk), lambda j, i, k: (j, k)) if mode == "nt"
              else pl.BlockSpec((tk, tn), lambda j, i, k: (k, j)))
    use_acc = nk > 1 and out_dtype != F32

    def body(a_ref, b_ref, o_ref, *acc):
        p = _dot(a_ref[...], b_ref[...], dims)
        if nk == 1:
            o_ref[...] = p.astype(out_dtype)
            return
        k = pl.program_id(2)
        dst = acc[0] if use_acc else o_ref

        @pl.when(k == 0)
        def _():
            dst[...] = p

        @pl.when(k > 0)
        def _():
            dst[...] += p

        if use_acc:
            @pl.when(k == nk - 1)
            def _():
                o_ref[...] = dst[...].astype(out_dtype)

    return pl.pallas_call(
        body, name=name, grid=(n // tn, m // tm, nk),
        in_specs=[a_spec, b_spec],
        out_specs=pl.BlockSpec((tm, tn), lambda j, i, k: (i, j)),
        out_shape=jax.ShapeDtypeStruct((m, n), out_dtype),
        scratch_shapes=[pltpu.VMEM((tm, tn), F32)] if use_acc else [],
        compiler_params=_params(("parallel", "parallel", "arbitrary")),
    )(a, b)


HALO = 32
CONV_TC = 256
CONV_T = 1024


def _conv_fwd(name, x, x_off, ch, w, b, k_width, gate_off=None):
    seq = x.shape[0]
    t_blk = min(CONV_T, seq)
    tc = CONV_TC
    hb = t_blk // HALO
    xcb = x_off // tc
    has_gate = gate_off is not None

    def body(*refs):
        if has_gate:
            xm_ref, xh_ref, gm_ref, gh_ref, w_ref, b_ref, y_ref, win = refs
        else:
            xm_ref, xh_ref, w_ref, b_ref, y_ref, win = refs
        t = pl.program_id(1)
        xm, xh = xm_ref[...], xh_ref[...]
        if has_gate:
            xm = xm * _sigmoid(gm_ref[...])
            xh = xh * _sigmoid(gh_ref[...])
        win[0:HALO, :] = jnp.where(t == 0, 0.0, xh)
        win[HALO:HALO + t_blk, :] = xm
        acc = jnp.broadcast_to(b_ref[...], (t_blk, tc))
        for k in range(k_width):
            acc = acc + w_ref[k:k + 1, :] * win[HALO - (k_width - 1) + k:HALO - (k_width - 1) + k + t_blk, :]
        y_ref[...] = acc

    main = lambda off: pl.BlockSpec((t_blk, tc), lambda c, t: (t, off + c))
    halo = lambda off: pl.BlockSpec((HALO, tc), lambda c, t: (jnp.maximum(t * hb - 1, 0), off + c))
    ins, specs = [x, x], [main(xcb), halo(xcb)]
    if has_gate:
        gcb = gate_off // tc
        ins += [x, x]
        specs += [main(gcb), halo(gcb)]
    ins += [w, b]
    specs += [pl.BlockSpec((w.shape[0], tc), lambda c, t: (0, c)), pl.BlockSpec((1, tc), lambda c, t: (0, c))]
    return pl.pallas_call(
        body, name=name, grid=(ch // tc, seq // t_blk), in_specs=specs,
        out_specs=pl.BlockSpec((t_blk, tc), lambda c, t: (t, c)),
        out_shape=jax.ShapeDtypeStruct((seq, ch), F32),
        scratch_shapes=[pltpu.VMEM((HALO + t_blk, tc), F32)],
        compiler_params=_params(("parallel", "arbitrary")),
    )(*ins)


def _conv_bwd(name, dy, x, x_off, ch, w, k_width, gate_off=None):
    seq = x.shape[0]
    t_blk = min(CONV_T, seq)
    tc = CONV_TC
    hb = t_blk // HALO
    nt = seq // t_blk
    xcb = x_off // tc
    has_gate = gate_off is not None
    kp = w.shape[0]

    def body(*refs):
        if has_gate:
            dm_ref, dh_ref, xm_ref, xh_ref, gm_ref, gh_ref, w_ref, dv_ref, dg_ref, dw_ref, db_ref, winx, wind = refs
        else:
            dm_ref, dh_ref, xm_ref, xh_ref, w_ref, dx_ref, dw_ref, db_ref, winx, wind = refs
        t = pl.program_id(1)
        xm, xh = xm_ref[...], xh_ref[...]
        if has_gate:
            sg = _sigmoid(gm_ref[...])
            um = xm * sg
            uh = xh * _sigmoid(gh_ref[...])
        else:
            um, uh = xm, xh
        winx[0:HALO, :] = jnp.where(t == nt - 1, 0.0, uh)
        winx[HALO:HALO + t_blk, :] = um
        dm = dm_ref[...]
        wind[0:t_blk, :] = dm
        wind[t_blk:t_blk + HALO, :] = jnp.where(t == 0, 0.0, dh_ref[...])
        du = jnp.zeros((t_blk, tc), F32)
        for k in range(k_width):
            du = du + w_ref[k:k + 1, :] * wind[k_width - 1 - k:k_width - 1 - k + t_blk, :]
        if has_gate:
            dv_ref[...] = (du * sg).astype(dv_ref.dtype)
            dg_ref[...] = (du * xm * sg * (1.0 - sg)).astype(dg_ref.dtype)
        else:
            dx_ref[...] = du.astype(dx_ref.dtype)

        @pl.when(t == 0)
        def _():
            dw_ref[...] = jnp.zeros_like(dw_ref)
            db_ref[...] = jnp.zeros_like(db_ref)

        for k in range(k_width):
            s0 = HALO - (k_width - 1) + k
            dw_ref[k:k + 1, :] += jnp.sum(dm * winx[s0:s0 + t_blk, :], axis=0, keepdims=True)
        db_ref[...] += jnp.sum(dm, axis=0, keepdims=True)

    rt = lambda t: nt - 1 - t
    main = lambda off: pl.BlockSpec((t_blk, tc), lambda c, t: (rt(t), off + c))
    past = lambda off: pl.BlockSpec((HALO, tc), lambda c, t: (jnp.maximum(rt(t) * hb - 1, 0), off + c))
    future = pl.BlockSpec((HALO, tc), lambda c, t: (jnp.minimum((rt(t) + 1) * hb, seq // HALO - 1), c))
    ins, specs = [dy, dy, x, x], [main(0), future, main(xcb), past(xcb)]
    if has_gate:
        gcb = gate_off // tc
        ins += [x, x]
        specs += [main(gcb), past(gcb)]
    ins += [w]
    specs += [pl.BlockSpec((kp, tc), lambda c, t: (0, c))]
    blk = pl.BlockSpec((t_blk, tc), lambda c, t: (rt(t), c))
    n_dx = 2 if has_gate else 1
    return pl.pallas_call(
        body, name=name, grid=(ch // tc, nt), in_specs=specs,
        out_specs=[blk] * n_dx + [pl.BlockSpec((kp, tc), lambda c, t: (0, c)), pl.BlockSpec((1, tc), lambda c, t: (0, c))],
        out_shape=[jax.ShapeDtypeStruct((seq, ch), BF16)] * n_dx + [jax.ShapeDtypeStruct((kp, ch), F32),
                                                                    jax.ShapeDtypeStruct((1, ch), F32)],
        scratch_shapes=[pltpu.VMEM((HALO + t_blk, tc), F32), pltpu.VMEM((HALO + t_blk, tc), F32)],
        compiler_params=_params(("parallel", "arbitrary")),
    )(*ins)


@jax.custom_vjp
def _inv_unit_lower(lows):
    n = lows[0].shape[0]
    eye = (lax.broadcasted_iota(jnp.int32, (n, n), 0) == lax.broadcasted_iota(jnp.int32, (n, n), 1)).astype(F32)
    accs = [eye - low for low in lows]
    pws = list(lows)
    steps = 1
    while steps * 2 < n:
        pws = [_dot_3x(pw, pw) for pw in pws]
        accs = [acc + _dot_3x(acc, pw) for acc, pw in zip(accs, pws)]
        steps *= 2
    return tuple(accs)


def _inv_fwd(lows):
    ts = _inv_unit_lower(lows)
    return ts, ts


def _inv_bwd(ts, dts):
    left = [_dot_3x(t, dt, TN) for t, dt in zip(ts, dts)]
    return (tuple(-_dot_3x(l, t, NT) for l, t in zip(left, ts)),)


_inv_unit_lower.defvjp(_inv_fwd, _inv_bwd)


def _gdr_chunk(cqkv, z, sm, alog, dtb, onw, state):
    c = cqkv.shape[0]
    hs = range(HEADS)
    ri = lax.broadcasted_iota(jnp.int32, (c, c), 0)
    ci = lax.broadcasted_iota(jnp.int32, (c, c), 1)
    incl, strict = ri >= ci, ri > ci
    beta_all = _sigmoid(sm)
    la_all = -jnp.exp(alog) * _softplus(sm + dtb)
    g_cols = _dot_hi(incl.astype(F32), la_all)
    g_rows = _dot_hi(la_all, (ri <= ci).astype(F32), TN)
    g_end = jnp.sum(la_all, axis=0, keepdims=True)
    act = _silu(cqkv)
    sl = lambda base, h: slice(base + h * HEAD_DIM, base + (h + 1) * HEAD_DIM)
    q = [act[:, sl(0, h)] for h in hs]
    k = [act[:, sl(BRANCH, h)] for h in hs]
    v = [act[:, sl(2 * BRANCH, h)] for h in hs]
    q = [x * lax.rsqrt(jnp.sum(x * x, axis=-1, keepdims=True) + NORM_EPS) * (HEAD_DIM ** -0.5) for x in q]
    k = [x * lax.rsqrt(jnp.sum(x * x, axis=-1, keepdims=True) + NORM_EPS) for x in k]
    beta = [_colsel(beta_all, BETA_LANE + h) for h in hs]
    g = [_colsel(g_cols, ALPHA_LANE + h) for h in hs]
    g_row = [_rowsel(g_rows, ALPHA_LANE + h) for h in hs]
    g_last = [_colsel(g_end, ALPHA_LANE + h) for h in hs]
    decay = [jnp.where(incl, jnp.exp(jnp.where(incl, g[h] - g_row[h], 0.0)), 0.0) for h in hs]
    kk = [_dot(k[h], k[h], NT) for h in hs]
    qk = [_dot(q[h], k[h], NT) * decay[h] for h in hs]
    t_inv = _inv_unit_lower(tuple(jnp.where(strict, beta[h] * kk[h] * decay[h], 0.0) for h in hs))
    eg = [jnp.exp(g[h]) for h in hs]
    u0 = [_dot(t_inv[h], v[h] * beta[h]) for h in hs]
    w_cum = [_dot(t_inv[h], k[h] * (beta[h] * eg[h])) for h in hs]
    s_in = [state[h] for h in hs]
    u = [u0[h] - _dot(w_cum[h], s_in[h]) for h in hs]
    o = [_dot(q[h] * eg[h], s_in[h]) + _dot(qk[h], u[h]) for h in hs]
    s_out = [s_in[h] * jnp.exp(g_last[h]) + _dot(k[h] * jnp.exp(g_last[h] - g[h]), u[h], TN) for h in hs]
    o = [x * lax.rsqrt(jnp.mean(x * x, axis=-1, keepdims=True) + NORM_EPS) * onw for x in o]
    y = [o[h] * _silu(z[:, sl(0, h)]) for h in hs]
    return jnp.concatenate(y, axis=1), jnp.concatenate([s[None] for s in s_out], axis=0)


def _gdr_specs(nc, order):
    return [
        pl.BlockSpec((CHUNK, 3 * BRANCH), lambda n: (order(n), 0)),
        pl.BlockSpec((CHUNK, BRANCH), lambda n: (order(n), OFF_ZA // BRANCH)),
        pl.BlockSpec((CHUNK, LANE), lambda n: (order(n), OFF_SMALL // LANE)),
        pl.BlockSpec((1, LANE), lambda n: (0, 0)),
        pl.BlockSpec((1, LANE), lambda n: (0, 0)),
        pl.BlockSpec((1, LANE), lambda n: (0, 0)),
    ]


def _gdr_fwd(name, cqkv, proj, alog, dtb, onw):
    seq = cqkv.shape[0]
    nc = seq // CHUNK

    def body(c_ref, z_ref, sm_ref, al_ref, dt_ref, on_ref, y_ref, st_ref, state):
        @pl.when(pl.program_id(0) == 0)
        def _():
            state[...] = jnp.zeros_like(state)

        s_in = state[...]
        st_ref[0] = s_in
        y, s_out = _gdr_chunk(c_ref[...], z_ref[...], sm_ref[...], al_ref[...], dt_ref[...], on_ref[...], s_in)
        y_ref[...] = y.astype(y_ref.dtype)
        state[...] = s_out

    return pl.pallas_call(
        body, name=name, grid=(nc,), in_specs=_gdr_specs(nc, lambda n: n),
        out_specs=[pl.BlockSpec((CHUNK, BRANCH), lambda n: (n, 0)),
                   pl.BlockSpec((1, HEADS, HEAD_DIM, HEAD_DIM), lambda n: (n, 0, 0, 0))],
        out_shape=[jax.ShapeDtypeStruct((seq, BRANCH), BF16),
                   jax.ShapeDtypeStruct((nc, HEADS, HEAD_DIM, HEAD_DIM), F32)],
        scratch_shapes=[pltpu.VMEM((HEADS, HEAD_DIM, HEAD_DIM), F32)],
        compiler_params=_params(("arbitrary",)),
    )(cqkv, proj, proj, alog, dtb, onw)


def _gdr_bwd(name, dy, states, cqkv, proj, alog, dtb, onw):
    seq = cqkv.shape[0]
    nc = seq // CHUNK
    rev = lambda n: nc - 1 - n

    def body(c_ref, z_ref, sm_ref, al_ref, dt_ref, on_ref, dy_ref, st_ref,
             dc_ref, dz_ref, dsm_ref, dal_ref, ddt_ref, don_ref, dstate):
        first = pl.program_id(0) == 0

        @pl.when(first)
        def _():
            dstate[...] = jnp.zeros_like(dstate)
            dal_ref[...] = jnp.zeros_like(dal_ref)
            ddt_ref[...] = jnp.zeros_like(ddt_ref)
            don_ref[...] = jnp.zeros_like(don_ref)

        _, vjp = jax.vjp(_gdr_chunk, c_ref[...], z_ref[...], sm_ref[...], al_ref[...], dt_ref[...], on_ref[...],
                         st_ref[0])
        dc, dz, dsm, dal, ddt, don, ds = vjp((dy_ref[...].astype(F32), dstate[...]))
        dc_ref[...] = dc
        dz_ref[...] = dz.astype(dz_ref.dtype)
        dsm_ref[...] = dsm
        dal_ref[...] += dal
        ddt_ref[...] += ddt
        don_ref[...] += don
        dstate[...] = ds

    small = pl.BlockSpec((1, LANE), lambda n: (0, 0))
    return pl.pallas_call(
        body, name=name, grid=(nc,),
        in_specs=_gdr_specs(nc, rev) + [pl.BlockSpec((CHUNK, BRANCH), lambda n: (rev(n), 0)),
                                        pl.BlockSpec((1, HEADS, HEAD_DIM, HEAD_DIM), lambda n: (rev(n), 0, 0, 0))],
        out_specs=[pl.BlockSpec((CHUNK, 3 * BRANCH), lambda n: (rev(n), 0)),
                   pl.BlockSpec((CHUNK, BRANCH), lambda n: (rev(n), 0)),
                   pl.BlockSpec((CHUNK, LANE), lambda n: (rev(n), 0)), small, small, small],
        out_shape=[jax.ShapeDtypeStruct((seq, 3 * BRANCH), F32), jax.ShapeDtypeStruct((seq, BRANCH), BF16),
                   jax.ShapeDtypeStruct((seq, LANE), F32)] + [jax.ShapeDtypeStruct((1, LANE), F32)] * 3,
        scratch_shapes=[pltpu.VMEM((HEADS, HEAD_DIM, HEAD_DIM), F32)],
        compiler_params=_params(("arbitrary",)),
    )(cqkv, proj, proj, alog, dtb, onw, dy, states)


GATE_T = 512
ATT_T = 1024


def _fox_gate_fwd(name, proj, fb):
    seq = proj.shape[0]
    tb = min(GATE_T, seq)

    def body(sm_ref, fb_ref, c_ref, ct_ref):
        tri = (lax.broadcasted_iota(jnp.int32, (tb, tb), 0) >= lax.broadcasted_iota(jnp.int32, (tb, tb), 1)).astype(F32)
        carry = jnp.zeros((1, LANE), F32)
        for i in range(seq // tb):
            lf = -_softplus(-(sm_ref[i * tb:(i + 1) * tb, :] + fb_ref[...]))
            cb = _dot_hi(tri, lf) + carry
            c_ref[i * tb:(i + 1) * tb, :] = cb
            ct_ref[:, i * tb:(i + 1) * tb] = cb.T
            carry = carry + jnp.sum(lf, axis=0, keepdims=True)

    return pl.pallas_call(
        body, name=name, grid=(1,),
        in_specs=[pl.BlockSpec((seq, LANE), lambda i: (0, OFF_SMALL // LANE)), pl.BlockSpec((1, LANE), lambda i: (0, 0))],
        out_specs=[pl.BlockSpec((seq, LANE), lambda i: (0, 0)), pl.BlockSpec((LANE, seq), lambda i: (0, 0))],
        out_shape=[jax.ShapeDtypeStruct((seq, LANE), F32), jax.ShapeDtypeStruct((LANE, seq), F32)],
        compiler_params=_params(("arbitrary",)),
    )(proj, fb)


def _fox_gate_bwd(name, dck, proj, fb):
    seq = proj.shape[0]
    tb = min(GATE_T, seq)
    nb = seq // tb

    def body(d_ref, sm_ref, fb_ref, o_ref, dfb_ref, pad):
        tri = (lax.broadcasted_iota(jnp.int32, (tb, tb), 0) >= lax.broadcasted_iota(jnp.int32, (tb, tb), 1)).astype(F32)
        pad[...] = jnp.zeros_like(pad)
        carry = jnp.zeros((HEADS, 1), F32)
        dfb = jnp.zeros((1, LANE), F32)
        for i in reversed(range(nb)):
            blk = d_ref[:, i * tb:(i + 1) * tb]
            pad[FORGET_LANE:FORGET_LANE + HEADS, :] = _dot_hi(blk, tri) + carry
            carry = carry + jnp.sum(blk, axis=1, keepdims=True)
            x = sm_ref[i * tb:(i + 1) * tb, :] + fb_ref[...]
            dsm = pad[...].T * _sigmoid(-x)
            o_ref[i * tb:(i + 1) * tb, :] = dsm
            dfb = dfb + jnp.sum(dsm, axis=0, keepdims=True)
        dfb_ref[...] = dfb

    return pl.pallas_call(
        body, name=name, grid=(1,),
        in_specs=[pl.BlockSpec((HEADS, seq), lambda i: (0, 0)), pl.BlockSpec((seq, LANE), lambda i: (0, OFF_SMALL // LANE)),
                  pl.BlockSpec((1, LANE), lambda i: (0, 0))],
        out_specs=[pl.BlockSpec((seq, LANE), lambda i: (0, 0)), pl.BlockSpec((1, LANE), lambda i: (0, 0))],
        out_shape=[jax.ShapeDtypeStruct((seq, LANE), F32), jax.ShapeDtypeStruct((1, LANE), F32)],
        scratch_shapes=[pltpu.VMEM((LANE, tb), F32)],
        compiler_params=_params(("arbitrary",)),
    )(dck, proj, fb)


def _att_scores(q, k, cq, ck_row, diagonal):
    s = _dot(q, k, NT) * (HEAD_DIM ** -0.5) + (cq - ck_row)
    if diagonal:
        keep = lax.broadcasted_iota(jnp.int32, s.shape, 0) >= lax.broadcasted_iota(jnp.int32, s.shape, 1)
        s = jnp.where(keep, s, -jnp.inf)
    return s


def _causal_pairs(nq, key_major):
    pairs = ([(i, j) for j in range(nq) for i in range(j, nq)] if key_major
             else [(i, j) for i in range(nq) for j in range(i + 1)])
    return jnp.asarray([p[0] for p in pairs], jnp.int32), jnp.asarray([p[1] for p in pairs], jnp.int32)


def _attn_specs(tq):
    qs = lambda off: pl.BlockSpec((tq, HEAD_DIM), lambda h, t, it, jt: (it[t], off + h))
    kv = lambda off: pl.BlockSpec((tq, HEAD_DIM), lambda h, t, it, jt: (jt[t], off + h))
    c_spec = pl.BlockSpec((tq, LANE), lambda h, t, it, jt: (it[t], 0))
    ct_spec = pl.BlockSpec((1, 1, tq), lambda h, t, it, jt: (FORGET_LANE + h, 0, jt[t]))
    lse_spec = pl.BlockSpec((1, tq, LANE), lambda h, t, it, jt: (h, it[t], 0))
    return qs, kv, c_spec, ct_spec, lse_spec


def _ride_along(comm, refs, n_in, n_out, n_scratch, first, last):
    n_c = len(comm[1]) if comm else 0
    ins, c_in = refs[:n_in], refs[n_in:n_in + n_c]
    outs, c_out = refs[n_in + n_c:n_in + n_c + n_out], refs[n_in + n_c + n_out:n_in + 2 * n_c + n_out]
    scratch = refs[n_in + 2 * n_c + n_out:n_in + 2 * n_c + n_out + n_scratch]
    sems = refs[n_in + 2 * n_c + n_out + n_scratch:]
    if not comm:
        return ins, outs, scratch, lambda: None
    start, finish = _comm_ops(comm[0], c_in, c_out, sems)
    pl.when(first)(start)
    return ins, outs, scratch, lambda: pl.when(last)(finish)


def _attn_fwd(name, proj, c, ct3, comm=None):
    seq = proj.shape[0]
    tq = min(ATT_T, seq)
    nq = seq // tq
    qb, zb = OFF_QKVC // HEAD_DIM, OFF_ZC // HEAD_DIM
    i_tab, j_tab = _causal_pairs(nq, False)
    n_pairs = i_tab.shape[0]
    c_arrs = list(comm[1]) if comm else []
    c_shapes, c_sems = _comm_plan(comm[0], c_arrs) if comm else ([], [])

    def body(it, jt, *refs):
        h, t = pl.program_id(0), pl.program_id(1)
        i, j = it[t], jt[t]
        ins, outs, scratch, finish = _ride_along(comm, refs, 6, 3, 3, (h == 0) & (t == 0),
                                                 (h == HEADS - 1) & (t == n_pairs - 1))
        q_ref, k_ref, v_ref, z_ref, c_ref, ct_ref = ins
        y_ref, o_ref, lse_ref = outs
        m_s, l_s, acc_s = scratch

        @pl.when(j == 0)
        def _():
            m_s[...] = jnp.full_like(m_s, -jnp.inf)
            l_s[...] = jnp.zeros_like(l_s)
            acc_s[...] = jnp.zeros_like(acc_s)

        def step(diagonal):
            cq = _colsel(c_ref[...], FORGET_LANE + h)
            s = _att_scores(q_ref[...], k_ref[...], cq, ct_ref[0], diagonal)
            m_old = m_s[...]
            m_new = jnp.maximum(m_old, jnp.max(s, axis=1, keepdims=True))
            p = jnp.exp(s - m_new)
            alpha = jnp.exp(m_old - m_new)
            l_s[...] = alpha * l_s[...] + jnp.sum(p, axis=1, keepdims=True)
            p_hi = p.astype(MXU_DTYPE).astype(F32)
            acc_s[...] = alpha * acc_s[...] + _dot(p_hi, v_ref[...]) + _dot(p - p_hi, v_ref[...])
            m_s[...] = m_new

        @pl.when(j < i)
        def _():
            step(False)

        @pl.when(j == i)
        def _():
            step(True)
            o = acc_s[...] / l_s[...]
            o_ref[...] = o
            y_ref[...] = (o * _silu(z_ref[...])).astype(y_ref.dtype)
            lse_ref[0] = jnp.broadcast_to(m_s[...] + jnp.log(l_s[...]), (tq, LANE))

        finish()

    qs, kv, c_spec, ct_spec, lse_spec = _attn_specs(tq)
    res = pl.pallas_call(
        body, name=name,
        grid_spec=pltpu.PrefetchScalarGridSpec(
            num_scalar_prefetch=2, grid=(HEADS, n_pairs),
            in_specs=[qs(qb), kv(qb + HEADS), kv(qb + 2 * HEADS), qs(zb), c_spec, ct_spec] + [HBM_SPEC] * len(c_arrs),
            out_specs=[qs(0), qs(0), lse_spec] + [HBM_SPEC] * len(c_arrs),
            scratch_shapes=[pltpu.VMEM((tq, 1), F32), pltpu.VMEM((tq, 1), F32), pltpu.VMEM((tq, HEAD_DIM), F32)]
            + c_sems),
        out_shape=[jax.ShapeDtypeStruct((seq, BRANCH), BF16), jax.ShapeDtypeStruct((seq, BRANCH), F32),
                   jax.ShapeDtypeStruct((HEADS, seq, LANE), F32)] + c_shapes,
        compiler_params=_params(("arbitrary", "arbitrary")),
    )(i_tab, j_tab, proj, proj, proj, proj, c, ct3, *c_arrs)
    return res[0], res[1], res[2], list(res[3:])


def _attn_dq(name, dy, o, lse, proj, c, ct3, comm=None):
    seq = proj.shape[0]
    tq = min(ATT_T, seq)
    nq = seq // tq
    qb, zb = OFF_QKVC // HEAD_DIM, OFF_ZC // HEAD_DIM
    i_tab, j_tab = _causal_pairs(nq, False)
    n_pairs = i_tab.shape[0]
    c_arrs = list(comm[1]) if comm else []
    c_shapes, c_sems = _comm_plan(comm[0], c_arrs) if comm else ([], [])

    def body(it, jt, *refs):
        h, t = pl.program_id(0), pl.program_id(1)
        i, j = it[t], jt[t]
        ins, outs, scratch, finish = _ride_along(comm, refs, 9, 2, 3, (h == 0) & (t == 0),
                                                 (h == HEADS - 1) & (t == n_pairs - 1))
        q_ref, k_ref, v_ref, z_ref, c_ref, ct_ref, dy_ref, o_ref, lse_ref = ins
        dq_ref, dz_ref = outs
        do_s, dl_s, acc_s = scratch

        @pl.when(j == 0)
        def _():
            z = z_ref[...]
            sg = _sigmoid(z)
            dyv = dy_ref[...].astype(F32)
            do = dyv * z * sg
            do_s[...] = do
            dl_s[...] = jnp.sum(do.astype(MXU_DTYPE).astype(F32) * o_ref[...], axis=1, keepdims=True)
            dz_ref[...] = (dyv * o_ref[...] * sg * (1.0 + z * (1.0 - sg))).astype(dz_ref.dtype)
            acc_s[...] = jnp.zeros_like(acc_s)

        def step(diagonal):
            cq = _colsel(c_ref[...], FORGET_LANE + h)
            s = _att_scores(q_ref[...], k_ref[...], cq, ct_ref[0], diagonal)
            p = jnp.exp(s - jnp.max(lse_ref[0], axis=1, keepdims=True))
            dp = _dot(do_s[...], v_ref[...], NT)
            ds = p * (dp - dl_s[...])
            acc_s[...] += _dot(ds, k_ref[...])

        @pl.when(j < i)
        def _():
            step(False)

        @pl.when(j == i)
        def _():
            step(True)
            dq_ref[...] = (acc_s[...] * (HEAD_DIM ** -0.5)).astype(dq_ref.dtype)

        finish()

    qs, kv, c_spec, ct_spec, lse_spec = _attn_specs(tq)
    res = pl.pallas_call(
        body, name=name,
        grid_spec=pltpu.PrefetchScalarGridSpec(
            num_scalar_prefetch=2, grid=(HEADS, n_pairs),
            in_specs=[qs(qb), kv(qb + HEADS), kv(qb + 2 * HEADS), qs(zb), c_spec, ct_spec, qs(0), qs(0), lse_spec]
            + [HBM_SPEC] * len(c_arrs),
            out_specs=[qs(0), qs(0)] + [HBM_SPEC] * len(c_arrs),
            scratch_shapes=[pltpu.VMEM((tq, HEAD_DIM), F32), pltpu.VMEM((tq, 1), F32), pltpu.VMEM((tq, HEAD_DIM), F32)]
            + c_sems),
        out_shape=[jax.ShapeDtypeStruct((seq, BRANCH), BF16)] * 2 + c_shapes,
        compiler_params=_params(("arbitrary", "arbitrary")),
    )(i_tab, j_tab, proj, proj, proj, proj, c, ct3, dy, o, lse, *c_arrs)
    return res[0], res[1], list(res[2:])


def _attn_dkv(name, dy, o, lse, proj, c, ct3, comm=None):
    seq = proj.shape[0]
    tq = min(ATT_T, seq)
    nq = seq // tq
    qb, zb = OFF_QKVC // HEAD_DIM, OFF_ZC // HEAD_DIM
    i_tab, j_tab = _causal_pairs(nq, True)
    n_pairs = i_tab.shape[0]
    c_arrs = list(comm[1]) if comm else []
    c_shapes, c_sems = _comm_plan(comm[0], c_arrs) if comm else ([], [])

    def body(it, jt, *refs):
        h, t = pl.program_id(0), pl.program_id(1)
        i, j = it[t], jt[t]
        ins, outs, scratch, finish = _ride_along(comm, refs, 9, 3, 3, (h == 0) & (t == 0),
                                                 (h == HEADS - 1) & (t == n_pairs - 1))
        q_ref, k_ref, v_ref, z_ref, c_ref, ct_ref, dy_ref, o_ref, lse_ref = ins
        dk_ref, dv_ref, dc_ref = outs
        dk_s, dv_s, dc_s = scratch

        @pl.when(i == j)
        def _():
            dk_s[...] = jnp.zeros_like(dk_s)
            dv_s[...] = jnp.zeros_like(dv_s)
            dc_s[...] = jnp.zeros_like(dc_s)

        def step(diagonal):
            z = z_ref[...]
            do = dy_ref[...].astype(F32) * _silu(z)
            delta = jnp.sum(do.astype(MXU_DTYPE).astype(F32) * o_ref[...], axis=1, keepdims=True)
            cq = _colsel(c_ref[...], FORGET_LANE + h)
            s = _att_scores(q_ref[...], k_ref[...], cq, ct_ref[0], diagonal)
            p = jnp.exp(s - jnp.max(lse_ref[0], axis=1, keepdims=True))
            dv_s[...] += _dot(p, do, TN)
            ds = p * (_dot(do, v_ref[...], NT) - delta)
            dk_s[...] += _dot(ds, q_ref[...], TN)
            dc_s[...] -= jnp.sum(ds, axis=0, keepdims=True)

        @pl.when(i == j)
        def _():
            step(True)

        @pl.when(i > j)
        def _():
            step(False)

        @pl.when(i == nq - 1)
        def _():
            dk_ref[...] = (dk_s[...] * (HEAD_DIM ** -0.5)).astype(dk_ref.dtype)
            dv_ref[...] = dv_s[...].astype(dv_ref.dtype)
            dc_ref[0] = dc_s[...]

        finish()

    qs, kv, c_spec, ct_spec, lse_spec = _attn_specs(tq)
    res = pl.pallas_call(
        body, name=name,
        grid_spec=pltpu.PrefetchScalarGridSpec(
            num_scalar_prefetch=2, grid=(HEADS, n_pairs),
            in_specs=[qs(qb), kv(qb + HEADS), kv(qb + 2 * HEADS), qs(zb), c_spec, ct_spec, qs(0), qs(0), lse_spec]
            + [HBM_SPEC] * len(c_arrs),
            out_specs=[kv(0), kv(0), pl.BlockSpec((1, 1, tq), lambda h, t, it, jt: (h, 0, jt[t]))]
            + [HBM_SPEC] * len(c_arrs),
            scratch_shapes=[pltpu.VMEM((tq, HEAD_DIM), F32), pltpu.VMEM((tq, HEAD_DIM), F32), pltpu.VMEM((1, tq), F32)]
            + c_sems),
        out_shape=[jax.ShapeDtypeStruct((seq, BRANCH), BF16)] * 2 + [jax.ShapeDtypeStruct((HEADS, 1, seq), F32)]
        + c_shapes,
        compiler_params=_params(("arbitrary", "arbitrary")),
    )(i_tab, j_tab, proj, proj, proj, proj, c, ct3, dy, o, lse, *c_arrs)
    return res[0], res[1], res[2], list(res[3:])


def _loss_head(name, y, target):
    seq, d = y.shape
    tile = min(256, seq)

    def body(y_ref, t_ref, dy_ref, l_ref):
        err = y_ref[...] - t_ref[...]
        dy_ref[...] = err / d

        @pl.when(pl.program_id(0) == 0)
        def _():
            l_ref[...] = jnp.zeros_like(l_ref)

        l_ref[...] += 0.5 * jnp.sum(jnp.mean(err * err, axis=-1, keepdims=True), axis=0, keepdims=True)

    return pl.pallas_call(
        body, name=name, grid=(seq // tile,),
        in_specs=[pl.BlockSpec((tile, d), lambda i: (i, 0))] * 2,
        out_specs=[pl.BlockSpec((tile, d), lambda i: (i, 0)), pl.BlockSpec((8, LANE), lambda i: (0, 0))],
        out_shape=[jax.ShapeDtypeStruct((seq, d), F32), jax.ShapeDtypeStruct((8, LANE), F32)],
        compiler_params=_params(("arbitrary",)),
    )(y, target)


def _adamw(name, w, m, v, g_parts, tile):
    rows, cols = w.shape
    n_g = len(g_parts)
    part_rows = rows // n_g
    tile = min(tile, part_rows)
    assert part_rows % tile == 0 and all(p.shape == (part_rows, cols) for p in g_parts), (name, w.shape)
    nb = part_rows // tile

    def body(*refs):
        w_ref, m_ref, v_ref = refs[:3]
        g_refs = refs[3:3 + n_g]
        g_ref, d_ref, nm_ref, nv_ref = refs[3 + n_g:]
        part = pl.program_id(0)
        g = g_refs[0][...]
        for k in range(1, n_g):
            g = jnp.where(part == k, g_refs[k][...], g)
        m_new = ADAM_B1 * m_ref[...] + (1.0 - ADAM_B1) * g
        v_new = ADAM_B2 * v_ref[...] + (1.0 - ADAM_B2) * (g * g)
        m_hat = m_new / (1.0 - ADAM_B1 ** ADAM_STEP)
        v_hat = v_new / (1.0 - ADAM_B2 ** ADAM_STEP)
        g_ref[...] = g
        d_ref[...] = -ADAM_LR * (m_hat / (jnp.sqrt(v_hat) + ADAM_EPS) + ADAM_WD * w_ref[...])
        nm_ref[...] = m_new
        nv_ref[...] = v_new

    blk = pl.BlockSpec((tile, cols), lambda p, i: (p * nb + i, 0))
    g_specs = [pl.BlockSpec((tile, cols), functools.partial(lambda p, i, k: (jnp.where(p == k, i, 0), 0), k=k))
               for k in range(n_g)]
    return pl.pallas_call(
        body, name=name, grid=(n_g, nb), in_specs=[blk] * 3 + g_specs, out_specs=[blk] * 4,
        out_shape=[jax.ShapeDtypeStruct((rows, cols), F32)] * 4,
        compiler_params=_params(("arbitrary", "arbitrary")),
    )(w, m, v, *g_parts)


_ORIG_SEGMENTS = (
    ("qkv_a", 0, 3072), ("z_a", 3072, 1024), ("beta", 4096, 8), ("alpha", 4104, 8), ("glu", 4112, 2048),
    ("z_b", 6160, 1024), ("qkv_c", 7184, 3072), ("z_c", 10256, 1024), ("forget", 11280, 8), ("gate", 11288, 6144))
_PAD_ORDER = ("gate", "qkv_a", "z_a", "glu", "z_b", "qkv_c", "z_c", "beta", "alpha", "forget")


def _pad_cols(w):
    seg = {n: w[..., s:s + k] for n, s, k in _ORIG_SEGMENTS}
    fill = jnp.zeros(w.shape[:-1] + (N_PAD - N_IN,), w.dtype)
    return jnp.concatenate([seg[n] for n in _PAD_ORDER] + [fill], axis=-1)


def _unpad_cols(g):
    off, seg = 0, {}
    widths = {n: k for n, _, k in _ORIG_SEGMENTS}
    for n in _PAD_ORDER:
        seg[n] = g[..., off:off + widths[n]]
        off += widths[n]
    return jnp.concatenate([seg[n] for n, _, _ in _ORIG_SEGMENTS], axis=-1)


def _lane_row(vals, lane0):
    return jnp.pad(vals.astype(F32), (lane0, LANE - HEADS - lane0))[None]


def _layer_fwd(x, p, comm=None):
    seq = x.shape[0]
    h = _rowwise("rms_pre", _rms_pre_fn, [(x, D_MODEL, 0)], [p["pre_w"]], [(D_MODEL, BF16)], 256)[0]
    proj = _matmul("mm_in", h, p["w_in"], "nn", F32, 512, 1280, 2048)
    ca = _conv_fwd("conv_a", proj, OFF_QKVA, 3 * BRANCH, p["w4"], jnp.zeros((1, 3 * BRANCH), F32), SHORT_CONV)
    y_a, states = _gdr_fwd("gdr_fwd", ca, proj, p["alog"], p["dtb"], p["onw"])
    u2 = _conv_fwd("conv_b", proj, OFF_VAL, BRANCH, p["w31"], p["cb"], CONF_CONV, gate_off=OFF_GLUG)
    y_b = _rowwise("ln_gate", _ln_gate_fn, [(u2, BRANCH, 0), (proj, BRANCH, OFF_ZB // BRANCH)],
                   [p["ln_w"], p["ln_b"]], [(BRANCH, BF16)], 256)[0]
    c, ct = _fox_gate_fwd("fox_gate", proj, p["fb"])
    ct3 = ct.reshape(LANE, 1, seq)
    y_c, o_c, lse, got = _attn_fwd("attn_fwd", proj, c, ct3, comm)
    ys = (y_a, y_b, y_c)
    br = [_matmul("mm_br", ys[n], p["wbr"][n], "nn", F32, 512, 2048, 1024) for n in range(N_BRANCH)]
    merged = _rowwise("merge", _merge_fn, [(proj, D_MODEL, n) for n in range(N_BRANCH)] + [(b, D_MODEL, 0) for b in br],
                      [], [(D_MODEL, BF16)], 256)[0]
    out = _matmul("mm_out", merged, p["wout"], "nn", F32, 512, 2048, 2048)
    x_new = _rowwise("rms_post", _rms_post_fn, [(out, D_MODEL, 0), (x, D_MODEL, 0)], [p["post_w"]],
                     [(D_MODEL, F32)], 256)[0]
    saved = dict(x=x, ht=h.T, proj=proj, ca=ca, states=states, u2=u2, c=c, ct3=ct3, o_c=o_c, lse=lse, ys=ys, br=br,
                 merged=merged, out=out)
    return x_new, saved, got


def _layer_bwd(dxn, p, sv, above=None, own_half=None):
    x, proj = sv["x"], sv["proj"]
    seq = x.shape[0]
    g = {}
    d_out, g["post_w"] = _rowwise_bwd("rms_post_bwd", _rms_only_fn, [(sv["out"], D_MODEL, 0)], [p["post_w"]], [dxn],
                                      [BF16], 256)
    d_merged = _matmul("mm_out_dx", d_out, p["wout"], "nt", F32, 512, 2048, 2048)
    g["wout"] = _matmul("mm_out_dw", sv["merged"], d_out, "tn", F32, 1024, 1024, 512)
    rows = [(proj, D_MODEL, n) for n in range(N_BRANCH)] + [(b, D_MODEL, 0) for b in sv["br"]]
    d_gl0, d_gl1, d_gl2, d_b0, d_b1, d_b2 = _rowwise_bwd("merge_bwd", _merge_fn, rows, [], [d_merged], [BF16] * 6, 128)
    d_br = (d_b0, d_b1, d_b2)
    dys = [_matmul("mm_br_dx", d_br[n], p["wbr"][n], "nt", BF16, 512, 1024, 2048) for n in range(N_BRANCH)]
    g["wbr"] = jnp.stack([_matmul("mm_br_dw", sv["ys"][n], d_br[n], "tn", F32, 1024, 1024, 512)
                          for n in range(N_BRANCH)])
    dq, dzc, other = _attn_dq("attn_dq", dys[2], sv["o_c"], sv["lse"], proj, sv["c"], sv["ct3"],
                              ("halves", above) if above else None)
    summed = own_half(above, other) if above else None
    dk, dv, dck, got = _attn_dkv("attn_dkv", dys[2], sv["o_c"], sv["lse"], proj, sv["c"], sv["ct3"],
                                 ("reduce_scatter", summed) if above else None)
    dsm_c, g["fb"] = _fox_gate_bwd("fox_gate_bwd", dck.reshape(HEADS, seq), proj, p["fb"])
    du2, dzb, g["ln_w"], g["ln_b"] = _rowwise_bwd(
        "ln_gate_bwd", _ln_gate_fn, [(sv["u2"], BRANCH, 0), (proj, BRANCH, OFF_ZB // BRANCH)], [p["ln_w"], p["ln_b"]],
        [dys[1]], [F32, BF16], 256)
    dval, dgate, g["w31"], g["cb"] = _conv_bwd("conv_b_bwd", du2, proj, OFF_VAL, BRANCH, p["w31"], CONF_CONV,
                                               gate_off=OFF_GLUG)
    dca, dza, dsm_a, g["alog"], g["dtb"], g["onw"] = _gdr_bwd("gdr_bwd", dys[0], sv["states"], sv["ca"], proj,
                                                              p["alog"], p["dtb"], p["onw"])
    dqkva, g["w4"], _ = _conv_bwd("conv_a_bwd", dca, proj, OFF_QKVA, 3 * BRANCH, p["w4"], SHORT_CONV)
    d_small = jnp.pad((dsm_a + dsm_c).astype(BF16), ((0, 0), (0, N_PAD - OFF_SMALL - LANE)))
    d_proj = jnp.concatenate([d_gl0, d_gl1, d_gl2, dqkva, dza, dval, dgate, dzb, dq, dk, dv, dzc, d_small], axis=1)
    dh = _matmul("mm_in_dx", d_proj, p["w_in"], "nt", F32, 512, 2048, 2560)
    g["w_in"] = _matmul("mm_in_dw", sv["ht"], d_proj, "nn", F32, 2048, 640, 2048)
    dx, g["pre_w"] = _rowwise_bwd("rms_pre_bwd", _rms_pre_res_fn, [(x, D_MODEL, 0)], [p["pre_w"]], [dh, dxn], [F32], 256)
    return dx, g, summed, got


N_CHIPS = 4
N_DEV = 8
HBM_SPEC = pl.BlockSpec(memory_space=pltpu.HBM)


def _mesh_pos():
    return lax.axis_index("x"), lax.axis_index("y"), lax.axis_index("c")


def _other_chips(x, y):
    return [(1 - x, y), (x, 1 - y), (1 - x, 1 - y)]


def _comm_plan(kind, arrs):
    n = len(arrs)
    if kind == "allgather":
        return ([jax.ShapeDtypeStruct((3,) + a.shape, a.dtype) for a in arrs], [pltpu.SemaphoreType.DMA((3 * n,))] * 4)
    if kind == "halves":
        return ([jax.ShapeDtypeStruct((a.shape[0], a.shape[1] // 2, a.shape[2]), a.dtype) for a in arrs],
                [pltpu.SemaphoreType.DMA((N_CHIPS * n,))] * 2)
    return ([jax.ShapeDtypeStruct((3,) + a.shape[1:], a.dtype) for a in arrs], [pltpu.SemaphoreType.DMA((3 * n,))] * 2)


def _comm_ops(kind, ins, outs, sems):
    ops = {"allgather": _allgather_ops, "halves": _halves_ops, "reduce_scatter": _reduce_scatter_ops}
    return ops[kind](ins, outs, sems)


def _halves_ops(ins, outs, sems):
    send_sems, recv_sems = sems
    x, y, c = _mesh_pos()
    copies = []
    for a in range(len(ins)):
        rows = ins[a].shape[1] // 2
        for s in range(N_CHIPS):
            copies.append(pltpu.make_async_remote_copy(
                src_ref=ins[a].at[s, pl.ds((1 - c) * rows, rows)], dst_ref=outs[a].at[s],
                send_sem=send_sems.at[N_CHIPS * a + s], recv_sem=recv_sems.at[N_CHIPS * a + s],
                device_id=(x, y, 1 - c), device_id_type=MESH))

    def start():
        for cp in copies:
            cp.start()

    def finish():
        for cp in copies:
            cp.wait_recv()
        for cp in copies:
            cp.wait_send()

    return start, finish


def _allgather_ops(ins, outs, sems):
    send_sems, recv_sems, pass_send, pass_recv = sems
    n = len(ins)
    x, y, c = _mesh_pos()

    def part(a, core):
        half = ins[a].shape[0] // 2
        return pl.ds(half * core, half)

    def ici(a, j):
        px, py = _other_chips(x, y)[j]
        return pltpu.make_async_remote_copy(
            src_ref=ins[a].at[part(a, c)], dst_ref=outs[a].at[j, part(a, c)], send_sem=send_sems.at[3 * a + j],
            recv_sem=recv_sems.at[3 * a + j], device_id=(px, py, c), device_id_type=MESH)

    def d2d(a, j, core):
        blk = outs[a].at[j, part(a, core)]
        return pltpu.make_async_remote_copy(
            src_ref=blk, dst_ref=blk, send_sem=pass_send.at[3 * a + j], recv_sem=pass_recv.at[3 * a + j],
            device_id=(x, y, 1 - c), device_id_type=MESH)

    def start():
        for a in range(n):
            for j in range(3):
                ici(a, j).start()

    def finish():
        for a in range(n):
            for j in range(3):
                ici(a, j).wait_recv()
                d2d(a, j, c).start()
        for a in range(n):
            for j in range(3):
                d2d(a, j, 1 - c).wait_recv()
        for a in range(n):
            for j in range(3):
                ici(a, j).wait_send()
                d2d(a, j, c).wait_send()

    return start, finish


def _reduce_scatter_ops(ins, outs, sems):
    send_sems, recv_sems = sems
    n = len(ins)
    x, y, c = _mesh_pos()

    def remote(a, j):
        px, py = _other_chips(x, y)[j]
        return pltpu.make_async_remote_copy(
            src_ref=ins[a].at[2 * px + py], dst_ref=outs[a].at[j], send_sem=send_sems.at[3 * a + j],
            recv_sem=recv_sems.at[3 * a + j], device_id=(px, py, c), device_id_type=MESH)

    def start():
        for a in range(n):
            for j in range(3):
                remote(a, j).start()

    def finish():
        for a in range(n):
            for j in range(3):
                remote(a, j).wait_recv()
        for a in range(n):
            for j in range(3):
                remote(a, j).wait_send()

    return start, finish


def _exchange_chips(name, kind, arrs):
    n = len(arrs)
    shapes, sems = _comm_plan(kind, arrs)

    def body(*refs):
        start, finish = _comm_ops(kind, refs[:n], refs[n:2 * n], refs[2 * n:])
        start()
        finish()

    return pl.pallas_call(body, name=name, in_specs=[HBM_SPEC] * n, out_specs=[HBM_SPEC] * n, out_shape=shapes,
                          scratch_shapes=sems)(*arrs)


def _add_own_half(name, full, other, core, tile):
    n, rows, cols = other.shape
    tile = min(tile, rows)
    assert rows % tile == 0, (name, other.shape)
    nb = rows // tile

    def body(c_ref, f_ref, o_ref, out_ref):
        out_ref[...] = (f_ref[...] + o_ref[...]).astype(out_ref.dtype)

    return pl.pallas_call(
        body, name=name,
        grid_spec=pltpu.PrefetchScalarGridSpec(
            num_scalar_prefetch=1, grid=(n, nb),
            in_specs=[pl.BlockSpec((1, tile, cols), lambda s, i, c_ref: (s, c_ref[0] * nb + i, 0)),
                      pl.BlockSpec((1, tile, cols), lambda s, i, c_ref: (s, i, 0))],
            out_specs=pl.BlockSpec((1, tile, cols), lambda s, i, c_ref: (s, i, 0))),
        out_shape=jax.ShapeDtypeStruct(other.shape, BF16),
        compiler_params=_params(("parallel", "parallel")),
    )(core, full, other)


def _join_cores(name, arrs):
    n = len(arrs)

    def body(*refs):
        bufs = refs[n:2 * n]
        send_sems, recv_sems = refs[2 * n:]
        x, y, c = _mesh_pos()

        def copy(a, slot):
            return pltpu.make_async_remote_copy(
                src_ref=bufs[a].at[slot], dst_ref=bufs[a].at[slot], send_sem=send_sems.at[a], recv_sem=recv_sems.at[a],
                device_id=(x, y, 1 - c), device_id_type=MESH)

        for a in range(n):
            copy(a, c).start()
        for a in range(n):
            copy(a, 1 - c).wait_recv()
        for a in range(n):
            copy(a, c).wait_send()

    return pl.pallas_call(
        body, name=name, in_specs=[HBM_SPEC] * n, out_specs=[HBM_SPEC] * n,
        out_shape=[jax.ShapeDtypeStruct(a.shape, a.dtype) for a in arrs],
        input_output_aliases={a: a for a in range(n)},
        scratch_shapes=[pltpu.SemaphoreType.DMA((n,)), pltpu.SemaphoreType.DMA((n,))],
    )(*arrs)


def _sum_chips(name, own, recv, chip, core, tile):
    _, rows, cols = recv.shape
    tile = min(tile, rows)
    assert rows % tile == 0, (name, recv.shape)

    def body(chip_ref, core_ref, own_ref, recv_ref, o_ref):
        f32 = lambda a: a.astype(F32)
        o_ref[0] = ((f32(own_ref[0]) + f32(recv_ref[0])) + f32(recv_ref[1])) + f32(recv_ref[2])

    return pl.pallas_call(
        body, name=name,
        grid_spec=pltpu.PrefetchScalarGridSpec(
            num_scalar_prefetch=2, grid=(rows // tile,),
            in_specs=[pl.BlockSpec((1, tile, cols), lambda i, chip_ref, core_ref: (chip_ref[0], i, 0)),
                      pl.BlockSpec((3, tile, cols), lambda i, chip_ref, core_ref: (0, i, 0))],
            out_specs=pl.BlockSpec((1, tile, cols), lambda i, chip_ref, core_ref: (core_ref[0], i, 0))),
        out_shape=jax.ShapeDtypeStruct((2, rows, cols), F32),
        compiler_params=_params(("parallel",)),
    )(chip, core, own, recv)


def _allgather_devices(name, buf):
    def body(in_ref, out_ref, send_sems, recv_sems):
        x, y, c = _mesh_pos()
        me = 4 * x + 2 * y + c
        out_ref[me] = in_ref[...]

        def remote(k, slot):
            peer = (x ^ (k >> 2), y ^ ((k >> 1) & 1), c ^ (k & 1))
            return pltpu.make_async_remote_copy(
                src_ref=in_ref, dst_ref=out_ref.at[slot], send_sem=send_sems.at[k - 1], recv_sem=recv_sems.at[k - 1],
                device_id=peer, device_id_type=MESH)

        for k in range(1, N_DEV):
            remote(k, me).start()
        for k in range(1, N_DEV):
            remote(k, me ^ k).wait_recv()
        for k in range(1, N_DEV):
            remote(k, me).wait_send()

    vmem = pl.BlockSpec(memory_space=pltpu.VMEM)
    return pl.pallas_call(
        body, name=name, in_specs=[vmem], out_specs=vmem,
        out_shape=jax.ShapeDtypeStruct((N_DEV,) + buf.shape, buf.dtype),
        scratch_shapes=[pltpu.SemaphoreType.DMA((N_DEV - 1,)), pltpu.SemaphoreType.DMA((N_DEV - 1,))],
    )(buf)


def _sum_slots(name, a, tile):
    n, rows, cols = a.shape
    tile = min(tile, rows)
    assert rows % tile == 0, (name, a.shape)

    def body(a_ref, o_ref):
        acc = a_ref[0].astype(F32)
        for i in range(1, n):
            acc = acc + a_ref[i].astype(F32)
        o_ref[...] = acc

    return pl.pallas_call(
        body, name=name, grid=(rows // tile,),
        in_specs=[pl.BlockSpec((n, tile, cols), lambda i: (0, i, 0))],
        out_specs=pl.BlockSpec((tile, cols), lambda i: (i, 0)),
        out_shape=jax.ShapeDtypeStruct((rows, cols), F32),
        compiler_params=_params(("parallel",)),
    )(a)


_SMALL = (
    ("pre_norm_w", (D_MODEL,)), ("post_norm_w", (D_MODEL,)), ("a_log", (HEADS,)), ("dt_bias", (HEADS,)),
    ("o_norm_w", (HEAD_DIM,)), ("conv_b", (BRANCH,)), ("ln_w", (BRANCH,)), ("ln_b", (BRANCH,)), ("f_bias", (HEADS,)),
    ("conv_qkv_w", (SHORT_CONV, 3 * BRANCH)), ("conv_w", (CONF_CONV, BRANCH)))


def _small_grads(g):
    heads = lambda a, lane0: a[0, lane0:lane0 + HEADS]
    return jnp.concatenate([
        g["pre_w"][0], g["post_w"][0], heads(g["alog"], ALPHA_LANE), heads(g["dtb"], ALPHA_LANE), g["onw"][0],
        g["cb"][0], g["ln_w"][0], g["ln_b"][0], heads(g["fb"], FORGET_LANE), g["w4"][:SHORT_CONV].reshape(-1),
        g["w31"][:CONF_CONV].reshape(-1)])
_SHARDED_SMALL = {"conv_qkv_w": 3 * BRANCH // N_CHIPS, "conv_w": BRANCH // N_CHIPS}
_WEIGHTS = ("pre_norm_w", "post_norm_w", "w_in", "conv_qkv_w", "a_log", "dt_bias", "o_norm_w", "conv_w", "conv_b",
            "ln_w", "ln_b", "f_bias", "w_branch", "w_out")


def _pack(parts):
    flat = jnp.concatenate([p.reshape(-1).astype(F32) for p in parts])
    rows = -(-flat.shape[0] // (8 * LANE)) * 8
    return jnp.pad(flat, (0, rows * LANE - flat.shape[0])).reshape(rows, LANE)


def _unpack(buf, shapes):
    flat, out, off = buf.reshape(-1), [], 0
    for shp in shapes:
        size = 1
        for s in shp:
            size *= s
        out.append(flat[off:off + size].reshape(shp))
        off += size
    return out


def kernel(x, pre_norm_w, post_norm_w, w_in, conv_qkv_w, a_log, dt_bias, o_norm_w, conv_w, conv_b, ln_w, ln_b, f_bias, w_branch, w_out, loss_target, m_pre_norm_w, m_post_norm_w, m_w_in, m_conv_qkv_w, m_a_log, m_dt_bias, m_o_norm_w, m_conv_w, m_conv_b, m_ln_w, m_ln_b, m_f_bias, m_w_branch, m_w_out, v_pre_norm_w, v_post_norm_w, v_w_in, v_conv_qkv_w, v_a_log, v_dt_bias, v_o_norm_w, v_conv_w, v_conv_b, v_ln_w, v_ln_b, v_f_bias, v_w_branch, v_w_out):
    weights = dict(pre_norm_w=pre_norm_w, post_norm_w=post_norm_w, w_in=w_in, conv_qkv_w=conv_qkv_w, a_log=a_log,
                   dt_bias=dt_bias, o_norm_w=o_norm_w, conv_w=conv_w, conv_b=conv_b, ln_w=ln_w, ln_b=ln_b, f_bias=f_bias,
                   w_branch=w_branch, w_out=w_out)
    mom1 = dict(pre_norm_w=m_pre_norm_w, post_norm_w=m_post_norm_w, w_in=m_w_in, conv_qkv_w=m_conv_qkv_w, a_log=m_a_log,
                dt_bias=m_dt_bias, o_norm_w=m_o_norm_w, conv_w=m_conv_w, conv_b=m_conv_b, ln_w=m_ln_w, ln_b=m_ln_b,
                f_bias=m_f_bias, w_branch=m_w_branch, w_out=m_w_out)
    mom2 = dict(pre_norm_w=v_pre_norm_w, post_norm_w=v_post_norm_w, w_in=v_w_in, conv_qkv_w=v_conv_qkv_w, a_log=v_a_log,
                dt_bias=v_dt_bias, o_norm_w=v_o_norm_w, conv_w=v_conv_w, conv_b=v_conv_b, ln_w=v_ln_w, ln_b=v_ln_b,
                f_bias=v_f_bias, w_branch=v_w_branch, w_out=v_w_out)
    chip = 2 * lax.axis_index("x") + lax.axis_index("y")
    core = lax.axis_index("c").astype(jnp.int32).reshape(1)
    chip_id = chip.astype(jnp.int32).reshape(1)

    w_in_b, w_out_b = w_in.astype(BF16), w_out.astype(BF16)
    w_br_b = w_branch.astype(BF16).reshape(DEPTH, N_BRANCH * BRANCH, D_MODEL // N_CHIPS)
    shards = lambda l: [w_in_b[l], w_br_b[l], w_out_b[l]]

    def whole(own, got, axis=-1):
        parts = []
        for s in range(N_CHIPS):
            d = chip ^ s
            parts.append(jnp.where(d == 0, own, jnp.where(d == 2, got[0], jnp.where(d == 1, got[1], got[2]))))
        return jnp.concatenate(parts, axis=axis)

    first = _exchange_chips("allgather_weights", "allgather", shards(0) + [conv_qkv_w, conv_w])
    c4_full = jnp.pad(whole(conv_qkv_w, first[3]), ((0, 0), (0, 8 - SHORT_CONV), (0, 0)))
    c31_full = jnp.pad(whole(conv_w, first[4]), ((0, 0), (0, 32 - CONF_CONV), (0, 0)))

    def layer_params(l, got):
        return dict(
            pre_w=pre_norm_w[l][None], post_w=post_norm_w[l][None], w_in=_pad_cols(whole(w_in_b[l], got[0])),
            w4=c4_full[l], alog=_lane_row(a_log[l], ALPHA_LANE), dtb=_lane_row(dt_bias[l], ALPHA_LANE),
            onw=o_norm_w[l][None], w31=c31_full[l], cb=conv_b[l][None], ln_w=ln_w[l][None], ln_b=ln_b[l][None],
            fb=_lane_row(f_bias[l], FORGET_LANE),
            wbr=whole(w_br_b[l], got[1]).reshape(N_BRANCH, BRANCH, D_MODEL), wout=whole(w_out_b[l], got[2], axis=0))

    act = x[0]
    got = first[:3]
    layers, saved = [], []
    for l in range(DEPTH):
        layers.append(layer_params(l, got))
        act, sv, got = _layer_fwd(act, layers[l], ("allgather", shards(l + 1)) if l + 1 < DEPTH else None)
        saved.append(sv)
    d_act, loss_blk = _loss_head("loss_head", act, loss_target[0])

    parts_in, parts_br, parts_out, small_g = [], [], [], []

    def finish_reduce(own, recv):
        mine = [_sum_chips("sum_chips", o, r, chip_id, core, t) for o, r, t in zip(own, recv, (64, 512, 128))]
        joined = _join_cores("join_cores", mine)
        for lst, j in zip((parts_in, parts_br, parts_out), joined):
            lst.append(j.reshape(2 * j.shape[1], j.shape[2]))

    def own_half(parts, other):
        return [_add_own_half("add_own_half", f, o, core, t) for f, o, t in zip(parts, other, (64, 512, 128))]

    above = None
    for l in reversed(range(DEPTH)):
        d_act, g, summed, recv = _layer_bwd(d_act, layers[l], saved[l], above, own_half)
        if above:
            finish_reduce(summed, recv)
        g_in = _unpad_cols(g["w_in"]).reshape(D_MODEL, N_CHIPS, N_IN // N_CHIPS).transpose(1, 0, 2)
        g_br = g["wbr"].reshape(N_BRANCH * BRANCH, N_CHIPS, D_MODEL // N_CHIPS).transpose(1, 0, 2)
        g_out = g["wout"].reshape(N_CHIPS, D_MODEL // N_CHIPS, D_MODEL)
        above = [g_in, g_br, g_out]
        small_g.append(_small_grads(g))
    summed = own_half(above, _exchange_chips("halves_to_sibling", "halves", above))
    finish_reduce(summed, _exchange_chips("reduce_scatter_grads", "reduce_scatter", summed))
    small_g = jnp.stack(small_g[::-1])
    parts_in, parts_br, parts_out = parts_in[::-1], parts_br[::-1], parts_out[::-1]

    gathered = _allgather_devices("allgather_small", _pack([small_g, loss_blk[0, 0:1]]))
    summed = _sum_slots("sum_devices", gathered, gathered.shape[1]).reshape(-1)
    loss = summed[small_g.size]
    summed = summed[:small_g.size].reshape(small_g.shape)
    total, off = {}, 0
    for n, shape in _SMALL:
        size = shape[0] * (shape[1] if len(shape) > 1 else 1)
        total[n] = summed[:, off:off + size].reshape((DEPTH,) + shape)
        off += size
    for n, width in _SHARDED_SMALL.items():
        total[n] = lax.dynamic_slice_in_dim(total[n], chip * width, width, axis=2)

    names = list(total)
    packed = [_pack([d[n] for n in names]) for d in (weights, mom1, mom2)]
    res = _adamw("adamw_small", packed[0], packed[1], packed[2], [_pack([total[n] for n in names])], packed[0].shape[0])
    shapes = [weights[n].shape for n in names]
    grads, delta, new_m, new_v = [dict(zip(names, _unpack(r, shapes))) for r in res]
    big = (("w_in", parts_in, (DEPTH * D_MODEL, N_IN // N_CHIPS), 64),
           ("w_branch", parts_br, (DEPTH * N_BRANCH * BRANCH, D_MODEL // N_CHIPS), 512),
           ("w_out", parts_out, (DEPTH * D_MODEL // N_CHIPS, D_MODEL), 128))
    for n, parts, shape2, tile in big:
        res = _adamw("adamw_" + n, weights[n].reshape(shape2), mom1[n].reshape(shape2), mom2[n].reshape(shape2),
                     parts, tile)
        grads[n], delta[n], new_m[n], new_v[n] = [r.reshape(weights[n].shape) for r in res]

    outs = [loss, d_act[None]]
    for d in (grads, delta, new_m, new_v):
        outs += [d[n] for n in _WEIGHTS]
    return tuple(outs)
```
